```python
import jax, jax.numpy as jnp
from jax import lax
import numpy as np

D_MODEL = 1024
BATCH = 8
SEQ = 2048
DEPTH = 1
DEC_BATCH = 128
DEC_SEQ = 8
PAST_LEN = 16384
PAGE_SIZE = 128

RET_HEADS = 4
RET_DK = 128
RET_DV = 256
RET_QK = RET_HEADS * RET_DK
RET_V = RET_HEADS * RET_DV
GDN_HEADS = 8
GDN_DK = 128
GDN_DV = 128
GDN_QK = GDN_HEADS * GDN_DK
GDN_V = GDN_HEADS * GDN_DV
CONV_W = 4
CONV_CH = 2 * GDN_QK + GDN_V
D_FF = 4 * D_MODEL
CHUNK = 64
ROPE_BASE = 10000.0
EPS = 1e-6
IN_SIZES = (RET_QK, RET_QK, RET_V, RET_V, CONV_CH, GDN_HEADS, GDN_HEADS, GDN_V, D_MODEL, D_MODEL)
IN_W = RET_QK * 2 + RET_V * 2 + CONV_CH + 2 * GDN_HEADS + GDN_V + 2 * D_MODEL

kernel_name = 'hybrid_retention_gated_delta_decoder_step'


def _rmsnorm(x, g):
    x32 = x.astype(jnp.float32)
    y = x32 * lax.rsqrt(jnp.mean(x32 * x32, axis=-1, keepdims=True) + EPS) * g.astype(jnp.float32)
    return y.astype(x.dtype)


def _head_rmsnorm(o, g):
    B, T, H, D = o.shape
    y = o * lax.rsqrt(jnp.mean(o * o, axis=-1, keepdims=True) + EPS)
    return y.reshape(B, T, H * D) * g.astype(jnp.float32)


def _l2norm(x):
    return x * lax.rsqrt(jnp.sum(x * x, axis=-1, keepdims=True) + EPS)


def _rope(x, pos):
    D = x.shape[-1]
    inv = ROPE_BASE ** (-jnp.arange(0, D, 2, dtype=jnp.float32) / D)
    ang = pos[:, None] * inv[None, :]
    cos = jnp.cos(ang)[None, :, None, :]
    sin = jnp.sin(ang)[None, :, None, :]
    x1, x2 = x[..., : D // 2], x[..., D // 2:]
    return jnp.concatenate([x1 * cos - x2 * sin, x1 * sin + x2 * cos], axis=-1)


def _chunk_len(T):
    return CHUNK if T % CHUNK == 0 else T


def _to_chunks(x, C):
    B, T = x.shape[0], x.shape[1]
    y = x.reshape((B, T // C, C) + x.shape[2:])
    perm = (1, 0, 3, 2) + tuple(range(4, y.ndim))
    return y.transpose(perm)


def _from_chunks(o):
    N, B, H, C, D = o.shape
    return o.transpose(1, 0, 3, 2, 4).reshape(B, N * C, H, D)


def _retention(q, k, v, s0):
    T = q.shape[1]
    H = q.shape[2]
    C = _chunk_len(T)
    log_g = jnp.log1p(-jnp.exp2(-5.0 - jnp.arange(H, dtype=jnp.float32)))
    idx = jnp.arange(C, dtype=jnp.float32)
    diff = idx[:, None] - idx[None, :]
    causal = diff >= 0
    d_intra = jnp.where(causal[None], jnp.exp(log_g[:, None, None] * jnp.where(causal, diff, 0.0)[None]), 0.0)
    d_q = jnp.exp(log_g[:, None] * (idx + 1.0)[None, :])[..., None]
    d_k = jnp.exp(log_g[:, None] * (C - 1.0 - idx)[None, :])[..., None]
    d_c = jnp.exp(log_g * C)[:, None, None]

    def step(s, inp):
        qc, kc, vc = inp
        scores = jnp.einsum('bhid,bhjd->bhij', qc, kc) * d_intra
        o = jnp.einsum('bhij,bhjv->bhiv', scores, vc) + jnp.einsum('bhid,bhdv->bhiv', qc, s) * d_q
        s = s * d_c + jnp.einsum('bhjd,bhjv->bhdv', kc * d_k, vc)
        return s, o

    s, o = lax.scan(step, s0, (_to_chunks(q, C), _to_chunks(k, C), _to_chunks(v, C)))
    return _from_chunks(o), s


def _gated_delta(q, k, v, g, beta, s0):
    T = q.shape[1]
    Dv = v.shape[-1]
    C = _chunk_len(T)
    ids = jnp.arange(C)
    tril = ids[:, None] >= ids[None, :]
    strict = ids[:, None] > ids[None, :]

    def step(s, inp):
        qc, kc, vc, gc, bc = inp
        G = jnp.cumsum(gc, axis=-1)
        L = jnp.exp(jnp.where(tril, G[..., :, None] - G[..., None, :], -jnp.inf))
        kb = kc * bc[..., None]
        A = jnp.where(strict, jnp.einsum('bhid,bhjd->bhij', kb, kc) * L, 0.0)
        rhs = jnp.concatenate([vc * bc[..., None], kb * jnp.exp(G)[..., None]], axis=-1)
        sol = lax.linalg.triangular_solve(A, rhs, left_side=True, lower=True, unit_diagonal=True)
        u, w = sol[..., :Dv], sol[..., Dv:]
        v_new = u - jnp.einsum('bhcd,bhdv->bhcv', w, s)
        attn = jnp.einsum('bhid,bhjd->bhij', qc, kc) * L
        o = jnp.einsum('bhcd,bhdv->bhcv', qc * jnp.exp(G)[..., None], s) + jnp.einsum('bhij,bhjv->bhiv', attn, v_new)
        Gl = G[..., -1]
        s = s * jnp.exp(Gl)[..., None, None] + jnp.einsum('bhcd,bhcv->bhdv', kc * jnp.exp(Gl[..., None] - G)[..., None], v_new)
        return s, o

    g_c = g.reshape(g.shape[0], T // C, C, g.shape[2]).transpose(1, 0, 3, 2)
    b_c = beta.reshape(beta.shape[0], T // C, C, beta.shape[2]).transpose(1, 0, 3, 2)
    s, o = lax.scan(step, s0, (_to_chunks(q, C), _to_chunks(k, C), _to_chunks(v, C), g_c, b_c))
    return _from_chunks(o), s


def _causal_conv(x, buf, w):
    T = x.shape[1]
    xp = jnp.concatenate([buf.astype(x.dtype), x], axis=1)
    y = xp[:, 0:T] * w[0]
    for i in range(1, CONV_W):
        y = y + xp[:, i:i + T] * w[i]
    return jax.nn.silu(y), xp[:, -(CONV_W - 1):]


def _layer(x, pos, s_ret, s_gdn, buf, ln1, w_in, conv_w, a_log, dt_bias, ret_norm, gdn_norm,
           w_ret_br, w_gdn_br, w_out, ln2, w_up, w_down):
    B, T, _ = x.shape
    f32 = jnp.float32
    h = _rmsnorm(x, ln1)
    proj = jnp.einsum('btd,de->bte', h, w_in)
    parts = []
    off = 0
    for n in IN_SIZES:
        parts.append(proj[..., off:off + n])
        off += n
    q_r, k_r, v_r, g_r, qkv, a, b, z, gate_r, gate_g = parts

    q_r = _rope(q_r.astype(f32).reshape(B, T, RET_HEADS, RET_DK), pos)
    k_r = _rope(k_r.astype(f32).reshape(B, T, RET_HEADS, RET_DK), pos) * (RET_DK ** -0.5)
    v_r = v_r.astype(f32).reshape(B, T, RET_HEADS, RET_DV)
    o_r, s_ret_new = _retention(q_r, k_r, v_r, s_ret.astype(f32))
    o_r = (_head_rmsnorm(o_r, ret_norm) * jax.nn.silu(g_r.astype(f32))).astype(x.dtype)

    qkv_c, buf_new = _causal_conv(qkv, buf, conv_w)
    qkv_c = qkv_c.astype(f32)
    q_g = _l2norm(qkv_c[..., :GDN_QK].reshape(B, T, GDN_HEADS, GDN_DK)) * (GDN_DK ** -0.5)
    k_g = _l2norm(qkv_c[..., GDN_QK:2 * GDN_QK].reshape(B, T, GDN_HEADS, GDN_DK))
    v_g = qkv_c[..., 2 * GDN_QK:].reshape(B, T, GDN_HEADS, GDN_DV)
    g = -jnp.exp(a_log.astype(f32)) * jax.nn.softplus(a.astype(f32) + dt_bias.astype(f32))
    beta = jax.nn.sigmoid(b.astype(f32))
    o_g, s_gdn_new = _gated_delta(q_g, k_g, v_g, g, beta, s_gdn.astype(f32))
    o_g = (_head_rmsnorm(o_g, gdn_norm) * jax.nn.silu(z.astype(f32))).astype(x.dtype)

    br_r = jnp.einsum('bte,ed->btd', o_r, w_ret_br)
    br_g = jnp.einsum('bte,ed->btd', o_g, w_gdn_br)
    mix = jax.nn.sigmoid(gate_r) * br_r + jax.nn.sigmoid(gate_g) * br_g
    x = x + jnp.einsum('btd,de->bte', mix, w_out)

    h2 = _rmsnorm(x, ln2)
    x = x + jnp.einsum('btf,fd->btd', jnp.square(jax.nn.relu(jnp.einsum('btd,df->btf', h2, w_up))), w_down)
    return x, s_ret_new.astype(s_ret.dtype), s_gdn_new.astype(s_gdn.dtype), buf_new.astype(buf.dtype)


def setup_inputs(seed: int = 0) -> dict:
    key = jax.random.key(seed)
    ks = jax.random.split(key, 20)
    f32 = jnp.float32
    nrm = lambda k, shape, s: jax.random.normal(k, shape, f32) * s
    dt = jnp.exp(jax.random.uniform(ks[9], (DEPTH, GDN_HEADS), f32, np.log(1e-3), np.log(1e-1)))
    return {
        'x_prompt': nrm(ks[0], (BATCH, SEQ, D_MODEL), 1.0),
        'x_sample': nrm(ks[1], (DEC_BATCH, DEC_SEQ, D_MODEL), 1.0),
        'state_ret': nrm(ks[2], (DEPTH, DEC_BATCH, RET_HEADS, RET_DK, RET_DV), 0.3),
        'state_gdn': nrm(ks[3], (DEPTH, DEC_BATCH, GDN_HEADS, GDN_DK, GDN_DV), 0.1),
        'state_conv': nrm(ks[4], (DEPTH, DEC_BATCH, CONV_W - 1, CONV_CH), 1.0),
        'ln1': 1.0 + nrm(ks[5], (DEPTH, D_MODEL), 0.01),
        'w_in': nrm(ks[6], (DEPTH, D_MODEL, IN_W), D_MODEL ** -0.5),
        'conv_w': nrm(ks[7], (DEPTH, CONV_W, CONV_CH), CONV_W ** -0.5),
        'a_log': jnp.log(jax.random.uniform(ks[8], (DEPTH, GDN_HEADS), f32, 1.0, 16.0)),
        'dt_bias': dt + jnp.log(-jnp.expm1(-dt)),
        'ret_norm': 1.0 + nrm(ks[10], (DEPTH, RET_V), 0.01),
        'gdn_norm': 1.0 + nrm(ks[11], (DEPTH, GDN_V), 0.01),
        'w_ret_br': nrm(ks[12], (DEPTH, RET_V, D_MODEL), RET_V ** -0.5),
        'w_gdn_br': nrm(ks[13], (DEPTH, GDN_V, D_MODEL), GDN_V ** -0.5),
        'w_out': nrm(ks[14], (DEPTH, D_MODEL, D_MODEL), D_MODEL ** -0.5),
        'ln2': 1.0 + nrm(ks[15], (DEPTH, D_MODEL), 0.01),
        'w_up': nrm(ks[16], (DEPTH, D_MODEL, D_FF), D_MODEL ** -0.5),
        'w_down': nrm(ks[17], (DEPTH, D_FF, D_MODEL), D_FF ** -0.5),
        'ln_f': 1.0 + nrm(ks[18], (D_MODEL,), 0.01),
    }


def reference(x_prompt, x_sample, state_ret, state_gdn, state_conv, ln1, w_in, conv_w, a_log, dt_bias,
              ret_norm, gdn_norm, w_ret_br, w_gdn_br, w_out, ln2, w_up, w_down, ln_f):
    Bp, Tp, _ = x_prompt.shape
    Ts = x_sample.shape[1]
    pos_p = jnp.arange(Tp, dtype=jnp.float32)
    pos_s = PAST_LEN + jnp.arange(Ts, dtype=jnp.float32)
    yp, ys = x_prompt, x_sample
    ret_p, gdn_p, conv_p, ret_s, gdn_s, conv_s = [], [], [], [], [], []
    for l in range(DEPTH):
        w = (ln1[l], w_in[l], conv_w[l], a_log[l], dt_bias[l], ret_norm[l], gdn_norm[l],
             w_ret_br[l], w_gdn_br[l], w_out[l], ln2[l], w_up[l], w_down[l])
        z_ret = jnp.zeros((Bp, RET_HEADS, RET_DK, RET_DV), state_ret.dtype)
        z_gdn = jnp.zeros((Bp, GDN_HEADS, GDN_DK, GDN_DV), state_gdn.dtype)
        z_conv = jnp.zeros((Bp, CONV_W - 1, CONV_CH), state_conv.dtype)
        yp, r, g, c = _layer(yp, pos_p, z_ret, z_gdn, z_conv, *w)
        ret_p.append(r); gdn_p.append(g); conv_p.append(c)
        ys, r, g, c = _layer(ys, pos_s, state_ret[l], state_gdn[l], state_conv[l], *w)
        ret_s.append(r); gdn_s.append(g); conv_s.append(c)
    y_prompt = _rmsnorm(yp, ln_f)
    y_sample = _rmsnorm(ys, ln_f)
    new_ret_prompt = jnp.stack(ret_p)
    new_gdn_prompt = jnp.stack(gdn_p)
    new_conv_prompt = jnp.stack(conv_p)
    new_ret_sample = jnp.stack(ret_s)
    new_gdn_sample = jnp.stack(gdn_s)
    new_conv_sample = jnp.stack(conv_s)
    return (y_prompt, y_sample, new_ret_prompt, new_gdn_prompt, new_conv_prompt, new_ret_sample, new_gdn_sample, new_conv_sample)
```

```python
import functools

import jax
import jax.numpy as jnp
from jax import lax
from jax.experimental import pallas as pl
from jax.experimental.pallas import tpu as pltpu

F32 = jnp.float32
BF16 = jnp.bfloat16

D_MODEL = 1024
PAST_LEN = 16384
RET_HEADS, RET_DK, RET_DV = 4, 128, 256
RET_QK = RET_HEADS * RET_DK
RET_V = RET_HEADS * RET_DV
GDN_HEADS, GDN_DK, GDN_DV = 8, 128, 128
GDN_QK = GDN_HEADS * GDN_DK
GDN_V = GDN_HEADS * GDN_DV
CONV_W = 4
CONV_CH = 2 * GDN_QK + GDN_V
D_FF = 4 * D_MODEL
CHUNK = 64
ROPE_BASE = 10000.0
EPS = 1e-6

N_MAIN = 2 * RET_QK + 2 * RET_V + CONV_CH + GDN_V + 2 * D_MODEL
AB_W = 256
SUBLANES = 8
LANES = 128
VMEM_LIMIT = 56 * 1024 * 1024


def _params(sem):
    return pltpu.CompilerParams(dimension_semantics=sem, vmem_limit_bytes=VMEM_LIMIT)


def _const_spec(shape):
    zeros = (0,) * len(shape)
    return pl.BlockSpec(shape, lambda *_: zeros, pipeline_mode=pl.Buffered(1))


def _dot(a, b):
    return jnp.dot(a, b, preferred_element_type=F32)


def _dot_nt(a, b):
    return lax.dot_general(a, b, (((1,), (1,)), ((), ())), preferred_element_type=F32)


def _dot_tn(a, b):
    return lax.dot_general(a, b, (((0,), (0,)), ((), ())), preferred_element_type=F32)


def _rms(x, g):
    return x * lax.rsqrt(jnp.mean(x * x, axis=-1, keepdims=True) + EPS) * g


def _in_proj_kernel(x_ref, ln_ref, w_ref, wab_ref, o_ref, oab_ref, h_ref):
    @pl.when(pl.program_id(1) == 0)
    def _():
        hb = _rms(x_ref[...], ln_ref[...]).astype(BF16)
        h_ref[...] = hb
        oab_ref[...] = _dot(hb, wab_ref[...])

    o_ref[...] = _dot(h_ref[...], w_ref[...])


def _in_proj(x2d, ln1, w_main, w_ab):
    n = x2d.shape[0]
    tm = min(1024, n)
    tn = 1536
    return pl.pallas_call(
        _in_proj_kernel,
        grid=(n // tm, N_MAIN // tn),
        in_specs=[
            pl.BlockSpec((tm, D_MODEL), lambda i, j: (i, 0)),
            _const_spec((1, D_MODEL)),
            pl.BlockSpec((D_MODEL, tn), lambda i, j: (0, j)),
            _const_spec((D_MODEL, AB_W)),
        ],
        out_specs=[
            pl.BlockSpec((tm, tn), lambda i, j: (i, j)),
            pl.BlockSpec((tm, AB_W), lambda i, j: (i, 0)),
        ],
        out_shape=[jax.ShapeDtypeStruct((n, N_MAIN), F32), jax.ShapeDtypeStruct((n, AB_W), F32)],
        scratch_shapes=[pltpu.VMEM((tm, D_MODEL), BF16)],
        compiler_params=_params(("arbitrary", "arbitrary")),
        name="in_proj",
    )(x2d, ln1, w_main, w_ab)


def _retention_kernel(q_ref, k_ref, v_ref, cos_ref, sin_ref, dintra_ref, dq_ref, dk_ref, dc_ref, s0_ref,
                      o_ref, s_ref):
    @pl.when(pl.program_id(1) == 0)
    def _():
        s_ref[...] = s0_ref[...]

    cosf = cos_ref[...]
    sinf = sin_ref[...]
    for h in range(RET_HEADS):
        q = q_ref[:, h * RET_DK:(h + 1) * RET_DK]
        k = k_ref[:, h * RET_DK:(h + 1) * RET_DK]
        v = v_ref[:, h * RET_DV:(h + 1) * RET_DV]
        qr = q * cosf + pltpu.roll(q, RET_DK // 2, 1) * sinf
        kr = (k * cosf + pltpu.roll(k, RET_DK // 2, 1) * sinf) * (RET_DK ** -0.5)
        qb = qr.astype(BF16)
        vb = v.astype(BF16)
        s = s_ref[0, h]
        scores = _dot_nt(qb, kr.astype(BF16)) * dintra_ref[h]
        o = _dot(scores.astype(BF16), vb) + _dot(qb, s.astype(BF16)) * dq_ref[h]
        o_ref[:, h * RET_DV:(h + 1) * RET_DV] = o
        s_ref[0, h] = s * dc_ref[h] + _dot_tn((kr * dk_ref[h]).astype(BF16), vb)


def _retention(proj, s0, pos, c, nb):
    n = proj.shape[0]
    nc = n // (nb * c)
    inv = ROPE_BASE ** (-jnp.arange(0, RET_DK, 2, dtype=F32) / RET_DK)
    ang = pos[:, None] * inv[None, :]
    cosf = jnp.concatenate([jnp.cos(ang), jnp.cos(ang)], axis=-1)
    sinf = jnp.concatenate([-jnp.sin(ang), jnp.sin(ang)], axis=-1)
    log_g = jnp.log1p(-jnp.exp2(-5.0 - jnp.arange(RET_HEADS, dtype=F32)))
    idx = jnp.arange(c, dtype=F32)
    diff = idx[:, None] - idx[None, :]
    causal = diff >= 0
    d_intra = jnp.where(causal[None], jnp.exp(log_g[:, None, None] * jnp.where(causal, diff, 0.0)[None]), 0.0)
    d_q = jnp.broadcast_to(jnp.exp(log_g[:, None] * (idx + 1.0)[None, :])[..., None], (RET_HEADS, c, RET_DV))
    d_k = jnp.broadcast_to(jnp.exp(log_g[:, None] * (c - 1.0 - idx)[None, :])[..., None], (RET_HEADS, c, RET_DK))
    d_c = jnp.broadcast_to(jnp.exp(log_g * c)[:, None, None], (RET_HEADS, 1, RET_DV))
    row = lambda b, i: b * nc + i
    state_spec = pl.BlockSpec((1, RET_HEADS, RET_DK, RET_DV), lambda b, i: (b, 0, 0, 0))
    return pl.pallas_call(
        _retention_kernel,
        grid=(nb, nc),
        in_specs=[
            pl.BlockSpec((c, RET_QK), lambda b, i: (row(b, i), 0)),
            pl.BlockSpec((c, RET_QK), lambda b, i: (row(b, i), 1)),
            pl.BlockSpec((c, RET_V), lambda b, i: (row(b, i), 1)),
            pl.BlockSpec((c, RET_DK), lambda b, i: (i, 0)),
            pl.BlockSpec((c, RET_DK), lambda b, i: (i, 0)),
            _const_spec((RET_HEADS, c, c)),
            _const_spec((RET_HEADS, c, RET_DV)),
            _const_spec((RET_HEADS, c, RET_DK)),
            _const_spec((RET_HEADS, 1, RET_DV)),
            state_spec,
        ],
        out_specs=[pl.BlockSpec((c, RET_V), lambda b, i: (row(b, i), 0)), state_spec],
        out_shape=[jax.ShapeDtypeStruct((n, RET_V), F32), jax.ShapeDtypeStruct(s0.shape, F32)],
        compiler_params=_params(("arbitrary", "arbitrary")),
        name="retention",
    )(proj, proj, proj, cosf, sinf, d_intra, d_q, d_k, d_c, s0)


def _split(x):
    hi = x.astype(BF16)
    return hi, (x - hi.astype(F32)).astype(BF16)


def _dot3(a, b):
    ah, al = _split(a)
    bh, bl = _split(b)
    return _dot(ah, bh) + (_dot(al, bh) + _dot(ah, bl))


def _inv_unit_lower(a, c):
    r = lax.broadcasted_iota(jnp.int32, (c, c), 0)
    col = lax.broadcasted_iota(jnp.int32, (c, c), 1)
    t = jnp.where(r == col, 1.0, 0.0) - a
    p = a
    span = 2
    while span < c:
        p = _dot3(p, p)
        t = t + _dot3(t, p)
        span *= 2
    return t


def _gdn_kernel(qkv_ref, ab_ref, cw_ref, alog_ref, dtb_ref, s0_ref, c0_ref, o_ref, s_ref, cfin_ref, ext_ref,
                *, c):
    @pl.when(pl.program_id(1) == 0)
    def _():
        s_ref[...] = s0_ref[...]
        ext_ref[0:SUBLANES, :] = c0_ref[0]

    ext_ref[SUBLANES:SUBLANES + c, :] = qkv_ref[...]

    def conv(col):
        sl = slice(col, col + LANES)
        base = SUBLANES - (CONV_W - 1)
        y = ext_ref[base:base + c, sl] * cw_ref[0:1, sl]
        for i in range(1, CONV_W):
            y = y + ext_ref[base + i:base + i + c, sl] * cw_ref[i:i + 1, sl]
        return y * jax.nn.sigmoid(y)

    lane = lax.broadcasted_iota(jnp.int32, (c, LANES), 1)
    row = lax.broadcasted_iota(jnp.int32, (c, LANES), 0)
    g_all = -jnp.exp(alog_ref[...]) * jax.nn.softplus(ab_ref[:, 0:LANES] + dtb_ref[...])
    g_all = jnp.where(lane < GDN_HEADS, g_all, 0.0)
    beta_all = jax.nn.sigmoid(ab_ref[:, LANES:2 * LANES])
    cum = g_all
    shift = 1
    while shift < c:
        cum = cum + jnp.where(row >= shift, pltpu.roll(cum, shift, 0), 0.0)
        shift *= 2
    cum_t = cum.T
    last = cum[c - 1:c, :]
    e_cum = jnp.exp(cum)
    e_rest = jnp.exp(last - cum)
    e_last = jnp.exp(last)

    r = lax.broadcasted_iota(jnp.int32, (c, c), 0)
    cc = lax.broadcasted_iota(jnp.int32, (c, c), 1)
    tril = r >= cc
    strict = r > cc

    for h in range(GDN_HEADS):
        q = conv(h * GDN_DK)
        k = conv(GDN_QK + h * GDN_DK)
        v = conv(2 * GDN_QK + h * GDN_DV)
        q = q * lax.rsqrt(jnp.sum(q * q, axis=-1, keepdims=True) + EPS) * (GDN_DK ** -0.5)
        k = k * lax.rsqrt(jnp.sum(k * k, axis=-1, keepdims=True) + EPS)
        beta = beta_all[:, h:h + 1]
        eg = e_cum[:, h:h + 1]
        decay = jnp.exp(jnp.where(tril, cum[:, h:h + 1] - cum_t[h:h + 1, :], -jnp.inf))
        kb = k * beta
        kbf = k.astype(BF16)
        a = jnp.where(strict, _dot_nt(kb.astype(BF16), kbf) * decay, 0.0)
        t = _inv_unit_lower(a, c)
        u = _dot3(t, v * beta)
        w = _dot3(t, kb * eg)
        s = s_ref[0, h]
        sb = s.astype(BF16)
        v_new = u - _dot(w.astype(BF16), sb)
        vnb = v_new.astype(BF16)
        qbf = q.astype(BF16)
        attn = _dot_nt(qbf, kbf) * decay
        o = _dot((q * eg).astype(BF16), sb) + _dot(attn.astype(BF16), vnb)
        o_ref[:, h * GDN_DV:(h + 1) * GDN_DV] = o
        s_ref[0, h] = s * e_last[:, h:h + 1] + _dot_tn((k * e_rest[:, h:h + 1]).astype(BF16), vnb)

    cfin_ref[0] = ext_ref[c:c + SUBLANES, :]
    ext_ref[0:SUBLANES, :] = ext_ref[c:c + SUBLANES, :]


def _gdn(proj, ab, conv_w, a_log, dt_bias, s0, conv0, c, nb):
    n = proj.shape[0]
    nc = n // (nb * c)
    pad_lanes = lambda x: jnp.pad(x.reshape(1, GDN_HEADS), ((0, 0), (0, LANES - GDN_HEADS)))
    conv0p = jnp.pad(conv0, ((0, 0), (SUBLANES - (CONV_W - 1), 0), (0, 0)))
    row = lambda b, i: b * nc + i
    state_spec = pl.BlockSpec((1, GDN_HEADS, GDN_DK, GDN_DV), lambda b, i: (b, 0, 0, 0))
    conv_spec = pl.BlockSpec((1, SUBLANES, CONV_CH), lambda b, i: (b, 0, 0))
    o, s_fin, c_fin = pl.pallas_call(
        functools.partial(_gdn_kernel, c=c),
        grid=(nb, nc),
        in_specs=[
            pl.BlockSpec((c, CONV_CH), lambda b, i: (row(b, i), 1)),
            pl.BlockSpec((c, AB_W), lambda b, i: (row(b, i), 0)),
            _const_spec((CONV_W, CONV_CH)),
            _const_spec((1, LANES)),
            _const_spec((1, LANES)),
            state_spec,
            conv_spec,
        ],
        out_specs=[pl.BlockSpec((c, GDN_V), lambda b, i: (row(b, i), 0)), state_spec, conv_spec],
        out_shape=[
            jax.ShapeDtypeStruct((n, GDN_V), F32),
            jax.ShapeDtypeStruct(s0.shape, F32),
            jax.ShapeDtypeStruct(conv0p.shape, F32),
        ],
        scratch_shapes=[pltpu.VMEM((c + SUBLANES, CONV_CH), F32)],
        compiler_params=_params(("arbitrary", "arbitrary")),
        name="gdn",
    )(proj, ab, conv_w, pad_lanes(a_log), pad_lanes(dt_bias), s0, conv0p)
    return o, s_fin, c_fin[:, SUBLANES - (CONV_W - 1):, :]


def _head_norm(o, heads):
    d = o.shape[-1] // heads
    parts = []
    for h in range(heads):
        oh = o[:, h * d:(h + 1) * d]
        parts.append(oh * lax.rsqrt(jnp.mean(oh * oh, axis=-1, keepdims=True) + EPS))
    return jnp.concatenate(parts, axis=-1)


def _attn_out_kernel(x_ref, or_ref, og_ref, gr_ref, z_ref, gater_ref, gateg_ref, rn_ref, gn_ref,
                     wr_ref, wg_ref, wo_ref, o_ref):
    gr = gr_ref[...]
    o_r = _head_norm(or_ref[...], RET_HEADS) * rn_ref[...] * (gr * jax.nn.sigmoid(gr))
    br_r = _dot(o_r.astype(BF16), wr_ref[...])
    z = z_ref[...]
    o_g = _head_norm(og_ref[...], GDN_HEADS) * gn_ref[...] * (z * jax.nn.sigmoid(z))
    br_g = _dot(o_g.astype(BF16), wg_ref[...])
    mix = jax.nn.sigmoid(gater_ref[...]) * br_r + jax.nn.sigmoid(gateg_ref[...]) * br_g
    o_ref[...] = x_ref[...] + _dot(mix.astype(BF16), wo_ref[...])


def _attn_out(x2d, o_r, o_g, proj, ret_norm, gdn_norm, w_ret_br, w_gdn_br, w_out):
    n = x2d.shape[0]
    tm = min(512, n)
    rows = lambda blk: pl.BlockSpec((tm, D_MODEL), lambda i: (i, blk))
    wspec = _const_spec((D_MODEL, D_MODEL))
    return pl.pallas_call(
        _attn_out_kernel,
        grid=(n // tm,),
        in_specs=[rows(0), rows(0), rows(0), rows(2), rows(6), rows(7), rows(8),
                  _const_spec((1, D_MODEL)), _const_spec((1, D_MODEL)), wspec, wspec, wspec],
        out_specs=rows(0),
        out_shape=jax.ShapeDtypeStruct((n, D_MODEL), F32),
        compiler_params=_params(("arbitrary",)),
        name="attn_out",
    )(x2d, o_r, o_g, proj, proj, proj, proj, ret_norm, gdn_norm, w_ret_br, w_gdn_br, w_out)


FF_BLOCK = 1024


def _mlp_kernel(x_ref, ln2_ref, wu_ref, wd_ref, lnf_ref, o_ref):
    x = x_ref[...]
    hb = _rms(x, ln2_ref[...]).astype(BF16)
    acc = x
    for f in range(D_FF // FF_BLOCK):
        sl = slice(f * FF_BLOCK, (f + 1) * FF_BLOCK)
        up = jnp.maximum(_dot(hb, wu_ref[:, sl]), 0.0)
        acc = acc + _dot((up * up).astype(BF16), wd_ref[sl, :])
    o_ref[...] = _rms(acc, lnf_ref[...])


def _mlp(x2d, ln2, w_up, w_down, ln_f):
    n = x2d.shape[0]
    tm = min(512, n)
    rows = pl.BlockSpec((tm, D_MODEL), lambda i: (i, 0))
    return pl.pallas_call(
        _mlp_kernel,
        grid=(n // tm,),
        in_specs=[rows, _const_spec((1, D_MODEL)), _const_spec((D_MODEL, D_FF)), _const_spec((D_FF, D_MODEL)),
                  _const_spec((1, D_MODEL))],
        out_specs=rows,
        out_shape=jax.ShapeDtypeStruct((n, D_MODEL), F32),
        compiler_params=_params(("arbitrary",)),
        name="mlp",
    )(x2d, ln2, w_up, w_down, ln_f)


def _group(x, pos, s_ret, s_gdn, s_conv, wts):
    (ln1, w_main, w_ab, conv_w, a_log, dt_bias, ret_norm, gdn_norm, w_ret_br, w_gdn_br, w_out, ln2, w_up, w_down,
     ln_f) = wts
    nb, t, _ = x.shape
    c = CHUNK if t % CHUNK == 0 else t
    x2d = x.reshape(nb * t, D_MODEL)
    proj, ab = _in_proj(x2d, ln1, w_main, w_ab)
    o_r, ret_new = _retention(proj, s_ret, pos, c, nb)
    o_g, gdn_new, conv_new = _gdn(proj, ab, conv_w, a_log, dt_bias, s_gdn, s_conv, c, nb)
    x1 = _attn_out(x2d, o_r, o_g, proj, ret_norm, gdn_norm, w_ret_br, w_gdn_br, w_out)
    y = _mlp(x1, ln2, w_up, w_down, ln_f)
    return y.reshape(x.shape), ret_new, gdn_new, conv_new


def kernel(x_prompt, x_sample, state_ret, state_gdn, state_conv, ln1, w_in, conv_w, a_log, dt_bias, ret_norm,
           gdn_norm, w_ret_br, w_gdn_br, w_out, ln2, w_up, w_down, ln_f):
    depth = w_in.shape[0]
    assert depth == 1, "single-layer trunk"
    bp, tp, _ = x_prompt.shape
    ts = x_sample.shape[1]
    w = w_in[0]
    ab0 = 2 * RET_QK + 2 * RET_V + CONV_CH
    ab1 = ab0 + 2 * GDN_HEADS
    w_main = jnp.concatenate([w[:, :ab0], w[:, ab1:]], axis=1).astype(BF16)
    w_ab = jnp.zeros((D_MODEL, AB_W), F32)
    w_ab = w_ab.at[:, :GDN_HEADS].set(w[:, ab0:ab0 + GDN_HEADS])
    w_ab = w_ab.at[:, LANES:LANES + GDN_HEADS].set(w[:, ab0 + GDN_HEADS:ab1]).astype(BF16)
    vec = lambda v: v.reshape(1, -1)
    wts = (vec(ln1[0]), w_main, w_ab, conv_w[0], a_log[0], dt_bias[0], vec(ret_norm[0]), vec(gdn_norm[0]),
           w_ret_br[0].astype(BF16), w_gdn_br[0].astype(BF16), w_out[0].astype(BF16), vec(ln2[0]),
           w_up[0].astype(BF16), w_down[0].astype(BF16), vec(ln_f))
    pos_p = jnp.arange(tp, dtype=F32)
    pos_s = PAST_LEN + jnp.arange(ts, dtype=F32)
    zeros = lambda *shape: jnp.zeros((bp,) + shape, F32)
    yp, ret_p, gdn_p, conv_p = _group(x_prompt, pos_p, zeros(RET_HEADS, RET_DK, RET_DV),
                                      zeros(GDN_HEADS, GDN_DK, GDN_DV), zeros(CONV_W - 1, CONV_CH), wts)
    ys, ret_s, gdn_s, conv_s = _group(x_sample, pos_s, state_ret[0], state_gdn[0], state_conv[0], wts)
    return (yp, ys, ret_p[None], gdn_p[None], conv_p[None], ret_s[None], gdn_s[None], conv_s[None])
```

```python
import functools

import jax
import jax.numpy as jnp
from jax import lax
from jax.experimental import pallas as pl
from jax.experimental.pallas import tpu as pltpu

F32 = jnp.float32
BF16 = jnp.bfloat16

D_MODEL = 1024
PAST_LEN = 16384
RET_HEADS, RET_DK, RET_DV = 4, 128, 256
RET_QK = RET_HEADS * RET_DK
RET_V = RET_HEADS * RET_DV
GDN_HEADS, GDN_DK, GDN_DV = 8, 128, 128
GDN_QK = GDN_HEADS * GDN_DK
GDN_V = GDN_HEADS * GDN_DV
CONV_W = 4
CONV_CH = 2 * GDN_QK + GDN_V
D_FF = 4 * D_MODEL
CHUNK = 64
ROPE_BASE = 10000.0
EPS = 1e-6

N_MAIN = 2 * RET_QK + 2 * RET_V + CONV_CH + GDN_V + 2 * D_MODEL
AB_W = 256
SUBLANES = 8
LANES = 128
VMEM_LIMIT = 56 * 1024 * 1024


def _params(sem):
    return pltpu.CompilerParams(dimension_semantics=sem, vmem_limit_bytes=VMEM_LIMIT)


def _const_spec(shape):
    zeros = (0,) * len(shape)
    return pl.BlockSpec(shape, lambda *_: zeros, pipeline_mode=pl.Buffered(1))


def _dot(a, b):
    return jnp.dot(a, b, preferred_element_type=F32)


def _dot_nt(a, b):
    return lax.dot_general(a, b, (((1,), (1,)), ((), ())), preferred_element_type=F32)


def _dot_tn(a, b):
    return lax.dot_general(a, b, (((0,), (0,)), ((), ())), preferred_element_type=F32)


def _bf(x):
    return x.astype(BF16)


def _rows(parts):
    return jnp.concatenate(parts, axis=0)


def _rms(x, g):
    return x * lax.rsqrt(jnp.mean(x * x, axis=-1, keepdims=True) + EPS) * g


def _in_proj_kernel(x_ref, ln_ref, w_ref, wab_ref, o_ref, oab_ref, h_ref):
    @pl.when(pl.program_id(1) == 0)
    def _():
        hb = _rms(x_ref[...], ln_ref[...]).astype(BF16)
        h_ref[...] = hb
        oab_ref[...] = _dot(hb, wab_ref[...])

    o_ref[...] = _dot(h_ref[...], w_ref[...])


def _in_proj(x2d, ln1, w_main, w_ab):
    n = x2d.shape[0]
    tm = min(1024, n)
    tn = 1536
    return pl.pallas_call(
        _in_proj_kernel,
        grid=(n // tm, N_MAIN // tn),
        in_specs=[
            pl.BlockSpec((tm, D_MODEL), lambda i, j: (i, 0)),
            _const_spec((1, D_MODEL)),
            pl.BlockSpec((D_MODEL, tn), lambda i, j: (0, j)),
            _const_spec((D_MODEL, AB_W)),
        ],
        out_specs=[
            pl.BlockSpec((tm, tn), lambda i, j: (i, j)),
            pl.BlockSpec((tm, AB_W), lambda i, j: (i, 0)),
        ],
        out_shape=[jax.ShapeDtypeStruct((n, N_MAIN), F32), jax.ShapeDtypeStruct((n, AB_W), F32)],
        scratch_shapes=[pltpu.VMEM((tm, D_MODEL), BF16)],
        compiler_params=_params(("arbitrary", "arbitrary")),
        name="in_proj",
    )(x2d, ln1, w_main, w_ab)


def _retention_kernel(q_ref, k_ref, v_ref, cos_ref, sin_ref, dintra_ref, dq_ref, dk_ref, dc_ref, s0_ref,
                      o_ref, s_ref):
    @pl.when(pl.program_id(1) == 0)
    def _():
        s_ref[...] = s0_ref[...]

    cosf = cos_ref[...]
    sinf = sin_ref[...]

    def rope(ref, h):
        x = ref[:, h * RET_DK:(h + 1) * RET_DK]
        return x * cosf + pltpu.roll(x, RET_DK // 2, 1) * sinf

    heads = range(RET_HEADS)
    qb = [_bf(rope(q_ref, h)) for h in heads]
    k = [rope(k_ref, h) * (RET_DK ** -0.5) for h in heads]
    vb = [_bf(v_ref[:, h * RET_DV:(h + 1) * RET_DV]) for h in heads]
    scores = [_dot_nt(qb[h], _bf(k[h])) * dintra_ref[h] for h in heads]
    s = [s_ref[0, h] for h in heads]
    qs = [_dot(qb[h], _bf(s[h])) for h in heads]
    kdt = [(k[h] * dk_ref[h]).T for h in heads]
    upd = [_dot(_bf(_rows([kdt[h], scores[h]])), vb[h]) for h in heads]
    for h in heads:
        o_ref[:, h * RET_DV:(h + 1) * RET_DV] = upd[h][RET_DK:] + qs[h] * dq_ref[h]
        s_ref[0, h] = s[h] * dc_ref[h] + upd[h][:RET_DK]


def _retention(proj, s0, pos, c, nb):
    n = proj.shape[0]
    nc = n // (nb * c)
    inv = ROPE_BASE ** (-jnp.arange(0, RET_DK, 2, dtype=F32) / RET_DK)
    ang = pos[:, None] * inv[None, :]
    cosf = jnp.concatenate([jnp.cos(ang), jnp.cos(ang)], axis=-1)
    sinf = jnp.concatenate([-jnp.sin(ang), jnp.sin(ang)], axis=-1)
    log_g = jnp.log1p(-jnp.exp2(-5.0 - jnp.arange(RET_HEADS, dtype=F32)))
    idx = jnp.arange(c, dtype=F32)
    diff = idx[:, None] - idx[None, :]
    causal = diff >= 0
    d_intra = jnp.where(causal[None], jnp.exp(log_g[:, None, None] * jnp.where(causal, diff, 0.0)[None]), 0.0)
    d_q = jnp.broadcast_to(jnp.exp(log_g[:, None] * (idx + 1.0)[None, :])[..., None], (RET_HEADS, c, RET_DV))
    d_k = jnp.broadcast_to(jnp.exp(log_g[:, None] * (c - 1.0 - idx)[None, :])[..., None], (RET_HEADS, c, RET_DK))
    d_c = jnp.broadcast_to(jnp.exp(log_g * c)[:, None, None], (RET_HEADS, 1, RET_DV))
    row = lambda b, i: b * nc + i
    state_spec = pl.BlockSpec((1, RET_HEADS, RET_DK, RET_DV), lambda b, i: (b, 0, 0, 0))
    return pl.pallas_call(
        _retention_kernel,
        grid=(nb, nc),
        in_specs=[
            pl.BlockSpec((c, RET_QK), lambda b, i: (row(b, i), 0)),
            pl.BlockSpec((c, RET_QK), lambda b, i: (row(b, i), 1)),
            pl.BlockSpec((c, RET_V), lambda b, i: (row(b, i), 1)),
            pl.BlockSpec((c, RET_DK), lambda b, i: (i, 0)),
            pl.BlockSpec((c, RET_DK), lambda b, i: (i, 0)),
            _const_spec((RET_HEADS, c, c)),
            _const_spec((RET_HEADS, c, RET_DV)),
            _const_spec((RET_HEADS, c, RET_DK)),
            _const_spec((RET_HEADS, 1, RET_DV)),
            state_spec,
        ],
        out_specs=[pl.BlockSpec((c, RET_V), lambda b, i: (row(b, i), 0)), state_spec],
        out_shape=[jax.ShapeDtypeStruct((n, RET_V), F32), jax.ShapeDtypeStruct(s0.shape, F32)],
        compiler_params=_params(("arbitrary", "arbitrary")),
        name="retention",
    )(proj, proj, proj, cosf, sinf, d_intra, d_q, d_k, d_c, s0)


def _solve_correction(a, c):
    heads = range(len(a))
    n = [-x for x in a]
    ab = [_bf(x) for x in a]
    p = [_dot(ab[h], ab[h]) for h in heads]
    span = 2
    while span < c:
        pb = [_bf(x) for x in p]
        last = 2 * span >= c
        if last:
            prod = [_dot(_bf(n[h]), pb[h]) for h in heads]
            n = [n[h] + p[h] + prod[h] for h in heads]
        else:
            prod = [_dot(_bf(_rows([n[h], p[h]])), pb[h]) for h in heads]
            n = [n[h] + p[h] + prod[h][:c] for h in heads]
            p = [prod[h][c:] for h in heads]
        span *= 2
    return n


def _gdn_kernel(qkv_ref, ab_ref, cw_ref, alog_ref, dtb_ref, s0_ref, c0_ref, o_ref, s_ref, cfin_ref, ext_ref,
                *, c):
    @pl.when(pl.program_id(1) == 0)
    def _():
        s_ref[...] = s0_ref[...]
        ext_ref[0:SUBLANES, :] = c0_ref[0]

    ext_ref[SUBLANES:SUBLANES + c, :] = qkv_ref[...]

    def conv(col):
        sl = slice(col, col + LANES)
        base = SUBLANES - (CONV_W - 1)
        y = ext_ref[base:base + c, sl] * cw_ref[0:1, sl]
        for i in range(1, CONV_W):
            y = y + ext_ref[base + i:base + i + c, sl] * cw_ref[i:i + 1, sl]
        return y * jax.nn.sigmoid(y)

    def l2norm(x):
        return x * lax.rsqrt(jnp.sum(x * x, axis=-1, keepdims=True) + EPS)

    lane = lax.broadcasted_iota(jnp.int32, (c, LANES), 1)
    row = lax.broadcasted_iota(jnp.int32, (c, LANES), 0)
    g_all = -jnp.exp(alog_ref[...]) * jax.nn.softplus(ab_ref[:, 0:LANES] + dtb_ref[...])
    g_all = jnp.where(lane < GDN_HEADS, g_all, 0.0)
    beta_all = jax.nn.sigmoid(ab_ref[:, LANES:2 * LANES])
    cum = g_all
    shift = 1
    while shift < c:
        cum = cum + jnp.where(row >= shift, pltpu.roll(cum, shift, 0), 0.0)
        shift *= 2
    cum_t = cum.T
    last = cum[c - 1:c, :]
    e_cum = jnp.exp(cum)
    e_rest = jnp.exp(last - cum)
    e_last = jnp.exp(last)

    r = lax.broadcasted_iota(jnp.int32, (c, c), 0)
    cc = lax.broadcasted_iota(jnp.int32, (c, c), 1)
    tril = r >= cc
    strict = r > cc

    heads = range(GDN_HEADS)
    q = [l2norm(conv(h * GDN_DK)) * (GDN_DK ** -0.5) for h in heads]
    k = [l2norm(conv(GDN_QK + h * GDN_DK)) for h in heads]
    v = [conv(2 * GDN_QK + h * GDN_DV) for h in heads]
    beta = [beta_all[:, h:h + 1] for h in heads]
    eg = [e_cum[:, h:h + 1] for h in heads]
    decay = [jnp.exp(jnp.where(tril, cum[:, h:h + 1] - cum_t[h:h + 1, :], -jnp.inf)) for h in heads]
    kb = [k[h] * beta[h] for h in heads]
    kbf = [_bf(k[h]) for h in heads]
    raw = [_dot_nt(_bf(_rows([kb[h], q[h]])), kbf[h]) for h in heads]
    a = [jnp.where(strict, raw[h][:c] * decay[h], 0.0) for h in heads]
    attn = [raw[h][c:] * decay[h] for h in heads]
    n = _solve_correction(a, c)
    rhs = [jnp.concatenate([v[h] * beta[h], kb[h] * eg[h]], axis=-1) for h in heads]
    sol = [rhs[h] + _dot(_bf(n[h]), _bf(rhs[h])) for h in heads]
    s = [s_ref[0, h] for h in heads]
    ws_qs = [_dot(_bf(_rows([sol[h][:, GDN_DV:], q[h] * eg[h]])), _bf(s[h])) for h in heads]
    v_new = [sol[h][:, :GDN_DV] - ws_qs[h][:c] for h in heads]
    kdt = [(k[h] * e_rest[:, h:h + 1]).T for h in heads]
    upd = [_dot(_bf(_rows([kdt[h], attn[h]])), _bf(v_new[h])) for h in heads]
    for h in heads:
        o_ref[:, h * GDN_DV:(h + 1) * GDN_DV] = ws_qs[h][c:] + upd[h][GDN_DK:]
        s_ref[0, h] = s[h] * e_last[:, h:h + 1] + upd[h][:GDN_DK]

    cfin_ref[0] = ext_ref[c:c + SUBLANES, :]
    ext_ref[0:SUBLANES, :] = ext_ref[c:c + SUBLANES, :]


def _gdn(proj, ab, conv_w, a_log, dt_bias, s0, conv0, c, nb):
    n = proj.shape[0]
    nc = n // (nb * c)
    pad_lanes = lambda x: jnp.pad(x.reshape(1, GDN_HEADS), ((0, 0), (0, LANES - GDN_HEADS)))
    conv0p = jnp.pad(conv0, ((0, 0), (SUBLANES - (CONV_W - 1), 0), (0, 0)))
    row = lambda b, i: b * nc + i
    state_spec = pl.BlockSpec((1, GDN_HEADS, GDN_DK, GDN_DV), lambda b, i: (b, 0, 0, 0))
    conv_spec = pl.BlockSpec((1, SUBLANES, CONV_CH), lambda b, i: (b, 0, 0))
    o, s_fin, c_fin = pl.pallas_call(
        functools.partial(_gdn_kernel, c=c),
        grid=(nb, nc),
        in_specs=[
            pl.BlockSpec((c, CONV_CH), lambda b, i: (row(b, i), 1)),
            pl.BlockSpec((c, AB_W), lambda b, i: (row(b, i), 0)),
            _const_spec((CONV_W, CONV_CH)),
            _const_spec((1, LANES)),
            _const_spec((1, LANES)),
            state_spec,
            conv_spec,
        ],
        out_specs=[pl.BlockSpec((c, GDN_V), lambda b, i: (row(b, i), 0)), state_spec, conv_spec],
        out_shape=[
            jax.ShapeDtypeStruct((n, GDN_V), F32),
            jax.ShapeDtypeStruct(s0.shape, F32),
            jax.ShapeDtypeStruct(conv0p.shape, F32),
        ],
        scratch_shapes=[pltpu.VMEM((c + SUBLANES, CONV_CH), F32)],
        compiler_params=_params(("arbitrary", "arbitrary")),
        name="gdn",
    )(proj, ab, conv_w, pad_lanes(a_log), pad_lanes(dt_bias), s0, conv0p)
    return o, s_fin, c_fin[:, SUBLANES - (CONV_W - 1):, :]


def _head_norm(o, heads):
    d = o.shape[-1] // heads
    parts = []
    for h in range(heads):
        oh = o[:, h * d:(h + 1) * d]
        parts.append(oh * lax.rsqrt(jnp.mean(oh * oh, axis=-1, keepdims=True) + EPS))
    return jnp.concatenate(parts, axis=-1)


def _attn_out_kernel(x_ref, or_ref, og_ref, gr_ref, z_ref, gater_ref, gateg_ref, rn_ref, gn_ref,
                     wr_ref, wg_ref, wo_ref, o_ref):
    gr = gr_ref[...]
    o_r = _head_norm(or_ref[...], RET_HEADS) * rn_ref[...] * (gr * jax.nn.sigmoid(gr))
    br_r = _dot(o_r.astype(BF16), wr_ref[...])
    z = z_ref[...]
    o_g = _head_norm(og_ref[...], GDN_HEADS) * gn_ref[...] * (z * jax.nn.sigmoid(z))
    br_g = _dot(o_g.astype(BF16), wg_ref[...])
    mix = jax.nn.sigmoid(gater_ref[...]) * br_r + jax.nn.sigmoid(gateg_ref[...]) * br_g
    o_ref[...] = x_ref[...] + _dot(mix.astype(BF16), wo_ref[...])


def _attn_out(x2d, o_r, o_g, proj, ret_norm, gdn_norm, w_ret_br, w_gdn_br, w_out):
    n = x2d.shape[0]
    tm = min(512, n)
    rows = lambda blk: pl.BlockSpec((tm, D_MODEL), lambda i: (i, blk))
    wspec = _const_spec((D_MODEL, D_MODEL))
    return pl.pallas_call(
        _attn_out_kernel,
        grid=(n // tm,),
        in_specs=[rows(0), rows(0), rows(0), rows(2), rows(6), rows(7), rows(8),
                  _const_spec((1, D_MODEL)), _const_spec((1, D_MODEL)), wspec, wspec, wspec],
        out_specs=rows(0),
        out_shape=jax.ShapeDtypeStruct((n, D_MODEL), F32),
        compiler_params=_params(("arbitrary",)),
        name="attn_out",
    )(x2d, o_r, o_g, proj, proj, proj, proj, ret_norm, gdn_norm, w_ret_br, w_gdn_br, w_out)


FF_BLOCK = 1024


def _mlp_kernel(x_ref, ln2_ref, wu_ref, wd_ref, lnf_ref, o_ref):
    x = x_ref[...]
    hb = _rms(x, ln2_ref[...]).astype(BF16)
    acc = x
    for f in range(D_FF // FF_BLOCK):
        sl = slice(f * FF_BLOCK, (f + 1) * FF_BLOCK)
        up = jnp.maximum(_dot(hb, wu_ref[:, sl]), 0.0)
        acc = acc + _dot((up * up).astype(BF16), wd_ref[sl, :])
    o_ref[...] = _rms(acc, lnf_ref[...])


def _mlp(x2d, ln2, w_up, w_down, ln_f):
    n = x2d.shape[0]
    tm = min(512, n)
    rows = pl.BlockSpec((tm, D_MODEL), lambda i: (i, 0))
    return pl.pallas_call(
        _mlp_kernel,
        grid=(n // tm,),
        in_specs=[rows, _const_spec((1, D_MODEL)), _const_spec((D_MODEL, D_FF)), _const_spec((D_FF, D_MODEL)),
                  _const_spec((1, D_MODEL))],
        out_specs=rows,
        out_shape=jax.ShapeDtypeStruct((n, D_MODEL), F32),
        compiler_params=_params(("arbitrary",)),
        name="mlp",
    )(x2d, ln2, w_up, w_down, ln_f)


def _group(x, pos, s_ret, s_gdn, s_conv, wts):
    (ln1, w_main, w_ab, conv_w, a_log, dt_bias, ret_norm, gdn_norm, w_ret_br, w_gdn_br, w_out, ln2, w_up, w_down,
     ln_f) = wts
    nb, t, _ = x.shape
    c = CHUNK if t % CHUNK == 0 else t
    x2d = x.reshape(nb * t, D_MODEL)
    proj, ab = _in_proj(x2d, ln1, w_main, w_ab)
    o_r, ret_new = _retention(proj, s_ret, pos, c, nb)
    o_g, gdn_new, conv_new = _gdn(proj, ab, conv_w, a_log, dt_bias, s_gdn, s_conv, c, nb)
    x1 = _attn_out(x2d, o_r, o_g, proj, ret_norm, gdn_norm, w_ret_br, w_gdn_br, w_out)
    y = _mlp(x1, ln2, w_up, w_down, ln_f)
    return y.reshape(x.shape), ret_new, gdn_new, conv_new


def kernel(x_prompt, x_sample, state_ret, state_gdn, state_conv, ln1, w_in, conv_w, a_log, dt_bias, ret_norm,
           gdn_norm, w_ret_br, w_gdn_br, w_out, ln2, w_up, w_down, ln_f):
    depth = w_in.shape[0]
    assert depth == 1, "single-layer trunk"
    bp, tp, _ = x_prompt.shape
    ts = x_sample.shape[1]
    w = w_in[0]
    ab0 = 2 * RET_QK + 2 * RET_V + CONV_CH
    ab1 = ab0 + 2 * GDN_HEADS
    w_main = jnp.concatenate([w[:, :ab0], w[:, ab1:]], axis=1).astype(BF16)
    w_ab = jnp.zeros((D_MODEL, AB_W), F32)
    w_ab = w_ab.at[:, :GDN_HEADS].set(w[:, ab0:ab0 + GDN_HEADS])
    w_ab = w_ab.at[:, LANES:LANES + GDN_HEADS].set(w[:, ab0 + GDN_HEADS:ab1]).astype(BF16)
    vec = lambda v: v.reshape(1, -1)
    wts = (vec(ln1[0]), w_main, w_ab, conv_w[0], a_log[0], dt_bias[0], vec(ret_norm[0]), vec(gdn_norm[0]),
           w_ret_br[0].astype(BF16), w_gdn_br[0].astype(BF16), w_out[0].astype(BF16), vec(ln2[0]),
           w_up[0].astype(BF16), w_down[0].astype(BF16), vec(ln_f))
    pos_p = jnp.arange(tp, dtype=F32)
    pos_s = PAST_LEN + jnp.arange(ts, dtype=F32)
    zeros = lambda *shape: jnp.zeros((bp,) + shape, F32)
    yp, ret_p, gdn_p, conv_p = _group(x_prompt, pos_p, zeros(RET_HEADS, RET_DK, RET_DV),
                                      zeros(GDN_HEADS, GDN_DK, GDN_DV), zeros(CONV_W - 1, CONV_CH), wts)
    ys, ret_s, gdn_s, conv_s = _group(x_sample, pos_s, state_ret[0], state_gdn[0], state_conv[0], wts)
    return (yp, ys, ret_p[None], gdn_p[None], conv_p[None], ret_s[None], gdn_s[None], conv_s[None])
```

```python
import functools

import jax
import jax.numpy as jnp
from jax import lax
from jax.experimental import pallas as pl
from jax.experimental.pallas import tpu as pltpu

F32 = jnp.float32
BF16 = jnp.bfloat16

D_MODEL = 1024
PAST_LEN = 16384
RET_HEADS, RET_DK, RET_DV = 4, 128, 256
RET_QK = RET_HEADS * RET_DK
RET_V = RET_HEADS * RET_DV
GDN_HEADS, GDN_DK, GDN_DV = 8, 128, 128
GDN_QK = GDN_HEADS * GDN_DK
GDN_V = GDN_HEADS * GDN_DV
CONV_W = 4
CONV_CH = 2 * GDN_QK + GDN_V
D_FF = 4 * D_MODEL
CHUNK = 64
ROPE_BASE = 10000.0
EPS = 1e-6

SUBLANES = 8
LANES = 128
STEP_ROWS = 64
VMEM_LIMIT = 56 * 1024 * 1024

PROJ_BLOCK = 1024
BLK_QK, BLK_V, BLK_CONV0, BLK_GATE0, N_BLOCKS = 0, 1, 2, 5, 9
N_CONV_BLOCKS = CONV_CH // PROJ_BLOCK
N_GATE_BLOCKS = 4
AB_W = 2 * LANES


def _params(sem):
    return pltpu.CompilerParams(dimension_semantics=sem, vmem_limit_bytes=VMEM_LIMIT)


def _const_spec(shape):
    zeros = (0,) * len(shape)
    return pl.BlockSpec(shape, lambda *_: zeros, pipeline_mode=pl.Buffered(1))


def _dot(a, b):
    return jnp.dot(a, b, preferred_element_type=F32)


def _dot_nt(a, b):
    return lax.dot_general(a, b, (((1,), (1,)), ((), ())), preferred_element_type=F32)


def _dot_tn(a, b):
    return lax.dot_general(a, b, (((0,), (0,)), ((), ())), preferred_element_type=F32)


def _bf(x):
    return x.astype(BF16)


def _rows(parts):
    return jnp.concatenate(parts, axis=0)


def _rms(x, g):
    return x * lax.rsqrt(jnp.mean(x * x, axis=-1, keepdims=True) + EPS) * g


def _silu(x):
    return x * jax.nn.sigmoid(x)


def _in_proj_kernel(x_ref, ln_ref, w_ref, wab_ref, cos_ref, sin_ref, cw_ref, hist_ref, alog_ref, dtb_ref,
                    qk_ref, v_ref, conv_ref, gate_ref, gb_ref, tail_ref, h_ref, carry_ref,
                    *, seq_is_group, tiles_per_seq):
    i = pl.program_id(0)
    j = pl.program_id(1)
    tm = x_ref.shape[0]
    groups = tm // SUBLANES

    def proj():
        return _dot(h_ref[...], w_ref[...])

    @pl.when((j == BLK_QK) & (i == 0))
    def _():
        carry_ref[...] = jnp.zeros_like(carry_ref)

    @pl.when(j == BLK_QK)
    def _():
        hb = _bf(_rms(x_ref[...], ln_ref[...]))
        h_ref[...] = hb
        ab = _dot(hb, wab_ref[...])
        gb_ref[:, :LANES] = -jnp.exp(alog_ref[...]) * jax.nn.softplus(ab[:, :LANES] + dtb_ref[...])
        gb_ref[:, LANES:] = jax.nn.sigmoid(ab[:, LANES:])
        r = _dot(hb, w_ref[...])
        cosf = cos_ref[...]
        sinf = sin_ref[...]
        for blk in range(2 * RET_HEADS):
            sl = slice(blk * RET_DK, (blk + 1) * RET_DK)
            x = r[:, sl]
            y = x * cosf + pltpu.roll(x, RET_DK // 2, 1) * sinf
            if blk >= RET_HEADS:
                y = y * (RET_DK ** -0.5)
            qk_ref[:, sl] = _bf(y)

    @pl.when(j == BLK_V)
    def _():
        v_ref[...] = _bf(proj())

    @pl.when((j >= BLK_CONV0) & (j < BLK_CONV0 + N_CONV_BLOCKS))
    def _():
        r3 = proj().reshape(groups, SUBLANES, PROJ_BLOCK)
        sub = lax.broadcasted_iota(jnp.int32, (1, SUBLANES, PROJ_BLOCK), 1)
        if seq_is_group:
            hist3 = hist_ref[...].reshape(groups, SUBLANES, PROJ_BLOCK)
            tail_ref[...] = r3.reshape(tm, PROJ_BLOCK)
        else:
            cj = j - BLK_CONV0
            first = jnp.where(i % tiles_per_seq == 0, hist_ref[...], carry_ref[cj][None])
            tail_ref[...] = r3[groups - 1:groups]
            carry_ref[cj] = r3[groups - 1]
        y = None
        for shift in range(CONV_W - 1, 0, -1):
            cur = pltpu.roll(r3, shift, 1)
            if seq_is_group:
                prev = pltpu.roll(hist3, shift, 1)
            else:
                prev = jnp.concatenate([pltpu.roll(first, shift, 1), cur[:groups - 1]], axis=0)
            term = jnp.where(sub < shift, prev, cur) * cw_ref[CONV_W - 1 - shift:CONV_W - shift, :][None]
            y = term if y is None else y + term
        y = y + r3 * cw_ref[CONV_W - 1:CONV_W, :][None]
        conv_ref[...] = _silu(y).reshape(tm, PROJ_BLOCK)

    @pl.when((j >= BLK_GATE0) & (j < BLK_GATE0 + 2))
    def _():
        gate_ref[...] = _silu(proj())

    @pl.when(j >= BLK_GATE0 + 2)
    def _():
        gate_ref[...] = jax.nn.sigmoid(proj())


def _in_proj(x2d, t, pos, conv0, wts):
    ln1, w_main, w_ab, conv_w, a_log, dt_bias = wts
    n = x2d.shape[0]
    nb = n // t
    seq_is_group = t == SUBLANES
    tm = min(512 if seq_is_group else 1024, n)
    assert n % tm == 0 and (seq_is_group or t % tm == 0), (n, t, tm)
    tiles_per_seq = max(t // tm, 1)
    inv = ROPE_BASE ** (-jnp.arange(0, RET_DK, 2, dtype=F32) / RET_DK)
    ang = jnp.tile(pos, max(tm // t, 1))[:, None] * inv[None, :]
    cosf = jnp.concatenate([jnp.cos(ang), jnp.cos(ang)], axis=-1)
    sinf = jnp.concatenate([-jnp.sin(ang), jnp.sin(ang)], axis=-1)
    pos_blocks = cosf.shape[0] // tm
    hist = jnp.pad(conv0, ((0, 0), (SUBLANES - (CONV_W - 1), 0), (0, 0)))
    cidx = lambda j: jnp.clip(j - BLK_CONV0, 0, N_CONV_BLOCKS - 1)
    gidx = lambda j: jnp.clip(j - BLK_GATE0, 0, N_GATE_BLOCKS - 1)
    if seq_is_group:
        hist = hist.reshape(n, CONV_CH)
        hist_spec = pl.BlockSpec((tm, PROJ_BLOCK), lambda i, j: (i, cidx(j)))
        tail_spec = hist_spec
        tail_shape = jax.ShapeDtypeStruct((n, CONV_CH), F32)
    else:
        hist_spec = pl.BlockSpec((1, SUBLANES, PROJ_BLOCK), lambda i, j: (i // tiles_per_seq, 0, cidx(j)))
        tail_spec = pl.BlockSpec((1, SUBLANES, PROJ_BLOCK), lambda i, j: (i, 0, cidx(j)))
        tail_shape = jax.ShapeDtypeStruct((n // tm, SUBLANES, CONV_CH), F32)
    pad_lanes = lambda x: jnp.pad(x.reshape(1, GDN_HEADS), ((0, 0), (0, LANES - GDN_HEADS)))
    once = dict(pipeline_mode=pl.Buffered(1))
    qk, v, conv, gate, gb, tail = pl.pallas_call(
        functools.partial(_in_proj_kernel, seq_is_group=seq_is_group, tiles_per_seq=tiles_per_seq),
        grid=(n // tm, N_BLOCKS),
        in_specs=[
            pl.BlockSpec((tm, D_MODEL), lambda i, j: (i, 0)),
            _const_spec((1, D_MODEL)),
            pl.BlockSpec((D_MODEL, PROJ_BLOCK), lambda i, j: (0, j)),
            _const_spec((D_MODEL, AB_W)),
            pl.BlockSpec((tm, RET_DK), lambda i, j: (i % pos_blocks, 0)),
            pl.BlockSpec((tm, RET_DK), lambda i, j: (i % pos_blocks, 0)),
            pl.BlockSpec((CONV_W, PROJ_BLOCK), lambda i, j: (0, cidx(j))),
            hist_spec,
            _const_spec((1, LANES)),
            _const_spec((1, LANES)),
        ],
        out_specs=[
            pl.BlockSpec((tm, PROJ_BLOCK), lambda i, j: (i, 0), **once),
            pl.BlockSpec((tm, PROJ_BLOCK), lambda i, j: (i, 0), **once),
            pl.BlockSpec((tm, PROJ_BLOCK), lambda i, j: (i, cidx(j))),
            pl.BlockSpec((tm, PROJ_BLOCK), lambda i, j: (i, gidx(j))),
            pl.BlockSpec((tm, AB_W), lambda i, j: (i, 0), **once),
            tail_spec,
        ],
        out_shape=[
            jax.ShapeDtypeStruct((n, 2 * RET_QK), BF16),
            jax.ShapeDtypeStruct((n, RET_V), BF16),
            jax.ShapeDtypeStruct((n, CONV_CH), F32),
            jax.ShapeDtypeStruct((n, N_GATE_BLOCKS * PROJ_BLOCK), F32),
            jax.ShapeDtypeStruct((n, AB_W), F32),
            tail_shape,
        ],
        scratch_shapes=[pltpu.VMEM((tm, D_MODEL), BF16), pltpu.VMEM((N_CONV_BLOCKS, SUBLANES, PROJ_BLOCK), F32)],
        compiler_params=_params(("arbitrary", "arbitrary")),
        name="in_proj",
    )(x2d, ln1, w_main, w_ab, cosf, sinf, conv_w, hist, pad_lanes(a_log), pad_lanes(dt_bias))
    conv_new = tail.reshape(nb, -1, SUBLANES, CONV_CH)[:, -1, SUBLANES - (CONV_W - 1):, :]
    return qk, v, conv, gate, gb, conv_new


def _retention_kernel(q_ref, k_ref, v_ref, dintra_ref, dq_ref, dk_ref, dc_ref, s0_ref, o_ref, s_ref, *, ns, c):
    @pl.when(pl.program_id(1) == 0)
    def _():
        s_ref[...] = s0_ref[...]

    heads = range(RET_HEADS)
    seqs = range(ns)
    qb = [q_ref[:, h * RET_DK:(h + 1) * RET_DK] for h in heads]
    kb = [k_ref[:, h * RET_DK:(h + 1) * RET_DK] for h in heads]
    vb = [v_ref[:, h * RET_DV:(h + 1) * RET_DV] for h in heads]
    scores = [_dot_nt(qb[h], kb[h]) * dintra_ref[h] for h in heads]
    kd = [kb[h].astype(F32) * dk_ref[h] for h in heads]
    if ns == 1:
        s = [s_ref[0, h] for h in heads]
        qs = [_dot(qb[h], _bf(s[h])) for h in heads]
        upd = [_dot(_bf(_rows([kd[h].T, scores[h]])), vb[h]) for h in heads]
        for h in heads:
            o_ref[:, h * RET_DV:(h + 1) * RET_DV] = upd[h][RET_DK:] + qs[h] * dq_ref[h]
            s_ref[0, h] = s[h] * dc_ref[h] + upd[h][:RET_DK]
    else:
        rows = lambda x, j: x[j * c:(j + 1) * c]
        q32 = [qb[h].astype(F32) for h in heads]
        v32 = [vb[h].astype(F32) for h in heads]
        s = [[s_ref[j, h] for h in heads] for j in seqs]
        qs = [_rows([_dot(_bf(rows(q32[h], j)), _bf(s[j][h])) for j in seqs]) for h in heads]
        intra = [_dot(_bf(scores[h]), vb[h]) for h in heads]
        for h in heads:
            o_ref[:, h * RET_DV:(h + 1) * RET_DV] = intra[h] + qs[h] * dq_ref[h]
        ktv = [[_dot_tn(_bf(rows(kd[h], j)), _bf(rows(v32[h], j))) for h in heads] for j in seqs]
        for j in seqs:
            for h in heads:
                s_ref[j, h] = s[j][h] * dc_ref[h] + ktv[j][h]


def _block_diag_rows(c):
    idx = jnp.arange(STEP_ROWS)
    same = (idx[:, None] // c) == (idx[None, :] // c)
    return same, (idx % c).astype(F32)


def _retention(qk, v, s0, c):
    n = qk.shape[0]
    nb = s0.shape[0]
    ns = STEP_ROWS // c
    steps_per_group = n // (nb * c) if ns == 1 else 1
    groups = nb // ns
    log_g = jnp.log1p(-jnp.exp2(-5.0 - jnp.arange(RET_HEADS, dtype=F32)))
    same, off = _block_diag_rows(c)
    diff = off[:, None] - off[None, :]
    causal = same & (diff >= 0)
    d_intra = jnp.where(causal[None], jnp.exp(log_g[:, None, None] * jnp.where(causal, diff, 0.0)[None]), 0.0)
    d_q = jnp.broadcast_to(jnp.exp(log_g[:, None] * (off + 1.0)[None, :])[..., None], (RET_HEADS, STEP_ROWS, RET_DV))
    d_k = jnp.broadcast_to(jnp.exp(log_g[:, None] * (c - 1.0 - off)[None, :])[..., None],
                           (RET_HEADS, STEP_ROWS, RET_DK))
    d_c = jnp.broadcast_to(jnp.exp(log_g * c)[:, None, None], (RET_HEADS, 1, RET_DV))
    row = lambda b, i: b * steps_per_group + i
    state_spec = pl.BlockSpec((ns, RET_HEADS, RET_DK, RET_DV), lambda b, i: (b, 0, 0, 0))
    return pl.pallas_call(
        functools.partial(_retention_kernel, ns=ns, c=c),
        grid=(groups, steps_per_group),
        in_specs=[
            pl.BlockSpec((STEP_ROWS, RET_QK), lambda b, i: (row(b, i), 0)),
            pl.BlockSpec((STEP_ROWS, RET_QK), lambda b, i: (row(b, i), 1)),
            pl.BlockSpec((STEP_ROWS, RET_V), lambda b, i: (row(b, i), 0)),
            _const_spec((RET_HEADS, STEP_ROWS, STEP_ROWS)),
            _const_spec((RET_HEADS, STEP_ROWS, RET_DV)),
            _const_spec((RET_HEADS, STEP_ROWS, RET_DK)),
            _const_spec((RET_HEADS, 1, RET_DV)),
            state_spec,
        ],
        out_specs=[pl.BlockSpec((STEP_ROWS, RET_V), lambda b, i: (row(b, i), 0)), state_spec],
        out_shape=[jax.ShapeDtypeStruct((n, RET_V), F32), jax.ShapeDtypeStruct(s0.shape, F32)],
        compiler_params=_params(("arbitrary", "arbitrary")),
        name="retention",
    )(qk, qk, v, d_intra, d_q, d_k, d_c, s0)


def _solve_correction(a, c):
    heads = range(len(a))
    m = a[0].shape[0]
    n = [-x for x in a]
    ab = [_bf(x) for x in a]
    p = [_dot(ab[h], ab[h]) for h in heads]
    span = 2
    while span < c:
        pb = [_bf(x) for x in p]
        last = 2 * span >= c
        if last:
            prod = [_dot(_bf(n[h]), pb[h]) for h in heads]
            n = [n[h] + p[h] + prod[h] for h in heads]
        else:
            prod = [_dot(_bf(_rows([n[h], p[h]])), pb[h]) for h in heads]
            n = [n[h] + p[h] + prod[h][:m] for h in heads]
            p = [prod[h][m:] for h in heads]
        span *= 2
    return n


def _gdn_kernel(qkv_ref, gb_ref, s0_ref, o_ref, s_ref, *, ns, c):
    @pl.when(pl.program_id(1) == 0)
    def _():
        s_ref[...] = s0_ref[...]

    m = STEP_ROWS

    def l2norm(x):
        return x * lax.rsqrt(jnp.sum(x * x, axis=-1, keepdims=True) + EPS)

    lane = lax.broadcasted_iota(jnp.int32, (m, LANES), 1)
    off = lax.broadcasted_iota(jnp.int32, (m, LANES), 0) % c
    g_all = jnp.where(lane < GDN_HEADS, gb_ref[:, :LANES], 0.0)
    beta_all = gb_ref[:, LANES:]
    cum = g_all
    shift = 1
    while shift < c:
        cum = cum + jnp.where(off >= shift, pltpu.roll(cum, shift, 0), 0.0)
        shift *= 2
    cum_t = cum.T
    last = cum.reshape(ns, c, LANES)[:, c - 1:c, :]
    e_cum = jnp.exp(cum)
    e_rest = jnp.exp((last - cum.reshape(ns, c, LANES)).reshape(m, LANES))
    e_last = jnp.exp(last)

    r = lax.broadcasted_iota(jnp.int32, (m, m), 0)
    cc = lax.broadcasted_iota(jnp.int32, (m, m), 1)
    same = (r // c) == (cc // c)
    tril = same & (r >= cc)
    strict = same & (r > cc)

    heads = range(GDN_HEADS)
    seqs = range(ns)
    q = [l2norm(qkv_ref[:, h * GDN_DK:(h + 1) * GDN_DK]) * (GDN_DK ** -0.5) for h in heads]
    k = [l2norm(qkv_ref[:, GDN_QK + h * GDN_DK:GDN_QK + (h + 1) * GDN_DK]) for h in heads]
    v = [qkv_ref[:, 2 * GDN_QK + h * GDN_DV:2 * GDN_QK + (h + 1) * GDN_DV] for h in heads]
    beta = [beta_all[:, h:h + 1] for h in heads]
    eg = [e_cum[:, h:h + 1] for h in heads]
    decay = [jnp.exp(jnp.where(tril, cum[:, h:h + 1] - cum_t[h:h + 1, :], -jnp.inf)) for h in heads]
    kb = [k[h] * beta[h] for h in heads]
    kbf = [_bf(k[h]) for h in heads]
    raw = [_dot_nt(_bf(_rows([kb[h], q[h]])), kbf[h]) for h in heads]
    a = [jnp.where(strict, raw[h][:m] * decay[h], 0.0) for h in heads]
    attn = [raw[h][m:] * decay[h] for h in heads]
    n = _solve_correction(a, c)
    rhs = [jnp.concatenate([v[h] * beta[h], kb[h] * eg[h]], axis=-1) for h in heads]
    sol = [rhs[h] + _dot(_bf(n[h]), _bf(rhs[h])) for h in heads]
    u = [sol[h][:, :GDN_DV] for h in heads]
    w = [sol[h][:, GDN_DV:] for h in heads]
    qe = [q[h] * eg[h] for h in heads]
    kd = [k[h] * e_rest[:, h:h + 1] for h in heads]
    if ns == 1:
        s = [s_ref[0, h] for h in heads]
        ws_qs = [_dot(_bf(_rows([w[h], qe[h]])), _bf(s[h])) for h in heads]
        v_new = [u[h] - ws_qs[h][:m] for h in heads]
        upd = [_dot(_bf(_rows([kd[h].T, attn[h]])), _bf(v_new[h])) for h in heads]
        for h in heads:
            o_ref[:, h * GDN_DV:(h + 1) * GDN_DV] = ws_qs[h][m:] + upd[h][GDN_DK:]
            s_ref[0, h] = s[h] * e_last[0, :, h:h + 1] + upd[h][:GDN_DK]
    else:
        rows = lambda x, j: x[j * c:(j + 1) * c]
        s = [[s_ref[j, h] for h in heads] for j in seqs]
        ws_qs = [[_dot(_bf(_rows([rows(w[h], j), rows(qe[h], j)])), _bf(s[j][h])) for j in seqs] for h in heads]
        v_new = [_rows([rows(u[h], j) - ws_qs[h][j][:c] for j in seqs]) for h in heads]
        qs = [_rows([ws_qs[h][j][c:] for j in seqs]) for h in heads]
        vnb = [_bf(v_new[h]) for h in heads]
        intra = [_dot(_bf(attn[h]), vnb[h]) for h in heads]
        for h in heads:
            o_ref[:, h * GDN_DV:(h + 1) * GDN_DV] = qs[h] + intra[h]
        ktv = [[_dot_tn(_bf(rows(kd[h], j)), _bf(rows(v_new[h], j))) for h in heads] for j in seqs]
        for j in seqs:
            for h in heads:
                s_ref[j, h] = s[j][h] * e_last[j, :, h:h + 1] + ktv[j][h]


def _gdn(conv, gb, s0, c):
    n = conv.shape[0]
    nb = s0.shape[0]
    ns = STEP_ROWS // c
    steps_per_group = n // (nb * c) if ns == 1 else 1
    groups = nb // ns
    row = lambda b, i: b * steps_per_group + i
    state_spec = pl.BlockSpec((ns, GDN_HEADS, GDN_DK, GDN_DV), lambda b, i: (b, 0, 0, 0))
    return pl.pallas_call(
        functools.partial(_gdn_kernel, ns=ns, c=c),
        grid=(groups, steps_per_group),
        in_specs=[
            pl.BlockSpec((STEP_ROWS, CONV_CH), lambda b, i: (row(b, i), 0)),
            pl.BlockSpec((STEP_ROWS, AB_W), lambda b, i: (row(b, i), 0)),
            state_spec,
        ],
        out_specs=[pl.BlockSpec((STEP_ROWS, GDN_V), lambda b, i: (row(b, i), 0)), state_spec],
        out_shape=[jax.ShapeDtypeStruct((n, GDN_V), F32), jax.ShapeDtypeStruct(s0.shape, F32)],
        compiler_params=_params(("arbitrary", "arbitrary")),
        name="gdn",
    )(conv, gb, s0)


def _head_norm(o, heads):
    d = o.shape[-1] // heads
    parts = []
    for h in range(heads):
        oh = o[:, h * d:(h + 1) * d]
        parts.append(oh * lax.rsqrt(jnp.mean(oh * oh, axis=-1, keepdims=True) + EPS))
    return jnp.concatenate(parts, axis=-1)


def _attn_out_kernel(x_ref, or_ref, og_ref, gr_ref, z_ref, gater_ref, gateg_ref, rn_ref, gn_ref,
                     wr_ref, wg_ref, wo_ref, o_ref):
    o_r = _head_norm(or_ref[...], RET_HEADS) * rn_ref[...] * gr_ref[...]
    br_r = _dot(_bf(o_r), wr_ref[...])
    o_g = _head_norm(og_ref[...], GDN_HEADS) * gn_ref[...] * z_ref[...]
    br_g = _dot(_bf(o_g), wg_ref[...])
    mix = gater_ref[...] * br_r + gateg_ref[...] * br_g
    o_ref[...] = x_ref[...] + _dot(_bf(mix), wo_ref[...])


def _attn_out(x2d, o_r, o_g, gate, ret_norm, gdn_norm, w_ret_br, w_gdn_br, w_out):
    n = x2d.shape[0]
    tm = min(512, n)
    rows = lambda blk: pl.BlockSpec((tm, D_MODEL), lambda i: (i, blk))
    wspec = _const_spec((D_MODEL, D_MODEL))
    return pl.pallas_call(
        _attn_out_kernel,
        grid=(n // tm,),
        in_specs=[rows(0), rows(0), rows(0), rows(0), rows(1), rows(2), rows(3),
                  _const_spec((1, D_MODEL)), _const_spec((1, D_MODEL)), wspec, wspec, wspec],
        out_specs=rows(0),
        out_shape=jax.ShapeDtypeStruct((n, D_MODEL), F32),
        compiler_params=_params(("arbitrary",)),
        name="attn_out",
    )(x2d, o_r, o_g, gate, gate, gate, gate, ret_norm, gdn_norm, w_ret_br, w_gdn_br, w_out)


FF_BLOCK = 1024


def _mlp_kernel(x_ref, ln2_ref, wu_ref, wd_ref, lnf_ref, o_ref):
    x = x_ref[...]
    hb = _bf(_rms(x, ln2_ref[...]))
    acc = x
    for f in range(D_FF // FF_BLOCK):
        sl = slice(f * FF_BLOCK, (f + 1) * FF_BLOCK)
        up = jnp.maximum(_dot(hb, wu_ref[:, sl]), 0.0)
        acc = acc + _dot(_bf(up * up), wd_ref[sl, :])
    o_ref[...] = _rms(acc, lnf_ref[...])


def _mlp(x2d, ln2, w_up, w_down, ln_f):
    n = x2d.shape[0]
    tm = min(512, n)
    rows = pl.BlockSpec((tm, D_MODEL), lambda i: (i, 0))
    return pl.pallas_call(
        _mlp_kernel,
        grid=(n // tm,),
        in_specs=[rows, _const_spec((1, D_MODEL)), _const_spec((D_MODEL, D_FF)), _const_spec((D_FF, D_MODEL)),
                  _const_spec((1, D_MODEL))],
        out_specs=rows,
        out_shape=jax.ShapeDtypeStruct((n, D_MODEL), F32),
        compiler_params=_params(("arbitrary",)),
        name="mlp",
    )(x2d, ln2, w_up, w_down, ln_f)


def _group(x, pos, s_ret, s_gdn, s_conv, wts):
    (in_wts, ret_norm, gdn_norm, w_ret_br, w_gdn_br, w_out, ln2, w_up, w_down, ln_f) = wts
    nb, t, _ = x.shape
    c = CHUNK if t % CHUNK == 0 else t
    assert STEP_ROWS % c == 0 and nb % (STEP_ROWS // c) == 0, (nb, t)
    x2d = x.reshape(nb * t, D_MODEL)
    qk, v, conv, gate, gb, conv_new = _in_proj(x2d, t, pos, s_conv, in_wts)
    o_r, ret_new = _retention(qk, v, s_ret, c)
    o_g, gdn_new = _gdn(conv, gb, s_gdn, c)
    x1 = _attn_out(x2d, o_r, o_g, gate, ret_norm, gdn_norm, w_ret_br, w_gdn_br, w_out)
    y = _mlp(x1, ln2, w_up, w_down, ln_f)
    return y.reshape(x.shape), ret_new, gdn_new, conv_new


def kernel(x_prompt, x_sample, state_ret, state_gdn, state_conv, ln1, w_in, conv_w, a_log, dt_bias, ret_norm,
           gdn_norm, w_ret_br, w_gdn_br, w_out, ln2, w_up, w_down, ln_f):
    depth = w_in.shape[0]
    assert depth == 1, "single-layer trunk"
    bp, tp, _ = x_prompt.shape
    ts = x_sample.shape[1]
    w = w_in[0]
    o_gr = 2 * RET_QK + RET_V
    o_qkv = o_gr + RET_V
    o_a = o_qkv + CONV_CH
    o_z = o_a + 2 * GDN_HEADS
    w_main = jnp.concatenate([w[:, :o_gr], w[:, o_qkv:o_a], w[:, o_gr:o_qkv], w[:, o_z:]], axis=1).astype(BF16)
    w_ab = jnp.zeros((D_MODEL, AB_W), F32)
    w_ab = w_ab.at[:, :GDN_HEADS].set(w[:, o_a:o_a + GDN_HEADS])
    w_ab = w_ab.at[:, LANES:LANES + GDN_HEADS].set(w[:, o_a + GDN_HEADS:o_z]).astype(BF16)
    vec = lambda v: v.reshape(1, -1)
    in_wts = (vec(ln1[0]), w_main, w_ab, conv_w[0], a_log[0], dt_bias[0])
    wts = (in_wts, vec(ret_norm[0]), vec(gdn_norm[0]), w_ret_br[0].astype(BF16), w_gdn_br[0].astype(BF16),
           w_out[0].astype(BF16), vec(ln2[0]), w_up[0].astype(BF16), w_down[0].astype(BF16), vec(ln_f))
    pos_p = jnp.arange(tp, dtype=F32)
    pos_s = PAST_LEN + jnp.arange(ts, dtype=F32)
    zeros = lambda *shape: jnp.zeros((bp,) + shape, F32)
    yp, ret_p, gdn_p, conv_p = _group(x_prompt, pos_p, zeros(RET_HEADS, RET_DK, RET_DV),
                                      zeros(GDN_HEADS, GDN_DK, GDN_DV), zeros(CONV_W - 1, CONV_CH), wts)
    ys, ret_s, gdn_s, conv_s = _group(x_sample, pos_s, state_ret[0], state_gdn[0], state_conv[0], wts)
    return (yp, ys, ret_p[None], gdn_p[None], conv_p[None], ret_s[None], gdn_s[None], conv_s[None])
```

```python
import functools

import jax
import jax.numpy as jnp
from jax import lax
from jax.experimental import pallas as pl
from jax.experimental.pallas import tpu as pltpu

F32 = jnp.float32
BF16 = jnp.bfloat16

D_MODEL = 1024
PAST_LEN = 16384
RET_HEADS, RET_DK, RET_DV = 4, 128, 256
RET_QK = RET_HEADS * RET_DK
RET_V = RET_HEADS * RET_DV
GDN_HEADS, GDN_DK, GDN_DV = 8, 128, 128
GDN_QK = GDN_HEADS * GDN_DK
GDN_V = GDN_HEADS * GDN_DV
CONV_W = 4
CONV_CH = 2 * GDN_QK + GDN_V
D_FF = 4 * D_MODEL
CHUNK = 64
ROPE_BASE = 10000.0
EPS = 1e-6

SUBLANES = 8
LANES = 128
MXU_COLS = 256
STEP_ROWS = 64
VMEM_LIMIT = 56 * 1024 * 1024

PROJ_BLOCK = 1024
BLK_QK, BLK_V, BLK_CONV0, BLK_GATE0, N_BLOCKS = 0, 1, 2, 5, 9
N_CONV_BLOCKS = CONV_CH // PROJ_BLOCK
N_GATE_BLOCKS = 4
AB_W = 2 * LANES


def _params(sem):
    return pltpu.CompilerParams(dimension_semantics=sem, vmem_limit_bytes=VMEM_LIMIT)


def _const_spec(shape):
    zeros = (0,) * len(shape)
    return pl.BlockSpec(shape, lambda *_: zeros, pipeline_mode=pl.Buffered(1))


def _dot(a, b):
    return jnp.dot(a, b, preferred_element_type=F32)


def _dot_nt(a, b):
    return lax.dot_general(a, b, (((1,), (1,)), ((), ())), preferred_element_type=F32)


def _dot_tn(a, b):
    return lax.dot_general(a, b, (((0,), (0,)), ((), ())), preferred_element_type=F32)


def _bf(x):
    return x.astype(BF16)


def _rows(parts):
    return jnp.concatenate(parts, axis=0)


def _rms(x, g):
    return x * lax.rsqrt(jnp.mean(x * x, axis=-1, keepdims=True) + EPS) * g


def _silu(x):
    return x * jax.nn.sigmoid(x)


def _in_proj_kernel(x_ref, ln_ref, w_ref, wab_ref, cos_ref, sin_ref, cw_ref, hist_ref, alog_ref, dtb_ref,
                    qk_ref, v_ref, conv_ref, gate_ref, gb_ref, tail_ref, h_ref, carry_ref,
                    *, seq_is_group, tiles_per_seq):
    i = pl.program_id(0)
    j = pl.program_id(1)
    tm = x_ref.shape[0]
    groups = tm // SUBLANES
    slabs = [slice(s * MXU_COLS, (s + 1) * MXU_COLS) for s in range(PROJ_BLOCK // MXU_COLS)]

    def proj(sl):
        return _dot(h_ref[...], w_ref[:, sl])

    @pl.when((j == BLK_QK) & (i == 0))
    def _():
        carry_ref[...] = jnp.zeros_like(carry_ref)

    @pl.when(j == BLK_QK)
    def _():
        hb = _bf(_rms(x_ref[...], ln_ref[...]))
        h_ref[...] = hb
        ab = _dot(hb, wab_ref[...])
        gb_ref[:, :LANES] = -jnp.exp(alog_ref[...]) * jax.nn.softplus(ab[:, :LANES] + dtb_ref[...])
        gb_ref[:, LANES:] = jax.nn.sigmoid(ab[:, LANES:])
        cosf = cos_ref[...]
        sinf = sin_ref[...]
        for sl in slabs:
            r = proj(sl)
            for part in range(MXU_COLS // RET_DK):
                x = r[:, part * RET_DK:(part + 1) * RET_DK]
                y = x * cosf + pltpu.roll(x, RET_DK // 2, 1) * sinf
                start = sl.start + part * RET_DK
                if start >= RET_QK:
                    y = y * (RET_DK ** -0.5)
                qk_ref[:, start:start + RET_DK] = _bf(y)

    @pl.when(j == BLK_V)
    def _():
        for sl in slabs:
            v_ref[:, sl] = _bf(proj(sl))

    @pl.when((j >= BLK_CONV0) & (j < BLK_CONV0 + N_CONV_BLOCKS))
    def _():
        cj = j - BLK_CONV0
        sub = lax.broadcasted_iota(jnp.int32, (1, SUBLANES, MXU_COLS), 1)
        for sl in slabs:
            r3 = proj(sl).reshape(groups, SUBLANES, MXU_COLS)
            if seq_is_group:
                hist3 = hist_ref[:, sl].reshape(groups, SUBLANES, MXU_COLS)
                tail_ref[:, sl] = r3.reshape(tm, MXU_COLS)
            else:
                first = jnp.where(i % tiles_per_seq == 0, hist_ref[:, :, sl], carry_ref[cj, :, sl][None])
                tail_ref[:, :, sl] = r3[groups - 1:groups]
                carry_ref[cj, :, sl] = r3[groups - 1]
            y = None
            for shift in range(CONV_W - 1, 0, -1):
                cur = pltpu.roll(r3, shift, 1)
                if seq_is_group:
                    prev = pltpu.roll(hist3, shift, 1)
                else:
                    prev = jnp.concatenate([pltpu.roll(first, shift, 1), cur[:groups - 1]], axis=0)
                term = jnp.where(sub < shift, prev, cur) * cw_ref[CONV_W - 1 - shift:CONV_W - shift, sl][None]
                y = term if y is None else y + term
            y = y + r3 * cw_ref[CONV_W - 1:CONV_W, sl][None]
            conv_ref[:, sl] = _silu(y).reshape(tm, MXU_COLS)

    @pl.when((j >= BLK_GATE0) & (j < BLK_GATE0 + 2))
    def _():
        for sl in slabs:
            gate_ref[:, sl] = _silu(proj(sl))

    @pl.when(j >= BLK_GATE0 + 2)
    def _():
        for sl in slabs:
            gate_ref[:, sl] = jax.nn.sigmoid(proj(sl))


def _in_proj(x2d, t, pos, conv0, wts):
    ln1, w_main, w_ab, conv_w, a_log, dt_bias = wts
    n = x2d.shape[0]
    nb = n // t
    seq_is_group = t == SUBLANES
    tm = min(512, n) if seq_is_group else min(1024, t)
    assert n % tm == 0 and (seq_is_group or t % tm == 0), (n, t, tm)
    tiles_per_seq = max(t // tm, 1)
    inv = ROPE_BASE ** (-jnp.arange(0, RET_DK, 2, dtype=F32) / RET_DK)
    ang = jnp.tile(pos, max(tm // t, 1))[:, None] * inv[None, :]
    cosf = jnp.concatenate([jnp.cos(ang), jnp.cos(ang)], axis=-1)
    sinf = jnp.concatenate([-jnp.sin(ang), jnp.sin(ang)], axis=-1)
    pos_blocks = cosf.shape[0] // tm
    hist = jnp.pad(conv0, ((0, 0), (SUBLANES - (CONV_W - 1), 0), (0, 0)))
    cidx = lambda j: jnp.clip(j - BLK_CONV0, 0, N_CONV_BLOCKS - 1)
    gidx = lambda j: jnp.clip(j - BLK_GATE0, 0, N_GATE_BLOCKS - 1)
    if seq_is_group:
        hist = hist.reshape(n, CONV_CH)
        hist_spec = pl.BlockSpec((tm, PROJ_BLOCK), lambda i, j: (i, cidx(j)))
        tail_spec = hist_spec
        tail_shape = jax.ShapeDtypeStruct((n, CONV_CH), F32)
    else:
        hist_spec = pl.BlockSpec((1, SUBLANES, PROJ_BLOCK), lambda i, j: (i // tiles_per_seq, 0, cidx(j)))
        tail_spec = pl.BlockSpec((1, SUBLANES, PROJ_BLOCK), lambda i, j: (i, 0, cidx(j)))
        tail_shape = jax.ShapeDtypeStruct((n // tm, SUBLANES, CONV_CH), F32)
    pad_lanes = lambda x: jnp.pad(x.reshape(1, GDN_HEADS), ((0, 0), (0, LANES - GDN_HEADS)))
    once = dict(pipeline_mode=pl.Buffered(1))
    qk, v, conv, gate, gb, tail = pl.pallas_call(
        functools.partial(_in_proj_kernel, seq_is_group=seq_is_group, tiles_per_seq=tiles_per_seq),
        grid=(n // tm, N_BLOCKS),
        in_specs=[
            pl.BlockSpec((tm, D_MODEL), lambda i, j: (i, 0)),
            _const_spec((1, D_MODEL)),
            pl.BlockSpec((D_MODEL, PROJ_BLOCK), lambda i, j: (0, j)),
            _const_spec((D_MODEL, AB_W)),
            pl.BlockSpec((tm, RET_DK), lambda i, j: (i % pos_blocks, 0)),
            pl.BlockSpec((tm, RET_DK), lambda i, j: (i % pos_blocks, 0)),
            pl.BlockSpec((CONV_W, PROJ_BLOCK), lambda i, j: (0, cidx(j))),
            hist_spec,
            _const_spec((1, LANES)),
            _const_spec((1, LANES)),
        ],
        out_specs=[
            pl.BlockSpec((tm, PROJ_BLOCK), lambda i, j: (i, 0), **once),
            pl.BlockSpec((tm, PROJ_BLOCK), lambda i, j: (i, 0), **once),
            pl.BlockSpec((tm, PROJ_BLOCK), lambda i, j: (i, cidx(j))),
            pl.BlockSpec((tm, PROJ_BLOCK), lambda i, j: (i, gidx(j))),
            pl.BlockSpec((tm, AB_W), lambda i, j: (i, 0), **once),
            tail_spec,
        ],
        out_shape=[
            jax.ShapeDtypeStruct((n, 2 * RET_QK), BF16),
            jax.ShapeDtypeStruct((n, RET_V), BF16),
            jax.ShapeDtypeStruct((n, CONV_CH), F32),
            jax.ShapeDtypeStruct((n, N_GATE_BLOCKS * PROJ_BLOCK), F32),
            jax.ShapeDtypeStruct((n, AB_W), F32),
            tail_shape,
        ],
        scratch_shapes=[pltpu.VMEM((tm, D_MODEL), BF16), pltpu.VMEM((N_CONV_BLOCKS, SUBLANES, PROJ_BLOCK), F32)],
        compiler_params=_params(("arbitrary", "arbitrary")),
        name="in_proj",
    )(x2d, ln1, w_main, w_ab, cosf, sinf, conv_w, hist, pad_lanes(a_log), pad_lanes(dt_bias))
    conv_new = tail.reshape(nb, -1, SUBLANES, CONV_CH)[:, -1, SUBLANES - (CONV_W - 1):, :]
    return qk, v, conv, gate, gb, conv_new


def _step_layout(nb, t, c):
    ns = STEP_ROWS // c
    if ns == 1:
        par = max(p for p in (4, 2, 1) if nb % p == 0)
        return nb, t, ns, par
    assert t == c and nb % ns == 0, (nb, t, c)
    return nb // ns, STEP_ROWS, ns, 1


def _retention_kernel(q_ref, k_ref, v_ref, dintra_ref, dq_ref, dk_ref, dc_ref, s0_ref, o_ref, s_ref,
                      *, par, ns, c):
    @pl.when(pl.program_id(1) == 0)
    def _():
        s_ref[...] = s0_ref[...]

    units = [(p, h) for p in range(par) for h in range(RET_HEADS)]
    ids = range(len(units))
    dk_sl = lambda h: slice(h * RET_DK, (h + 1) * RET_DK)
    dv_sl = lambda h: slice(h * RET_DV, (h + 1) * RET_DV)
    qb = [q_ref[p, :, dk_sl(h)] for p, h in units]
    kb = [k_ref[p, :, dk_sl(h)] for p, h in units]
    vb = [v_ref[p, :, dv_sl(h)] for p, h in units]
    scores = [_dot_nt(qb[u], kb[u]) * dintra_ref[units[u][1]] for u in ids]
    kd = [kb[u].astype(F32) * dk_ref[units[u][1]] for u in ids]
    if ns == 1:
        s = [s_ref[p, h] for p, h in units]
        qs = [_dot(qb[u], _bf(s[u])) for u in ids]
        upd = [_dot(_bf(_rows([kd[u].T, scores[u]])), vb[u]) for u in ids]
        for u, (p, h) in enumerate(units):
            o_ref[p, :, dv_sl(h)] = upd[u][RET_DK:] + qs[u] * dq_ref[h]
            s_ref[p, h] = s[u] * dc_ref[h] + upd[u][:RET_DK]
    else:
        seqs = range(ns)
        rows = lambda x, j: x[j * c:(j + 1) * c]
        q32 = [qb[u].astype(F32) for u in ids]
        v32 = [vb[u].astype(F32) for u in ids]
        s = [[s_ref[j, h] for j in seqs] for _, h in units]
        qs = [_rows([_dot(_bf(rows(q32[u], j)), _bf(s[u][j])) for j in seqs]) for u in ids]
        intra = [_dot(_bf(scores[u]), vb[u]) for u in ids]
        for u, (p, h) in enumerate(units):
            o_ref[p, :, dv_sl(h)] = intra[u] + qs[u] * dq_ref[h]
        ktv = [[_dot_tn(_bf(rows(kd[u], j)), _bf(rows(v32[u], j))) for j in seqs] for u in ids]
        for u, (p, h) in enumerate(units):
            for j in seqs:
                s_ref[j, h] = s[u][j] * dc_ref[h] + ktv[u][j]


def _retention(qk, v, s0, t, c):
    nb = s0.shape[0]
    groups, rows, ns, par = _step_layout(nb, t, c)
    log_g = jnp.log1p(-jnp.exp2(-5.0 - jnp.arange(RET_HEADS, dtype=F32)))
    idx = jnp.arange(STEP_ROWS)
    same = (idx[:, None] // c) == (idx[None, :] // c)
    off = (idx % c).astype(F32)
    diff = off[:, None] - off[None, :]
    causal = same & (diff >= 0)
    d_intra = jnp.where(causal[None], jnp.exp(log_g[:, None, None] * jnp.where(causal, diff, 0.0)[None]), 0.0)
    d_q = jnp.broadcast_to(jnp.exp(log_g[:, None] * (off + 1.0)[None, :])[..., None], (RET_HEADS, STEP_ROWS, RET_DV))
    d_k = jnp.broadcast_to(jnp.exp(log_g[:, None] * (c - 1.0 - off)[None, :])[..., None],
                           (RET_HEADS, STEP_ROWS, RET_DK))
    d_c = jnp.broadcast_to(jnp.exp(log_g * c)[:, None, None], (RET_HEADS, 1, RET_DV))
    blk = lambda cols, col_blk: pl.BlockSpec((par, STEP_ROWS, cols), lambda g, i: (g, i, col_blk))
    state_spec = pl.BlockSpec((par * ns, RET_HEADS, RET_DK, RET_DV), lambda g, i: (g, 0, 0, 0))
    qk3 = qk.reshape(groups, rows, 2 * RET_QK)
    o, s_new = pl.pallas_call(
        functools.partial(_retention_kernel, par=par, ns=ns, c=c),
        grid=(groups // par, rows // STEP_ROWS),
        in_specs=[
            blk(RET_QK, 0),
            blk(RET_QK, 1),
            blk(RET_V, 0),
            _const_spec((RET_HEADS, STEP_ROWS, STEP_ROWS)),
            _const_spec((RET_HEADS, STEP_ROWS, RET_DV)),
            _const_spec((RET_HEADS, STEP_ROWS, RET_DK)),
            _const_spec((RET_HEADS, 1, RET_DV)),
            state_spec,
        ],
        out_specs=[blk(RET_V, 0), state_spec],
        out_shape=[jax.ShapeDtypeStruct((groups, rows, RET_V), F32), jax.ShapeDtypeStruct(s0.shape, F32)],
        compiler_params=_params(("arbitrary", "arbitrary")),
        name="retention",
    )(qk3, qk3, v.reshape(groups, rows, RET_V), d_intra, d_q, d_k, d_c, s0)
    return o.reshape(nb * t, RET_V), s_new


def _solve_correction(a, c):
    heads = range(len(a))
    m = a[0].shape[0]
    n = [-x for x in a]
    ab = [_bf(x) for x in a]
    p = [_dot(ab[h], ab[h]) for h in heads]
    span = 2
    while span < c:
        pb = [_bf(x) for x in p]
        last = 2 * span >= c
        if last:
            prod = [_dot(_bf(n[h]), pb[h]) for h in heads]
            n = [n[h] + p[h] + prod[h] for h in heads]
        else:
            prod = [_dot(_bf(_rows([n[h], p[h]])), pb[h]) for h in heads]
            n = [n[h] + p[h] + prod[h][:m] for h in heads]
            p = [prod[h][m:] for h in heads]
        span *= 2
    return n


def _gdn_kernel(qkv_ref, gb_ref, s0_ref, o_ref, s_ref, *, par, ns, c):
    @pl.when(pl.program_id(1) == 0)
    def _():
        s_ref[...] = s0_ref[...]

    m = STEP_ROWS

    def l2norm(x):
        return x * lax.rsqrt(jnp.sum(x * x, axis=-1, keepdims=True) + EPS)

    lane = lax.broadcasted_iota(jnp.int32, (m, LANES), 1)
    off = lax.broadcasted_iota(jnp.int32, (m, LANES), 0) % c
    r = lax.broadcasted_iota(jnp.int32, (m, m), 0)
    cc = lax.broadcasted_iota(jnp.int32, (m, m), 1)
    same = (r // c) == (cc // c)
    tril = same & (r >= cc)
    strict = same & (r > cc)

    def gates(p):
        cum = jnp.where(lane < GDN_HEADS, gb_ref[p, :, :LANES], 0.0)
        shift = 1
        while shift < c:
            cum = cum + jnp.where(off >= shift, pltpu.roll(cum, shift, 0), 0.0)
            shift *= 2
        last = cum.reshape(ns, c, LANES)[:, c - 1:c, :]
        e_rest = jnp.exp((last - cum.reshape(ns, c, LANES)).reshape(m, LANES))
        return dict(cum=cum, cum_t=cum.T, e_cum=jnp.exp(cum), e_rest=e_rest, e_last=jnp.exp(last),
                    beta=gb_ref[p, :, LANES:])

    gt = [gates(p) for p in range(par)]
    units = [(p, h) for p in range(par) for h in range(GDN_HEADS)]
    ids = range(len(units))
    col = lambda x, h: x[:, h:h + 1]
    q = [l2norm(qkv_ref[p, :, h * GDN_DK:(h + 1) * GDN_DK]) * (GDN_DK ** -0.5) for p, h in units]
    k = [l2norm(qkv_ref[p, :, GDN_QK + h * GDN_DK:GDN_QK + (h + 1) * GDN_DK]) for p, h in units]
    v = [qkv_ref[p, :, 2 * GDN_QK + h * GDN_DV:2 * GDN_QK + (h + 1) * GDN_DV] for p, h in units]
    beta = [col(gt[p]["beta"], h) for p, h in units]
    eg = [col(gt[p]["e_cum"], h) for p, h in units]
    decay = [jnp.exp(jnp.where(tril, col(gt[p]["cum"], h) - gt[p]["cum_t"][h:h + 1, :], -jnp.inf)) for p, h in units]
    kb = [k[u] * beta[u] for u in ids]
    kbf = [_bf(k[u]) for u in ids]
    raw = [_dot_nt(_bf(_rows([kb[u], q[u]])), kbf[u]) for u in ids]
    a = [jnp.where(strict, raw[u][:m] * decay[u], 0.0) for u in ids]
    attn = [raw[u][m:] * decay[u] for u in ids]
    n = _solve_correction(a, c)
    rhs = [jnp.concatenate([v[u] * beta[u], kb[u] * eg[u]], axis=-1) for u in ids]
    sol = [rhs[u] + _dot(_bf(n[u]), _bf(rhs[u])) for u in ids]
    uu = [sol[u][:, :GDN_DV] for u in ids]
    w = [sol[u][:, GDN_DV:] for u in ids]
    qe = [q[u] * eg[u] for u in ids]
    kd = [k[u] * col(gt[p]["e_rest"], h) for u, (p, h) in enumerate(units)]
    dv_sl = lambda h: slice(h * GDN_DV, (h + 1) * GDN_DV)
    if ns == 1:
        s = [s_ref[p, h] for p, h in units]
        ws_qs = [_dot(_bf(_rows([w[u], qe[u]])), _bf(s[u])) for u in ids]
        v_new = [uu[u] - ws_qs[u][:m] for u in ids]
        upd = [_dot(_bf(_rows([kd[u].T, attn[u]])), _bf(v_new[u])) for u in ids]
        for u, (p, h) in enumerate(units):
            o_ref[p, :, dv_sl(h)] = ws_qs[u][m:] + upd[u][GDN_DK:]
            s_ref[p, h] = s[u] * gt[p]["e_last"][0, :, h:h + 1] + upd[u][:GDN_DK]
    else:
        seqs = range(ns)
        rows = lambda x, j: x[j * c:(j + 1) * c]
        s = [[s_ref[j, h] for j in seqs] for _, h in units]
        ws_qs = [[_dot(_bf(_rows([rows(w[u], j), rows(qe[u], j)])), _bf(s[u][j])) for j in seqs] for u in ids]
        v_new = [_rows([rows(uu[u], j) - ws_qs[u][j][:c] for j in seqs]) for u in ids]
        qs = [_rows([ws_qs[u][j][c:] for j in seqs]) for u in ids]
        intra = [_dot(_bf(attn[u]), _bf(v_new[u])) for u in ids]
        for u, (p, h) in enumerate(units):
            o_ref[p, :, dv_sl(h)] = qs[u] + intra[u]
        ktv = [[_dot_tn(_bf(rows(kd[u], j)), _bf(rows(v_new[u], j))) for j in seqs] for u in ids]
        for u, (p, h) in enumerate(units):
            for j in seqs:
                s_ref[j, h] = s[u][j] * gt[p]["e_last"][j, :, h:h + 1] + ktv[u][j]


def _gdn(conv, gb, s0, t, c):
    nb = s0.shape[0]
    groups, rows, ns, par = _step_layout(nb, t, c)
    blk = lambda cols: pl.BlockSpec((par, STEP_ROWS, cols), lambda g, i: (g, i, 0))
    state_spec = pl.BlockSpec((par * ns, GDN_HEADS, GDN_DK, GDN_DV), lambda g, i: (g, 0, 0, 0))
    o, s_new = pl.pallas_call(
        functools.partial(_gdn_kernel, par=par, ns=ns, c=c),
        grid=(groups // par, rows // STEP_ROWS),
        in_specs=[blk(CONV_CH), blk(AB_W), state_spec],
        out_specs=[blk(GDN_V), state_spec],
        out_shape=[jax.ShapeDtypeStruct((groups, rows, GDN_V), F32), jax.ShapeDtypeStruct(s0.shape, F32)],
        compiler_params=_params(("arbitrary", "arbitrary")),
        name="gdn",
    )(conv.reshape(groups, rows, CONV_CH), gb.reshape(groups, rows, AB_W), s0)
    return o.reshape(nb * t, GDN_V), s_new


def _head_norm(o, heads):
    d = o.shape[-1] // heads
    parts = []
    for h in range(heads):
        oh = o[:, h * d:(h + 1) * d]
        parts.append(oh * lax.rsqrt(jnp.mean(oh * oh, axis=-1, keepdims=True) + EPS))
    return jnp.concatenate(parts, axis=-1)


def _attn_out_kernel(x_ref, or_ref, og_ref, gr_ref, z_ref, gater_ref, gateg_ref, rn_ref, gn_ref,
                     wr_ref, wg_ref, wo_ref, o_ref):
    o_r = _head_norm(or_ref[...], RET_HEADS) * rn_ref[...] * gr_ref[...]
    br_r = _dot(_bf(o_r), wr_ref[...])
    o_g = _head_norm(og_ref[...], GDN_HEADS) * gn_ref[...] * z_ref[...]
    br_g = _dot(_bf(o_g), wg_ref[...])
    mix = gater_ref[...] * br_r + gateg_ref[...] * br_g
    o_ref[...] = x_ref[...] + _dot(_bf(mix), wo_ref[...])


def _attn_out(x2d, o_r, o_g, gate, ret_norm, gdn_norm, w_ret_br, w_gdn_br, w_out):
    n = x2d.shape[0]
    tm = min(512, n)
    rows = lambda blk: pl.BlockSpec((tm, D_MODEL), lambda i: (i, blk))
    wspec = _const_spec((D_MODEL, D_MODEL))
    return pl.pallas_call(
        _attn_out_kernel,
        grid=(n // tm,),
        in_specs=[rows(0), rows(0), rows(0), rows(0), rows(1), rows(2), rows(3),
                  _const_spec((1, D_MODEL)), _const_spec((1, D_MODEL)), wspec, wspec, wspec],
        out_specs=rows(0),
        out_shape=jax.ShapeDtypeStruct((n, D_MODEL), F32),
        compiler_params=_params(("arbitrary",)),
        name="attn_out",
    )(x2d, o_r, o_g, gate, gate, gate, gate, ret_norm, gdn_norm, w_ret_br, w_gdn_br, w_out)


FF_BLOCK = 1024


def _mlp_kernel(x_ref, ln2_ref, wu_ref, wd_ref, lnf_ref, o_ref):
    x = x_ref[...]
    hb = _bf(_rms(x, ln2_ref[...]))
    acc = x
    for f in range(D_FF // FF_BLOCK):
        sl = slice(f * FF_BLOCK, (f + 1) * FF_BLOCK)
        up = jnp.maximum(_dot(hb, wu_ref[:, sl]), 0.0)
        acc = acc + _dot(_bf(up * up), wd_ref[sl, :])
    o_ref[...] = _rms(acc, lnf_ref[...])


def _mlp(x2d, ln2, w_up, w_down, ln_f):
    n = x2d.shape[0]
    tm = min(512, n)
    rows = pl.BlockSpec((tm, D_MODEL), lambda i: (i, 0))
    return pl.pallas_call(
        _mlp_kernel,
        grid=(n // tm,),
        in_specs=[rows, _const_spec((1, D_MODEL)), _const_spec((D_MODEL, D_FF)), _const_spec((D_FF, D_MODEL)),
                  _const_spec((1, D_MODEL))],
        out_specs=rows,
        out_shape=jax.ShapeDtypeStruct((n, D_MODEL), F32),
        compiler_params=_params(("arbitrary",)),
        name="mlp",
    )(x2d, ln2, w_up, w_down, ln_f)


def _group(x, pos, s_ret, s_gdn, s_conv, wts):
    (in_wts, ret_norm, gdn_norm, w_ret_br, w_gdn_br, w_out, ln2, w_up, w_down, ln_f) = wts
    nb, t, _ = x.shape
    c = CHUNK if t % CHUNK == 0 else t
    assert STEP_ROWS % c == 0 and nb % (STEP_ROWS // c) == 0, (nb, t)
    x2d = x.reshape(nb * t, D_MODEL)
    qk, v, conv, gate, gb, conv_new = _in_proj(x2d, t, pos, s_conv, in_wts)
    o_r, ret_new = _retention(qk, v, s_ret, t, c)
    o_g, gdn_new = _gdn(conv, gb, s_gdn, t, c)
    x1 = _attn_out(x2d, o_r, o_g, gate, ret_norm, gdn_norm, w_ret_br, w_gdn_br, w_out)
    y = _mlp(x1, ln2, w_up, w_down, ln_f)
    return y.reshape(x.shape), ret_new, gdn_new, conv_new


def kernel(x_prompt, x_sample, state_ret, state_gdn, state_conv, ln1, w_in, conv_w, a_log, dt_bias, ret_norm,
           gdn_norm, w_ret_br, w_gdn_br, w_out, ln2, w_up, w_down, ln_f):
    depth = w_in.shape[0]
    assert depth == 1, "single-layer trunk"
    bp, tp, _ = x_prompt.shape
    ts = x_sample.shape[1]
    w = w_in[0]
    o_gr = 2 * RET_QK + RET_V
    o_qkv = o_gr + RET_V
    o_a = o_qkv + CONV_CH
    o_z = o_a + 2 * GDN_HEADS
    w_main = jnp.concatenate([w[:, :o_gr], w[:, o_qkv:o_a], w[:, o_gr:o_qkv], w[:, o_z:]], axis=1).astype(BF16)
    w_ab = jnp.zeros((D_MODEL, AB_W), F32)
    w_ab = w_ab.at[:, :GDN_HEADS].set(w[:, o_a:o_a + GDN_HEADS])
    w_ab = w_ab.at[:, LANES:LANES + GDN_HEADS].set(w[:, o_a + GDN_HEADS:o_z]).astype(BF16)
    vec = lambda v: v.reshape(1, -1)
    in_wts = (vec(ln1[0]), w_main, w_ab, conv_w[0], a_log[0], dt_bias[0])
    wts = (in_wts, vec(ret_norm[0]), vec(gdn_norm[0]), w_ret_br[0].astype(BF16), w_gdn_br[0].astype(BF16),
           w_out[0].astype(BF16), vec(ln2[0]), w_up[0].astype(BF16), w_down[0].astype(BF16), vec(ln_f))
    pos_p = jnp.arange(tp, dtype=F32)
    pos_s = PAST_LEN + jnp.arange(ts, dtype=F32)
    zeros = lambda *shape: jnp.zeros((bp,) + shape, F32)
    yp, ret_p, gdn_p, conv_p = _group(x_prompt, pos_p, zeros(RET_HEADS, RET_DK, RET_DV),
                                      zeros(GDN_HEADS, GDN_DK, GDN_DV), zeros(CONV_W - 1, CONV_CH), wts)
    ys, ret_s, gdn_s, conv_s = _group(x_sample, pos_s, state_ret[0], state_gdn[0], state_conv[0], wts)
    return (yp, ys, ret_p[None], gdn_p[None], conv_p[None], ret_s[None], gdn_s[None], conv_s[None])
```

```python
import functools

import jax
import jax.numpy as jnp
from jax import lax
from jax.experimental import pallas as pl
from jax.experimental.pallas import tpu as pltpu

F32 = jnp.float32
BF16 = jnp.bfloat16

D_MODEL = 1024
PAST_LEN = 16384
RET_HEADS, RET_DK, RET_DV = 4, 128, 256
RET_QK = RET_HEADS * RET_DK
RET_V = RET_HEADS * RET_DV
GDN_HEADS, GDN_DK, GDN_DV = 8, 128, 128
GDN_QK = GDN_HEADS * GDN_DK
GDN_V = GDN_HEADS * GDN_DV
CONV_W = 4
CONV_CH = 2 * GDN_QK + GDN_V
D_FF = 4 * D_MODEL
CHUNK = 64
ROPE_BASE = 10000.0
EPS = 1e-6

SUBLANES = 8
LANES = 128
MXU_COLS = 256
STEP_ROWS = 64
VMEM_LIMIT = 56 * 1024 * 1024

QKV_W = 2 * RET_QK + RET_V
GATE_W = 4 * D_MODEL
N_SILU_GATES = 2 * D_MODEL
AB_W = 2 * LANES


def _params(sem):
    return pltpu.CompilerParams(dimension_semantics=sem, vmem_limit_bytes=VMEM_LIMIT)


def _const_spec(shape):
    zeros = (0,) * len(shape)
    return pl.BlockSpec(shape, lambda *_: zeros, pipeline_mode=pl.Buffered(1))


def _dot(a, b):
    return jnp.dot(a, b, preferred_element_type=F32)


def _dot_nt(a, b):
    return lax.dot_general(a, b, (((1,), (1,)), ((), ())), preferred_element_type=F32)


def _dot_tn(a, b):
    return lax.dot_general(a, b, (((0,), (0,)), ((), ())), preferred_element_type=F32)


def _bf(x):
    return x.astype(BF16)


def _rows(parts):
    return jnp.concatenate(parts, axis=0)


def _rms(x, g):
    return x * lax.rsqrt(jnp.mean(x * x, axis=-1, keepdims=True) + EPS) * g


def _silu(x):
    return x * jax.nn.sigmoid(x)


def _in_proj_kernel(x_ref, ln_ref, wqkv_ref, wconv_ref, wgate_ref, wab_ref, cos_ref, sin_ref, cw_ref, hist_ref,
                    alog_ref, dtb_ref, qk_ref, v_ref, conv_ref, gate_ref, gb_ref, tail_ref, h_ref, carry_ref,
                    *, seq_is_group, tiles_per_seq):
    i = pl.program_id(0)
    tm = x_ref.shape[0]
    groups = tm // SUBLANES

    @pl.when(i == 0)
    def _():
        carry_ref[...] = jnp.zeros_like(carry_ref)

    h_ref[...] = _bf(_rms(x_ref[...], ln_ref[...]))
    ab = _dot(h_ref[...], wab_ref[...])
    gb_ref[:, :LANES] = -jnp.exp(alog_ref[...]) * jax.nn.softplus(ab[:, :LANES] + dtb_ref[...])
    gb_ref[:, LANES:] = jax.nn.sigmoid(ab[:, LANES:])

    def rope_slab(s):
        sl = slice(s * MXU_COLS, (s + 1) * MXU_COLS)
        r = _dot(h_ref[...], wqkv_ref[:, sl])
        for part in range(MXU_COLS // RET_DK):
            x = r[:, part * RET_DK:(part + 1) * RET_DK]
            y = x * cos_ref[...] + pltpu.roll(x, RET_DK // 2, 1) * sin_ref[...]
            start = sl.start + part * RET_DK
            if start >= RET_QK:
                y = y * (RET_DK ** -0.5)
            qk_ref[:, start:start + RET_DK] = _bf(y)

    def v_slab(s):
        sl = slice(s * MXU_COLS, (s + 1) * MXU_COLS)
        v_ref[:, sl] = _bf(_dot(h_ref[...], wqkv_ref[:, 2 * RET_QK + sl.start:2 * RET_QK + sl.stop]))

    def gate_slab(s):
        sl = slice(s * MXU_COLS, (s + 1) * MXU_COLS)
        r = _dot(h_ref[...], wgate_ref[:, sl])
        gate_ref[:, sl] = _bf(_silu(r) if sl.start < N_SILU_GATES else jax.nn.sigmoid(r))

    sub = lax.broadcasted_iota(jnp.int32, (1, SUBLANES, MXU_COLS), 1)

    def conv_slab(s):
        sl = slice(s * MXU_COLS, (s + 1) * MXU_COLS)
        w = [cw_ref[tap:tap + 1, sl][None] for tap in range(CONV_W)]
        r3 = _dot(h_ref[...], wconv_ref[:, sl]).reshape(groups, SUBLANES, MXU_COLS)
        cur1 = pltpu.roll(r3, 1, 1)
        if seq_is_group:
            hist3 = hist_ref[:, sl].reshape(groups, SUBLANES, MXU_COLS)
            tail_ref[:, sl] = r3.reshape(tm, MXU_COLS)
            prev1 = pltpu.roll(hist3, 1, 1)
            pair_prev = w[1] * hist3 + w[0] * prev1
        else:
            first = jnp.where(i % tiles_per_seq == 0, hist_ref[:, :, sl], carry_ref[:, sl][None])
            tail_ref[:, :, sl] = r3[groups - 1:groups]
            carry_ref[:, sl] = r3[groups - 1]
            first1 = pltpu.roll(first, 1, 1)
            prev1 = jnp.concatenate([first1, cur1[:groups - 1]], axis=0)
            pair_first = w[1] * first + w[0] * first1
        x1 = jnp.where(sub < 1, prev1, cur1)
        pair = w[1] * r3 + w[0] * x1
        cur2 = pltpu.roll(pair, 2, 1)
        if seq_is_group:
            prev2 = pltpu.roll(pair_prev, 2, 1)
        else:
            prev2 = jnp.concatenate([pltpu.roll(pair_first, 2, 1), cur2[:groups - 1]], axis=0)
        y = (w[3] * r3 + w[2] * x1) + jnp.where(sub < 2, prev2, cur2)
        conv_ref[:, sl] = _silu(y).reshape(tm, MXU_COLS)

    light = ([functools.partial(rope_slab, s) for s in range(2 * RET_QK // MXU_COLS)]
             + [functools.partial(v_slab, s) for s in range(RET_V // MXU_COLS)]
             + [functools.partial(gate_slab, s) for s in range(GATE_W // MXU_COLS)])
    n_conv = CONV_CH // MXU_COLS
    per_conv = len(light) // n_conv
    for s in range(n_conv):
        conv_slab(s)
        for f in light[s * per_conv:(s + 1) * per_conv]:
            f()
    for f in light[n_conv * per_conv:]:
        f()


def _in_proj(x2d, t, pos, conv0, wts):
    ln1, w_qkv, w_conv, w_gate, w_ab, conv_w, a_log, dt_bias = wts
    n = x2d.shape[0]
    nb = n // t
    seq_is_group = t == SUBLANES
    tm = min(256, n) if seq_is_group else min(512, t)
    assert n % tm == 0 and (seq_is_group or t % tm == 0), (n, t, tm)
    tiles_per_seq = max(t // tm, 1)
    inv = ROPE_BASE ** (-jnp.arange(0, RET_DK, 2, dtype=F32) / RET_DK)
    ang = jnp.tile(pos, max(tm // t, 1))[:, None] * inv[None, :]
    cosf = jnp.concatenate([jnp.cos(ang), jnp.cos(ang)], axis=-1)
    sinf = jnp.concatenate([-jnp.sin(ang), jnp.sin(ang)], axis=-1)
    pos_blocks = cosf.shape[0] // tm
    hist = jnp.pad(conv0, ((0, 0), (SUBLANES - (CONV_W - 1), 0), (0, 0)))
    rows = lambda cols: pl.BlockSpec((tm, cols), lambda i: (i, 0))
    if seq_is_group:
        hist = hist.reshape(n, CONV_CH)
        hist_spec = rows(CONV_CH)
        tail_spec = hist_spec
        tail_shape = jax.ShapeDtypeStruct((n, CONV_CH), F32)
    else:
        hist_spec = pl.BlockSpec((1, SUBLANES, CONV_CH), lambda i: (i // tiles_per_seq, 0, 0))
        tail_spec = pl.BlockSpec((1, SUBLANES, CONV_CH), lambda i: (i, 0, 0))
        tail_shape = jax.ShapeDtypeStruct((n // tm, SUBLANES, CONV_CH), F32)
    pad_lanes = lambda x: jnp.pad(x.reshape(1, GDN_HEADS), ((0, 0), (0, LANES - GDN_HEADS)))
    pos_spec = pl.BlockSpec((tm, RET_DK), lambda i: (i % pos_blocks, 0))
    qk, v, conv, gate, gb, tail = pl.pallas_call(
        functools.partial(_in_proj_kernel, seq_is_group=seq_is_group, tiles_per_seq=tiles_per_seq),
        grid=(n // tm,),
        in_specs=[
            rows(D_MODEL),
            _const_spec((1, D_MODEL)),
            _const_spec((D_MODEL, QKV_W)),
            _const_spec((D_MODEL, CONV_CH)),
            _const_spec((D_MODEL, GATE_W)),
            _const_spec((D_MODEL, AB_W)),
            pos_spec,
            pos_spec,
            _const_spec((CONV_W, CONV_CH)),
            hist_spec,
            _const_spec((1, LANES)),
            _const_spec((1, LANES)),
        ],
        out_specs=[rows(2 * RET_QK), rows(RET_V), rows(CONV_CH), rows(GATE_W), rows(AB_W), tail_spec],
        out_shape=[
            jax.ShapeDtypeStruct((n, 2 * RET_QK), BF16),
            jax.ShapeDtypeStruct((n, RET_V), BF16),
            jax.ShapeDtypeStruct((n, CONV_CH), F32),
            jax.ShapeDtypeStruct((n, GATE_W), BF16),
            jax.ShapeDtypeStruct((n, AB_W), F32),
            tail_shape,
        ],
        scratch_shapes=[pltpu.VMEM((tm, D_MODEL), BF16), pltpu.VMEM((SUBLANES, CONV_CH), F32)],
        compiler_params=_params(("arbitrary",)),
        name="in_proj",
    )(x2d, ln1, w_qkv, w_conv, w_gate, w_ab, cosf, sinf, conv_w, hist, pad_lanes(a_log), pad_lanes(dt_bias))
    conv_new = tail.reshape(nb, -1, SUBLANES, CONV_CH)[:, -1, SUBLANES - (CONV_W - 1):, :]
    return qk, v, conv, gate, gb, conv_new


def _step_layout(nb, t, c):
    ns = STEP_ROWS // c
    if ns == 1:
        par = max(p for p in (4, 2, 1) if nb % p == 0)
        return nb, t, ns, par
    assert t == c and nb % ns == 0, (nb, t, c)
    return nb // ns, STEP_ROWS, ns, 1


def _retention_kernel(q_ref, k_ref, v_ref, dintra_ref, dq_ref, dk_ref, dc_ref, s0_ref, o_ref, s_ref,
                      *, par, ns, c):
    @pl.when(pl.program_id(1) == 0)
    def _():
        s_ref[...] = s0_ref[...]

    units = [(p, h) for p in range(par) for h in range(RET_HEADS)]
    ids = range(len(units))
    dk_sl = lambda h: slice(h * RET_DK, (h + 1) * RET_DK)
    dv_sl = lambda h: slice(h * RET_DV, (h + 1) * RET_DV)
    qb = [q_ref[p, :, dk_sl(h)] for p, h in units]
    kb = [k_ref[p, :, dk_sl(h)] for p, h in units]
    vb = [v_ref[p, :, dv_sl(h)] for p, h in units]
    scores = [_dot_nt(qb[u], kb[u]) * dintra_ref[units[u][1]] for u in ids]
    kd = [kb[u].astype(F32) * dk_ref[units[u][1]] for u in ids]
    if ns == 1:
        s = [s_ref[p, h] for p, h in units]
        qs = [_dot(qb[u], _bf(s[u])) for u in ids]
        upd = [_dot(_bf(_rows([kd[u].T, scores[u]])), vb[u]) for u in ids]
        for u, (p, h) in enumerate(units):
            o_ref[p, :, dv_sl(h)] = upd[u][RET_DK:] + qs[u] * dq_ref[h]
            s_ref[p, h] = s[u] * dc_ref[h] + upd[u][:RET_DK]
    else:
        seqs = range(ns)
        rows = lambda x, j: x[j * c:(j + 1) * c]
        q32 = [qb[u].astype(F32) for u in ids]
        v32 = [vb[u].astype(F32) for u in ids]
        s = [[s_ref[j, h] for j in seqs] for _, h in units]
        qs = [_rows([_dot(_bf(rows(q32[u], j)), _bf(s[u][j])) for j in seqs]) for u in ids]
        intra = [_dot(_bf(scores[u]), vb[u]) for u in ids]
        for u, (p, h) in enumerate(units):
            o_ref[p, :, dv_sl(h)] = intra[u] + qs[u] * dq_ref[h]
        ktv = [[_dot_tn(_bf(rows(kd[u], j)), _bf(rows(v32[u], j))) for j in seqs] for u in ids]
        for u, (p, h) in enumerate(units):
            for j in seqs:
                s_ref[j, h] = s[u][j] * dc_ref[h] + ktv[u][j]


def _retention(qk, v, s0, t, c):
    nb = s0.shape[0]
    groups, rows, ns, par = _step_layout(nb, t, c)
    log_g = jnp.log1p(-jnp.exp2(-5.0 - jnp.arange(RET_HEADS, dtype=F32)))
    idx = jnp.arange(STEP_ROWS)
    same = (idx[:, None] // c) == (idx[None, :] // c)
    off = (idx % c).astype(F32)
    diff = off[:, None] - off[None, :]
    causal = same & (diff >= 0)
    d_intra = jnp.where(causal[None], jnp.exp(log_g[:, None, None] * jnp.where(causal, diff, 0.0)[None]), 0.0)
    d_q = jnp.broadcast_to(jnp.exp(log_g[:, None] * (off + 1.0)[None, :])[..., None], (RET_HEADS, STEP_ROWS, RET_DV))
    d_k = jnp.broadcast_to(jnp.exp(log_g[:, None] * (c - 1.0 - off)[None, :])[..., None],
                           (RET_HEADS, STEP_ROWS, RET_DK))
    d_c = jnp.broadcast_to(jnp.exp(log_g * c)[:, None, None], (RET_HEADS, 1, RET_DV))
    blk = lambda cols, col_blk: pl.BlockSpec((par, STEP_ROWS, cols), lambda g, i: (g, i, col_blk))
    state_spec = pl.BlockSpec((par * ns, RET_HEADS, RET_DK, RET_DV), lambda g, i: (g, 0, 0, 0))
    qk3 = qk.reshape(groups, rows, 2 * RET_QK)
    o, s_new = pl.pallas_call(
        functools.partial(_retention_kernel, par=par, ns=ns, c=c),
        grid=(groups // par, rows // STEP_ROWS),
        in_specs=[
            blk(RET_QK, 0),
            blk(RET_QK, 1),
            blk(RET_V, 0),
            _const_spec((RET_HEADS, STEP_ROWS, STEP_ROWS)),
            _const_spec((RET_HEADS, STEP_ROWS, RET_DV)),
            _const_spec((RET_HEADS, STEP_ROWS, RET_DK)),
            _const_spec((RET_HEADS, 1, RET_DV)),
            state_spec,
        ],
        out_specs=[blk(RET_V, 0), state_spec],
        out_shape=[jax.ShapeDtypeStruct((groups, rows, RET_V), F32), jax.ShapeDtypeStruct(s0.shape, F32)],
        compiler_params=_params(("arbitrary", "arbitrary")),
        name="retention",
    )(qk3, qk3, v.reshape(groups, rows, RET_V), d_intra, d_q, d_k, d_c, s0)
    return o.reshape(nb * t, RET_V), s_new


def _solve_correction(a, c):
    heads = range(len(a))
    m = a[0].shape[0]
    n = [-x for x in a]
    ab = [_bf(x) for x in a]
    p = [_dot(ab[h], ab[h]) for h in heads]
    span = 2
    while span < c:
        pb = [_bf(x) for x in p]
        last = 2 * span >= c
        if last:
            prod = [_dot(_bf(n[h]), pb[h]) for h in heads]
            n = [n[h] + p[h] + prod[h] for h in heads]
        else:
            prod = [_dot(_bf(_rows([n[h], p[h]])), pb[h]) for h in heads]
            n = [n[h] + p[h] + prod[h][:m] for h in heads]
            p = [prod[h][m:] for h in heads]
        span *= 2
    return n


def _gdn_kernel(qkv_ref, gb_ref, s0_ref, o_ref, s_ref, *, par, ns, c):
    @pl.when(pl.program_id(1) == 0)
    def _():
        s_ref[...] = s0_ref[...]

    m = STEP_ROWS

    def l2norm(x):
        return x * lax.rsqrt(jnp.sum(x * x, axis=-1, keepdims=True) + EPS)

    lane = lax.broadcasted_iota(jnp.int32, (m, LANES), 1)
    off = lax.broadcasted_iota(jnp.int32, (m, LANES), 0) % c
    r = lax.broadcasted_iota(jnp.int32, (m, m), 0)
    cc = lax.broadcasted_iota(jnp.int32, (m, m), 1)
    same = (r // c) == (cc // c)
    tril = same & (r >= cc)
    strict = same & (r > cc)

    def gates(p):
        cum = jnp.where(lane < GDN_HEADS, gb_ref[p, :, :LANES], 0.0)
        shift = 1
        while shift < c:
            cum = cum + jnp.where(off >= shift, pltpu.roll(cum, shift, 0), 0.0)
            shift *= 2
        last = cum.reshape(ns, c, LANES)[:, c - 1:c, :]
        e_rest = jnp.exp((last - cum.reshape(ns, c, LANES)).reshape(m, LANES))
        return dict(cum=cum, cum_t=cum.T, e_cum=jnp.exp(cum), e_rest=e_rest, e_last=jnp.exp(last),
                    beta=gb_ref[p, :, LANES:])

    gt = [gates(p) for p in range(par)]
    units = [(p, h) for p in range(par) for h in range(GDN_HEADS)]
    ids = range(len(units))
    col = lambda x, h: x[:, h:h + 1]
    q = [l2norm(qkv_ref[p, :, h * GDN_DK:(h + 1) * GDN_DK]) * (GDN_DK ** -0.5) for p, h in units]
    k = [l2norm(qkv_ref[p, :, GDN_QK + h * GDN_DK:GDN_QK + (h + 1) * GDN_DK]) for p, h in units]
    v = [qkv_ref[p, :, 2 * GDN_QK + h * GDN_DV:2 * GDN_QK + (h + 1) * GDN_DV] for p, h in units]
    beta = [col(gt[p]["beta"], h) for p, h in units]
    eg = [col(gt[p]["e_cum"], h) for p, h in units]
    decay = [jnp.exp(jnp.where(tril, col(gt[p]["cum"], h) - gt[p]["cum_t"][h:h + 1, :], -jnp.inf)) for p, h in units]
    kb = [k[u] * beta[u] for u in ids]
    kbf = [_bf(k[u]) for u in ids]
    raw = [_dot_nt(_bf(_rows([kb[u], q[u]])), kbf[u]) for u in ids]
    a = [jnp.where(strict, raw[u][:m] * decay[u], 0.0) for u in ids]
    attn = [raw[u][m:] * decay[u] for u in ids]
    n = _solve_correction(a, c)
    rhs = [jnp.concatenate([v[u] * beta[u], kb[u] * eg[u]], axis=-1) for u in ids]
    sol = [rhs[u] + _dot(_bf(n[u]), _bf(rhs[u])) for u in ids]
    uu = [sol[u][:, :GDN_DV] for u in ids]
    w = [sol[u][:, GDN_DV:] for u in ids]
    qe = [q[u] * eg[u] for u in ids]
    kd = [k[u] * col(gt[p]["e_rest"], h) for u, (p, h) in enumerate(units)]
    dv_sl = lambda h: slice(h * GDN_DV, (h + 1) * GDN_DV)
    if ns == 1:
        s = [s_ref[p, h] for p, h in units]
        ws_qs = [_dot(_bf(_rows([w[u], qe[u]])), _bf(s[u])) for u in ids]
        v_new = [uu[u] - ws_qs[u][:m] for u in ids]
        upd = [_dot(_bf(_rows([kd[u].T, attn[u]])), _bf(v_new[u])) for u in ids]
        for u, (p, h) in enumerate(units):
            o_ref[p, :, dv_sl(h)] = ws_qs[u][m:] + upd[u][GDN_DK:]
            s_ref[p, h] = s[u] * gt[p]["e_last"][0, :, h:h + 1] + upd[u][:GDN_DK]
    else:
        seqs = range(ns)
        rows = lambda x, j: x[j * c:(j + 1) * c]
        s = [[s_ref[j, h] for j in seqs] for _, h in units]
        ws_qs = [[_dot(_bf(_rows([rows(w[u], j), rows(qe[u], j)])), _bf(s[u][j])) for j in seqs] for u in ids]
        v_new = [_rows([rows(uu[u], j) - ws_qs[u][j][:c] for j in seqs]) for u in ids]
        qs = [_rows([ws_qs[u][j][c:] for j in seqs]) for u in ids]
        intra = [_dot(_bf(attn[u]), _bf(v_new[u])) for u in ids]
        for u, (p, h) in enumerate(units):
            o_ref[p, :, dv_sl(h)] = qs[u] + intra[u]
        ktv = [[_dot_tn(_bf(rows(kd[u], j)), _bf(rows(v_new[u], j))) for j in seqs] for u in ids]
        for u, (p, h) in enumerate(units):
            for j in seqs:
                s_ref[j, h] = s[u][j] * gt[p]["e_last"][j, :, h:h + 1] + ktv[u][j]


def _gdn(conv, gb, s0, t, c):
    nb = s0.shape[0]
    groups, rows, ns, par = _step_layout(nb, t, c)
    blk = lambda cols: pl.BlockSpec((par, STEP_ROWS, cols), lambda g, i: (g, i, 0))
    state_spec = pl.BlockSpec((par * ns, GDN_HEADS, GDN_DK, GDN_DV), lambda g, i: (g, 0, 0, 0))
    o, s_new = pl.pallas_call(
        functools.partial(_gdn_kernel, par=par, ns=ns, c=c),
        grid=(groups // par, rows // STEP_ROWS),
        in_specs=[blk(CONV_CH), blk(AB_W), state_spec],
        out_specs=[blk(GDN_V), state_spec],
        out_shape=[jax.ShapeDtypeStruct((groups, rows, GDN_V), F32), jax.ShapeDtypeStruct(s0.shape, F32)],
        compiler_params=_params(("arbitrary", "arbitrary")),
        name="gdn",
    )(conv.reshape(groups, rows, CONV_CH), gb.reshape(groups, rows, AB_W), s0)
    return o.reshape(nb * t, GDN_V), s_new


def _head_norm(o, heads):
    d = o.shape[-1] // heads
    parts = []
    for h in range(heads):
        oh = o[:, h * d:(h + 1) * d]
        parts.append(oh * lax.rsqrt(jnp.mean(oh * oh, axis=-1, keepdims=True) + EPS))
    return jnp.concatenate(parts, axis=-1)


def _attn_out_kernel(x_ref, or_ref, og_ref, gr_ref, z_ref, gater_ref, gateg_ref, rn_ref, gn_ref,
                     wr_ref, wg_ref, wo_ref, o_ref):
    o_r = _head_norm(or_ref[...], RET_HEADS) * rn_ref[...] * gr_ref[...].astype(F32)
    br_r = _dot(_bf(o_r), wr_ref[...])
    o_g = _head_norm(og_ref[...], GDN_HEADS) * gn_ref[...] * z_ref[...].astype(F32)
    br_g = _dot(_bf(o_g), wg_ref[...])
    mix = gater_ref[...].astype(F32) * br_r + gateg_ref[...].astype(F32) * br_g
    o_ref[...] = x_ref[...] + _dot(_bf(mix), wo_ref[...])


def _attn_out(x2d, o_r, o_g, gate, ret_norm, gdn_norm, w_ret_br, w_gdn_br, w_out):
    n = x2d.shape[0]
    tm = min(512, n)
    rows = lambda blk: pl.BlockSpec((tm, D_MODEL), lambda i: (i, blk))
    wspec = _const_spec((D_MODEL, D_MODEL))
    return pl.pallas_call(
        _attn_out_kernel,
        grid=(n // tm,),
        in_specs=[rows(0), rows(0), rows(0), rows(0), rows(1), rows(2), rows(3),
                  _const_spec((1, D_MODEL)), _const_spec((1, D_MODEL)), wspec, wspec, wspec],
        out_specs=rows(0),
        out_shape=jax.ShapeDtypeStruct((n, D_MODEL), F32),
        compiler_params=_params(("arbitrary",)),
        name="attn_out",
    )(x2d, o_r, o_g, gate, gate, gate, gate, ret_norm, gdn_norm, w_ret_br, w_gdn_br, w_out)


FF_BLOCK = 1024


def _mlp_kernel(x_ref, ln2_ref, wu_ref, wd_ref, lnf_ref, o_ref):
    x = x_ref[...]
    hb = _bf(_rms(x, ln2_ref[...]))
    acc = x
    for f in range(D_FF // FF_BLOCK):
        sl = slice(f * FF_BLOCK, (f + 1) * FF_BLOCK)
        up = jnp.maximum(_dot(hb, wu_ref[:, sl]), 0.0)
        acc = acc + _dot(_bf(up * up), wd_ref[sl, :])
    o_ref[...] = _rms(acc, lnf_ref[...])


def _mlp(x2d, ln2, w_up, w_down, ln_f):
    n = x2d.shape[0]
    tm = min(512, n)
    rows = pl.BlockSpec((tm, D_MODEL), lambda i: (i, 0))
    return pl.pallas_call(
        _mlp_kernel,
        grid=(n // tm,),
        in_specs=[rows, _const_spec((1, D_MODEL)), _const_spec((D_MODEL, D_FF)), _const_spec((D_FF, D_MODEL)),
                  _const_spec((1, D_MODEL))],
        out_specs=rows,
        out_shape=jax.ShapeDtypeStruct((n, D_MODEL), F32),
        compiler_params=_params(("arbitrary",)),
        name="mlp",
    )(x2d, ln2, w_up, w_down, ln_f)


def _group(x, pos, s_ret, s_gdn, s_conv, wts):
    (in_wts, ret_norm, gdn_norm, w_ret_br, w_gdn_br, w_out, ln2, w_up, w_down, ln_f) = wts
    nb, t, _ = x.shape
    c = CHUNK if t % CHUNK == 0 else t
    assert STEP_ROWS % c == 0 and nb % (STEP_ROWS // c) == 0, (nb, t)
    x2d = x.reshape(nb * t, D_MODEL)
    qk, v, conv, gate, gb, conv_new = _in_proj(x2d, t, pos, s_conv, in_wts)
    o_r, ret_new = _retention(qk, v, s_ret, t, c)
    o_g, gdn_new = _gdn(conv, gb, s_gdn, t, c)
    x1 = _attn_out(x2d, o_r, o_g, gate, ret_norm, gdn_norm, w_ret_br, w_gdn_br, w_out)
    y = _mlp(x1, ln2, w_up, w_down, ln_f)
    return y.reshape(x.shape), ret_new, gdn_new, conv_new


def kernel(x_prompt, x_sample, state_ret, state_gdn, state_conv, ln1, w_in, conv_w, a_log, dt_bias, ret_norm,
           gdn_norm, w_ret_br, w_gdn_br, w_out, ln2, w_up, w_down, ln_f):
    depth = w_in.shape[0]
    assert depth == 1, "single-layer trunk"
    bp, tp, _ = x_prompt.shape
    ts = x_sample.shape[1]
    w = w_in[0]
    o_gr = 2 * RET_QK + RET_V
    o_qkv = o_gr + RET_V
    o_a = o_qkv + CONV_CH
    o_z = o_a + 2 * GDN_HEADS
    w_qkv = w[:, :o_gr].astype(BF16)
    w_conv = w[:, o_qkv:o_a].astype(BF16)
    w_gate = jnp.concatenate([w[:, o_gr:o_qkv], w[:, o_z:]], axis=1).astype(BF16)
    w_ab = jnp.zeros((D_MODEL, AB_W), F32)
    w_ab = w_ab.at[:, :GDN_HEADS].set(w[:, o_a:o_a + GDN_HEADS])
    w_ab = w_ab.at[:, LANES:LANES + GDN_HEADS].set(w[:, o_a + GDN_HEADS:o_z]).astype(BF16)
    vec = lambda v: v.reshape(1, -1)
    in_wts = (vec(ln1[0]), w_qkv, w_conv, w_gate, w_ab, conv_w[0], a_log[0], dt_bias[0])
    wts = (in_wts, vec(ret_norm[0]), vec(gdn_norm[0]), w_ret_br[0].astype(BF16), w_gdn_br[0].astype(BF16),
           w_out[0].astype(BF16), vec(ln2[0]), w_up[0].astype(BF16), w_down[0].astype(BF16), vec(ln_f))
    pos_p = jnp.arange(tp, dtype=F32)
    pos_s = PAST_LEN + jnp.arange(ts, dtype=F32)
    zeros = lambda *shape: jnp.zeros((bp,) + shape, F32)
    yp, ret_p, gdn_p, conv_p = _group(x_prompt, pos_p, zeros(RET_HEADS, RET_DK, RET_DV),
                                      zeros(GDN_HEADS, GDN_DK, GDN_DV), zeros(CONV_W - 1, CONV_CH), wts)
    ys, ret_s, gdn_s, conv_s = _group(x_sample, pos_s, state_ret[0], state_gdn[0], state_conv[0], wts)
    return (yp, ys, ret_p[None], gdn_p[None], conv_p[None], ret_s[None], gdn_s[None], conv_s[None])
```

```python
import functools

import jax
import jax.numpy as jnp
from jax import lax
from jax.experimental import pallas as pl
from jax.experimental.pallas import tpu as pltpu

F32 = jnp.float32
BF16 = jnp.bfloat16

D_MODEL = 1024
PAST_LEN = 16384
RET_HEADS, RET_DK, RET_DV = 4, 128, 256
RET_QK = RET_HEADS * RET_DK
RET_V = RET_HEADS * RET_DV
GDN_HEADS, GDN_DK, GDN_DV = 8, 128, 128
GDN_QK = GDN_HEADS * GDN_DK
GDN_V = GDN_HEADS * GDN_DV
CONV_W = 4
CONV_CH = 2 * GDN_QK + GDN_V
D_FF = 4 * D_MODEL
CHUNK = 64
ROPE_BASE = 10000.0
EPS = 1e-6

SUBLANES = 8
LANES = 128
MXU_COLS = 256
STEP_ROWS = 64
VMEM_LIMIT = 56 * 1024 * 1024

QKV_W = 2 * RET_QK + RET_V
GATE_W = 4 * D_MODEL
AB_W = 2 * LANES


def _params(sem):
    return pltpu.CompilerParams(dimension_semantics=sem, vmem_limit_bytes=VMEM_LIMIT)


def _const_spec(shape):
    zeros = (0,) * len(shape)
    return pl.BlockSpec(shape, lambda *_: zeros, pipeline_mode=pl.Buffered(1))


def _dot(a, b):
    return jnp.dot(a, b, preferred_element_type=F32)


def _dot_nt(a, b):
    return lax.dot_general(a, b, (((1,), (1,)), ((), ())), preferred_element_type=F32)


def _dot_tn(a, b):
    return lax.dot_general(a, b, (((0,), (0,)), ((), ())), preferred_element_type=F32)


def _bf(x):
    return x.astype(BF16)


def _rows(parts):
    return jnp.concatenate(parts, axis=0)


def _rms(x, g):
    return x * lax.rsqrt(jnp.mean(x * x, axis=-1, keepdims=True) + EPS) * g


def _silu(x):
    return x * jax.nn.sigmoid(x)


def _in_proj_kernel(x_ref, ln_ref, wqkv_ref, wconv_ref, wab_ref, cos_ref, sin_ref, cw_ref, hist_ref,
                    alog_ref, dtb_ref, qk_ref, v_ref, conv_ref, gb_ref, tail_ref, h_ref, carry_ref,
                    *, seq_is_group, tiles_per_seq):
    i = pl.program_id(0)
    tm = x_ref.shape[0]
    groups = tm // SUBLANES

    @pl.when(i == 0)
    def _():
        carry_ref[...] = jnp.zeros_like(carry_ref)

    h_ref[...] = _bf(_rms(x_ref[...], ln_ref[...]))
    ab = _dot(h_ref[...], wab_ref[...])
    gb_ref[:, :LANES] = -jnp.exp(alog_ref[...]) * jax.nn.softplus(ab[:, :LANES] + dtb_ref[...])
    gb_ref[:, LANES:] = jax.nn.sigmoid(ab[:, LANES:])

    def rope_slab(s):
        sl = slice(s * MXU_COLS, (s + 1) * MXU_COLS)
        r = _dot(h_ref[...], wqkv_ref[:, sl])
        for part in range(MXU_COLS // RET_DK):
            x = r[:, part * RET_DK:(part + 1) * RET_DK]
            y = x * cos_ref[...] + pltpu.roll(x, RET_DK // 2, 1) * sin_ref[...]
            start = sl.start + part * RET_DK
            if start >= RET_QK:
                y = y * (RET_DK ** -0.5)
            qk_ref[:, start:start + RET_DK] = _bf(y)

    def v_slab(s):
        sl = slice(s * MXU_COLS, (s + 1) * MXU_COLS)
        v_ref[:, sl] = _bf(_dot(h_ref[...], wqkv_ref[:, 2 * RET_QK + sl.start:2 * RET_QK + sl.stop]))

    sub = lax.broadcasted_iota(jnp.int32, (1, SUBLANES, MXU_COLS), 1)

    def conv_slab(s):
        sl = slice(s * MXU_COLS, (s + 1) * MXU_COLS)
        w = [cw_ref[tap:tap + 1, sl][None] for tap in range(CONV_W)]
        r3 = _dot(h_ref[...], wconv_ref[:, sl]).reshape(groups, SUBLANES, MXU_COLS)
        cur1 = pltpu.roll(r3, 1, 1)
        if seq_is_group:
            hist3 = hist_ref[:, sl].reshape(groups, SUBLANES, MXU_COLS)
            tail_ref[:, sl] = r3.reshape(tm, MXU_COLS)
            prev1 = pltpu.roll(hist3, 1, 1)
            pair_prev = w[1] * hist3 + w[0] * prev1
        else:
            first = jnp.where(i % tiles_per_seq == 0, hist_ref[:, :, sl], carry_ref[:, sl][None])
            tail_ref[:, :, sl] = r3[groups - 1:groups]
            carry_ref[:, sl] = r3[groups - 1]
            first1 = pltpu.roll(first, 1, 1)
            prev1 = jnp.concatenate([first1, cur1[:groups - 1]], axis=0)
            pair_first = w[1] * first + w[0] * first1
        x1 = jnp.where(sub < 1, prev1, cur1)
        pair = w[1] * r3 + w[0] * x1
        cur2 = pltpu.roll(pair, 2, 1)
        if seq_is_group:
            prev2 = pltpu.roll(pair_prev, 2, 1)
        else:
            prev2 = jnp.concatenate([pltpu.roll(pair_first, 2, 1), cur2[:groups - 1]], axis=0)
        y = (w[3] * r3 + w[2] * x1) + jnp.where(sub < 2, prev2, cur2)
        conv_ref[:, sl] = _silu(y).reshape(tm, MXU_COLS)

    light = ([functools.partial(rope_slab, s) for s in range(2 * RET_QK // MXU_COLS)]
             + [functools.partial(v_slab, s) for s in range(RET_V // MXU_COLS)])
    for s in range(CONV_CH // MXU_COLS):
        conv_slab(s)
        if s < len(light):
            light[s]()


def _in_proj(x2d, t, pos, conv0, wts):
    ln1, w_qkv, w_conv, w_ab, conv_w, a_log, dt_bias = wts
    n = x2d.shape[0]
    nb = n // t
    seq_is_group = t == SUBLANES
    tm = min(256, n) if seq_is_group else min(512, t)
    assert n % tm == 0 and (seq_is_group or t % tm == 0), (n, t, tm)
    tiles_per_seq = max(t // tm, 1)
    inv = ROPE_BASE ** (-jnp.arange(0, RET_DK, 2, dtype=F32) / RET_DK)
    ang = jnp.tile(pos, max(tm // t, 1))[:, None] * inv[None, :]
    cosf = jnp.concatenate([jnp.cos(ang), jnp.cos(ang)], axis=-1)
    sinf = jnp.concatenate([-jnp.sin(ang), jnp.sin(ang)], axis=-1)
    pos_blocks = cosf.shape[0] // tm
    hist = jnp.pad(conv0, ((0, 0), (SUBLANES - (CONV_W - 1), 0), (0, 0)))
    rows = lambda cols: pl.BlockSpec((tm, cols), lambda i: (i, 0))
    if seq_is_group:
        hist = hist.reshape(n, CONV_CH)
        hist_spec = rows(CONV_CH)
        tail_spec = hist_spec
        tail_shape = jax.ShapeDtypeStruct((n, CONV_CH), F32)
    else:
        hist_spec = pl.BlockSpec((1, SUBLANES, CONV_CH), lambda i: (i // tiles_per_seq, 0, 0))
        tail_spec = pl.BlockSpec((1, SUBLANES, CONV_CH), lambda i: (i, 0, 0))
        tail_shape = jax.ShapeDtypeStruct((n // tm, SUBLANES, CONV_CH), F32)
    pad_lanes = lambda x: jnp.pad(x.reshape(1, GDN_HEADS), ((0, 0), (0, LANES - GDN_HEADS)))
    pos_spec = pl.BlockSpec((tm, RET_DK), lambda i: (i % pos_blocks, 0))
    qk, v, conv, gb, tail = pl.pallas_call(
        functools.partial(_in_proj_kernel, seq_is_group=seq_is_group, tiles_per_seq=tiles_per_seq),
        grid=(n // tm,),
        in_specs=[
            rows(D_MODEL),
            _const_spec((1, D_MODEL)),
            _const_spec((D_MODEL, QKV_W)),
            _const_spec((D_MODEL, CONV_CH)),
            _const_spec((D_MODEL, AB_W)),
            pos_spec,
            pos_spec,
            _const_spec((CONV_W, CONV_CH)),
            hist_spec,
            _const_spec((1, LANES)),
            _const_spec((1, LANES)),
        ],
        out_specs=[rows(2 * RET_QK), rows(RET_V), rows(CONV_CH), rows(AB_W), tail_spec],
        out_shape=[
            jax.ShapeDtypeStruct((n, 2 * RET_QK), BF16),
            jax.ShapeDtypeStruct((n, RET_V), BF16),
            jax.ShapeDtypeStruct((n, CONV_CH), F32),
            jax.ShapeDtypeStruct((n, AB_W), F32),
            tail_shape,
        ],
        scratch_shapes=[pltpu.VMEM((tm, D_MODEL), BF16), pltpu.VMEM((SUBLANES, CONV_CH), F32)],
        compiler_params=_params(("arbitrary",)),
        name="in_proj",
    )(x2d, ln1, w_qkv, w_conv, w_ab, cosf, sinf, conv_w, hist, pad_lanes(a_log), pad_lanes(dt_bias))
    conv_new = tail.reshape(nb, -1, SUBLANES, CONV_CH)[:, -1, SUBLANES - (CONV_W - 1):, :]
    return qk, v, conv, gb, conv_new


def _step_layout(nb, t, c):
    ns = STEP_ROWS // c
    if ns == 1:
        par = max(p for p in (4, 2, 1) if nb % p == 0)
        return nb, t, ns, par
    assert t == c and nb % ns == 0, (nb, t, c)
    return nb // ns, STEP_ROWS, ns, 1


def _retention_kernel(q_ref, k_ref, v_ref, dintra_ref, dq_ref, dk_ref, dc_ref, s0_ref, o_ref, s_ref,
                      *, par, ns, c):
    @pl.when(pl.program_id(1) == 0)
    def _():
        s_ref[...] = s0_ref[...]

    units = [(p, h) for p in range(par) for h in range(RET_HEADS)]
    ids = range(len(units))
    dk_sl = lambda h: slice(h * RET_DK, (h + 1) * RET_DK)
    dv_sl = lambda h: slice(h * RET_DV, (h + 1) * RET_DV)
    qb = [q_ref[p, :, dk_sl(h)] for p, h in units]
    kb = [k_ref[p, :, dk_sl(h)] for p, h in units]
    vb = [v_ref[p, :, dv_sl(h)] for p, h in units]
    scores = [_dot_nt(qb[u], kb[u]) * dintra_ref[units[u][1]] for u in ids]
    kd = [kb[u].astype(F32) * dk_ref[units[u][1]] for u in ids]
    if ns == 1:
        s = [s_ref[p, h] for p, h in units]
        qs = [_dot(qb[u], _bf(s[u])) for u in ids]
        upd = [_dot(_bf(_rows([kd[u].T, scores[u]])), vb[u]) for u in ids]
        for u, (p, h) in enumerate(units):
            o_ref[p, :, dv_sl(h)] = upd[u][RET_DK:] + qs[u] * dq_ref[h]
            s_ref[p, h] = s[u] * dc_ref[h] + upd[u][:RET_DK]
    else:
        seqs = range(ns)
        rows = lambda x, j: x[j * c:(j + 1) * c]
        q32 = [qb[u].astype(F32) for u in ids]
        v32 = [vb[u].astype(F32) for u in ids]
        s = [[s_ref[j, h] for j in seqs] for _, h in units]
        qs = [_rows([_dot(_bf(rows(q32[u], j)), _bf(s[u][j])) for j in seqs]) for u in ids]
        intra = [_dot(_bf(scores[u]), vb[u]) for u in ids]
        for u, (p, h) in enumerate(units):
            o_ref[p, :, dv_sl(h)] = intra[u] + qs[u] * dq_ref[h]
        ktv = [[_dot_tn(_bf(rows(kd[u], j)), _bf(rows(v32[u], j))) for j in seqs] for u in ids]
        for u, (p, h) in enumerate(units):
            for j in seqs:
                s_ref[j, h] = s[u][j] * dc_ref[h] + ktv[u][j]


def _retention(qk, v, s0, t, c):
    nb = s0.shape[0]
    groups, rows, ns, par = _step_layout(nb, t, c)
    log_g = jnp.log1p(-jnp.exp2(-5.0 - jnp.arange(RET_HEADS, dtype=F32)))
    idx = jnp.arange(STEP_ROWS)
    same = (idx[:, None] // c) == (idx[None, :] // c)
    off = (idx % c).astype(F32)
    diff = off[:, None] - off[None, :]
    causal = same & (diff >= 0)
    d_intra = jnp.where(causal[None], jnp.exp(log_g[:, None, None] * jnp.where(causal, diff, 0.0)[None]), 0.0)
    d_q = jnp.broadcast_to(jnp.exp(log_g[:, None] * (off + 1.0)[None, :])[..., None], (RET_HEADS, STEP_ROWS, RET_DV))
    d_k = jnp.broadcast_to(jnp.exp(log_g[:, None] * (c - 1.0 - off)[None, :])[..., None],
                           (RET_HEADS, STEP_ROWS, RET_DK))
    d_c = jnp.broadcast_to(jnp.exp(log_g * c)[:, None, None], (RET_HEADS, 1, RET_DV))
    blk = lambda cols, col_blk: pl.BlockSpec((par, STEP_ROWS, cols), lambda g, i: (g, i, col_blk))
    state_spec = pl.BlockSpec((par * ns, RET_HEADS, RET_DK, RET_DV), lambda g, i: (g, 0, 0, 0))
    qk3 = qk.reshape(groups, rows, 2 * RET_QK)
    o, s_new = pl.pallas_call(
        functools.partial(_retention_kernel, par=par, ns=ns, c=c),
        grid=(groups // par, rows // STEP_ROWS),
        in_specs=[
            blk(RET_QK, 0),
            blk(RET_QK, 1),
            blk(RET_V, 0),
            _const_spec((RET_HEADS, STEP_ROWS, STEP_ROWS)),
            _const_spec((RET_HEADS, STEP_ROWS, RET_DV)),
            _const_spec((RET_HEADS, STEP_ROWS, RET_DK)),
            _const_spec((RET_HEADS, 1, RET_DV)),
            state_spec,
        ],
        out_specs=[blk(RET_V, 0), state_spec],
        out_shape=[jax.ShapeDtypeStruct((groups, rows, RET_V), F32), jax.ShapeDtypeStruct(s0.shape, F32)],
        compiler_params=_params(("arbitrary", "arbitrary")),
        name="retention",
    )(qk3, qk3, v.reshape(groups, rows, RET_V), d_intra, d_q, d_k, d_c, s0)
    return o.reshape(nb * t, RET_V), s_new


def _solve_correction(a, c):
    heads = range(len(a))
    m = a[0].shape[0]
    n = [-x for x in a]
    ab = [_bf(x) for x in a]
    p = [_dot(ab[h], ab[h]) for h in heads]
    span = 2
    while span < c:
        pb = [_bf(x) for x in p]
        last = 2 * span >= c
        if last:
            prod = [_dot(_bf(n[h]), pb[h]) for h in heads]
            n = [n[h] + p[h] + prod[h] for h in heads]
        else:
            prod = [_dot(_bf(_rows([n[h], p[h]])), pb[h]) for h in heads]
            n = [n[h] + p[h] + prod[h][:m] for h in heads]
            p = [prod[h][m:] for h in heads]
        span *= 2
    return n


def _gdn_kernel(qkv_ref, gb_ref, s0_ref, o_ref, s_ref, *, par, ns, c):
    @pl.when(pl.program_id(1) == 0)
    def _():
        s_ref[...] = s0_ref[...]

    m = STEP_ROWS

    def l2norm(x):
        return x * lax.rsqrt(jnp.sum(x * x, axis=-1, keepdims=True) + EPS)

    lane = lax.broadcasted_iota(jnp.int32, (m, LANES), 1)
    off = lax.broadcasted_iota(jnp.int32, (m, LANES), 0) % c
    r = lax.broadcasted_iota(jnp.int32, (m, m), 0)
    cc = lax.broadcasted_iota(jnp.int32, (m, m), 1)
    same = (r // c) == (cc // c)
    tril = same & (r >= cc)
    strict = same & (r > cc)

    def gates(p):
        cum = jnp.where(lane < GDN_HEADS, gb_ref[p, :, :LANES], 0.0)
        shift = 1
        while shift < c:
            cum = cum + jnp.where(off >= shift, pltpu.roll(cum, shift, 0), 0.0)
            shift *= 2
        last = cum.reshape(ns, c, LANES)[:, c - 1:c, :]
        e_rest = jnp.exp((last - cum.reshape(ns, c, LANES)).reshape(m, LANES))
        return dict(cum=cum, cum_t=cum.T, e_cum=jnp.exp(cum), e_rest=e_rest, e_last=jnp.exp(last),
                    beta=gb_ref[p, :, LANES:])

    gt = [gates(p) for p in range(par)]
    units = [(p, h) for p in range(par) for h in range(GDN_HEADS)]
    ids = range(len(units))
    col = lambda x, h: x[:, h:h + 1]
    q = [l2norm(qkv_ref[p, :, h * GDN_DK:(h + 1) * GDN_DK]) * (GDN_DK ** -0.5) for p, h in units]
    k = [l2norm(qkv_ref[p, :, GDN_QK + h * GDN_DK:GDN_QK + (h + 1) * GDN_DK]) for p, h in units]
    v = [qkv_ref[p, :, 2 * GDN_QK + h * GDN_DV:2 * GDN_QK + (h + 1) * GDN_DV] for p, h in units]
    beta = [col(gt[p]["beta"], h) for p, h in units]
    eg = [col(gt[p]["e_cum"], h) for p, h in units]
    decay = [jnp.exp(jnp.where(tril, col(gt[p]["cum"], h) - gt[p]["cum_t"][h:h + 1, :], -jnp.inf)) for p, h in units]
    kb = [k[u] * beta[u] for u in ids]
    kbf = [_bf(k[u]) for u in ids]
    raw = [_dot_nt(_bf(_rows([kb[u], q[u]])), kbf[u]) for u in ids]
    a = [jnp.where(strict, raw[u][:m] * decay[u], 0.0) for u in ids]
    attn = [raw[u][m:] * decay[u] for u in ids]
    n = _solve_correction(a, c)
    rhs = [jnp.concatenate([v[u] * beta[u], kb[u] * eg[u]], axis=-1) for u in ids]
    sol = [rhs[u] + _dot(_bf(n[u]), _bf(rhs[u])) for u in ids]
    uu = [sol[u][:, :GDN_DV] for u in ids]
    w = [sol[u][:, GDN_DV:] for u in ids]
    qe = [q[u] * eg[u] for u in ids]
    kd = [k[u] * col(gt[p]["e_rest"], h) for u, (p, h) in enumerate(units)]
    dv_sl = lambda h: slice(h * GDN_DV, (h + 1) * GDN_DV)
    if ns == 1:
        s = [s_ref[p, h] for p, h in units]
        ws_qs = [_dot(_bf(_rows([w[u], qe[u]])), _bf(s[u])) for u in ids]
        v_new = [uu[u] - ws_qs[u][:m] for u in ids]
        upd = [_dot(_bf(_rows([kd[u].T, attn[u]])), _bf(v_new[u])) for u in ids]
        for u, (p, h) in enumerate(units):
            o_ref[p, :, dv_sl(h)] = ws_qs[u][m:] + upd[u][GDN_DK:]
            s_ref[p, h] = s[u] * gt[p]["e_last"][0, :, h:h + 1] + upd[u][:GDN_DK]
    else:
        seqs = range(ns)
        rows = lambda x, j: x[j * c:(j + 1) * c]
        s = [[s_ref[j, h] for j in seqs] for _, h in units]
        ws_qs = [[_dot(_bf(_rows([rows(w[u], j), rows(qe[u], j)])), _bf(s[u][j])) for j in seqs] for u in ids]
        v_new = [_rows([rows(uu[u], j) - ws_qs[u][j][:c] for j in seqs]) for u in ids]
        qs = [_rows([ws_qs[u][j][c:] for j in seqs]) for u in ids]
        intra = [_dot(_bf(attn[u]), _bf(v_new[u])) for u in ids]
        for u, (p, h) in enumerate(units):
            o_ref[p, :, dv_sl(h)] = qs[u] + intra[u]
        ktv = [[_dot_tn(_bf(rows(kd[u], j)), _bf(rows(v_new[u], j))) for j in seqs] for u in ids]
        for u, (p, h) in enumerate(units):
            for j in seqs:
                s_ref[j, h] = s[u][j] * gt[p]["e_last"][j, :, h:h + 1] + ktv[u][j]


def _gdn(conv, gb, s0, t, c):
    nb = s0.shape[0]
    groups, rows, ns, par = _step_layout(nb, t, c)
    blk = lambda cols: pl.BlockSpec((par, STEP_ROWS, cols), lambda g, i: (g, i, 0))
    state_spec = pl.BlockSpec((par * ns, GDN_HEADS, GDN_DK, GDN_DV), lambda g, i: (g, 0, 0, 0))
    o, s_new = pl.pallas_call(
        functools.partial(_gdn_kernel, par=par, ns=ns, c=c),
        grid=(groups // par, rows // STEP_ROWS),
        in_specs=[blk(CONV_CH), blk(AB_W), state_spec],
        out_specs=[blk(GDN_V), state_spec],
        out_shape=[jax.ShapeDtypeStruct((groups, rows, GDN_V), F32), jax.ShapeDtypeStruct(s0.shape, F32)],
        compiler_params=_params(("arbitrary", "arbitrary")),
        name="gdn",
    )(conv.reshape(groups, rows, CONV_CH), gb.reshape(groups, rows, AB_W), s0)
    return o.reshape(nb * t, GDN_V), s_new


def _attn_out_kernel(x_ref, or_ref, og_ref, ln1_ref, wgate_ref, rn_ref, gn_ref, wr_ref, wg_ref, wo_ref, o_ref,
                     h_ref):
    x = x_ref[...]
    h_ref[...] = _bf(_rms(x, ln1_ref[...]))
    slabs = [slice(s * MXU_COLS, (s + 1) * MXU_COLS) for s in range(D_MODEL // MXU_COLS)]

    def gate(block, sl, act):
        return act(_dot(h_ref[...], wgate_ref[:, block * D_MODEL + sl.start:block * D_MODEL + sl.stop]))

    def branch(src_ref, head_dim, gain_ref, block, w_ref):
        parts = []
        for sl in slabs:
            o = src_ref[:, sl]
            heads = [o[:, k:k + head_dim] for k in range(0, MXU_COLS, head_dim)]
            normed = [oh * lax.rsqrt(jnp.mean(oh * oh, axis=-1, keepdims=True) + EPS) for oh in heads]
            parts.append(_bf(jnp.concatenate(normed, axis=-1) * gain_ref[:, sl] * gate(block, sl, _silu)))
        return _dot(jnp.concatenate(parts, axis=-1), w_ref[...])

    br_r = branch(or_ref, RET_DV, rn_ref, 0, wr_ref)
    br_g = branch(og_ref, GDN_DV, gn_ref, 1, wg_ref)
    mix = [_bf(gate(2, sl, jax.nn.sigmoid) * br_r[:, sl] + gate(3, sl, jax.nn.sigmoid) * br_g[:, sl])
           for sl in slabs]
    o_ref[...] = x + _dot(jnp.concatenate(mix, axis=-1), wo_ref[...])


def _attn_out(x2d, o_r, o_g, ln1, w_gate, ret_norm, gdn_norm, w_ret_br, w_gdn_br, w_out):
    n = x2d.shape[0]
    tm = min(512, n)
    rows = pl.BlockSpec((tm, D_MODEL), lambda i: (i, 0))
    vec = _const_spec((1, D_MODEL))
    wspec = _const_spec((D_MODEL, D_MODEL))
    return pl.pallas_call(
        _attn_out_kernel,
        grid=(n // tm,),
        in_specs=[rows, rows, rows, vec, _const_spec((D_MODEL, GATE_W)), vec, vec, wspec, wspec, wspec],
        out_specs=rows,
        out_shape=jax.ShapeDtypeStruct((n, D_MODEL), F32),
        scratch_shapes=[pltpu.VMEM((tm, D_MODEL), BF16)],
        compiler_params=_params(("arbitrary",)),
        name="attn_out",
    )(x2d, o_r, o_g, ln1, w_gate, ret_norm, gdn_norm, w_ret_br, w_gdn_br, w_out)


FF_BLOCK = 1024


def _mlp_kernel(x_ref, ln2_ref, wu_ref, wd_ref, lnf_ref, o_ref):
    x = x_ref[...]
    hb = _bf(_rms(x, ln2_ref[...]))
    acc = x
    for f in range(D_FF // FF_BLOCK):
        sl = slice(f * FF_BLOCK, (f + 1) * FF_BLOCK)
        up = jnp.maximum(_dot(hb, wu_ref[:, sl]), 0.0)
        acc = acc + _dot(_bf(up * up), wd_ref[sl, :])
    o_ref[...] = _rms(acc, lnf_ref[...])


def _mlp(x2d, ln2, w_up, w_down, ln_f):
    n = x2d.shape[0]
    tm = min(512, n)
    rows = pl.BlockSpec((tm, D_MODEL), lambda i: (i, 0))
    return pl.pallas_call(
        _mlp_kernel,
        grid=(n // tm,),
        in_specs=[rows, _const_spec((1, D_MODEL)), _const_spec((D_MODEL, D_FF)), _const_spec((D_FF, D_MODEL)),
                  _const_spec((1, D_MODEL))],
        out_specs=rows,
        out_shape=jax.ShapeDtypeStruct((n, D_MODEL), F32),
        compiler_params=_params(("arbitrary",)),
        name="mlp",
    )(x2d, ln2, w_up, w_down, ln_f)


def _group(x, pos, s_ret, s_gdn, s_conv, wts):
    (in_wts, w_gate, ret_norm, gdn_norm, w_ret_br, w_gdn_br, w_out, ln2, w_up, w_down, ln_f) = wts
    nb, t, _ = x.shape
    c = CHUNK if t % CHUNK == 0 else t
    assert STEP_ROWS % c == 0 and nb % (STEP_ROWS // c) == 0, (nb, t)
    x2d = x.reshape(nb * t, D_MODEL)
    qk, v, conv, gb, conv_new = _in_proj(x2d, t, pos, s_conv, in_wts)
    o_r, ret_new = _retention(qk, v, s_ret, t, c)
    o_g, gdn_new = _gdn(conv, gb, s_gdn, t, c)
    x1 = _attn_out(x2d, o_r, o_g, in_wts[0], w_gate, ret_norm, gdn_norm, w_ret_br, w_gdn_br, w_out)
    y = _mlp(x1, ln2, w_up, w_down, ln_f)
    return y.reshape(x.shape), ret_new, gdn_new, conv_new


def kernel(x_prompt, x_sample, state_ret, state_gdn, state_conv, ln1, w_in, conv_w, a_log, dt_bias, ret_norm,
           gdn_norm, w_ret_br, w_gdn_br, w_out, ln2, w_up, w_down, ln_f):
    depth = w_in.shape[0]
    assert depth == 1, "single-layer trunk"
    bp, tp, _ = x_prompt.shape
    ts = x_sample.shape[1]
    w = w_in[0]
    o_gr = 2 * RET_QK + RET_V
    o_qkv = o_gr + RET_V
    o_a = o_qkv + CONV_CH
    o_z = o_a + 2 * GDN_HEADS
    w_qkv = w[:, :o_gr].astype(BF16)
    w_conv = w[:, o_qkv:o_a].astype(BF16)
    w_gate = jnp.concatenate([w[:, o_gr:o_qkv], w[:, o_z:]], axis=1).astype(BF16)
    w_ab = jnp.zeros((D_MODEL, AB_W), F32)
    w_ab = w_ab.at[:, :GDN_HEADS].set(w[:, o_a:o_a + GDN_HEADS])
    w_ab = w_ab.at[:, LANES:LANES + GDN_HEADS].set(w[:, o_a + GDN_HEADS:o_z]).astype(BF16)
    vec = lambda v: v.reshape(1, -1)
    in_wts = (vec(ln1[0]), w_qkv, w_conv, w_ab, conv_w[0], a_log[0], dt_bias[0])
    wts = (in_wts, w_gate, vec(ret_norm[0]), vec(gdn_norm[0]), w_ret_br[0].astype(BF16), w_gdn_br[0].astype(BF16),
           w_out[0].astype(BF16), vec(ln2[0]), w_up[0].astype(BF16), w_down[0].astype(BF16), vec(ln_f))
    pos_p = jnp.arange(tp, dtype=F32)
    pos_s = PAST_LEN + jnp.arange(ts, dtype=F32)
    zeros = lambda *shape: jnp.zeros((bp,) + shape, F32)
    yp, ret_p, gdn_p, conv_p = _group(x_prompt, pos_p, zeros(RET_HEADS, RET_DK, RET_DV),
                                      zeros(GDN_HEADS, GDN_DK, GDN_DV), zeros(CONV_W - 1, CONV_CH), wts)
    ys, ret_s, gdn_s, conv_s = _group(x_sample, pos_s, state_ret[0], state_gdn[0], state_conv[0], wts)
    return (yp, ys, ret_p[None], gdn_p[None], conv_p[None], ret_s[None], gdn_s[None], conv_s[None])
```

```python
import functools

import jax
import jax.numpy as jnp
from jax import lax
from jax.experimental import pallas as pl
from jax.experimental.pallas import tpu as pltpu

F32 = jnp.float32
BF16 = jnp.bfloat16

D_MODEL = 1024
PAST_LEN = 16384
RET_HEADS, RET_DK, RET_DV = 4, 128, 256
RET_QK = RET_HEADS * RET_DK
RET_V = RET_HEADS * RET_DV
GDN_HEADS, GDN_DK, GDN_DV = 8, 128, 128
GDN_QK = GDN_HEADS * GDN_DK
GDN_V = GDN_HEADS * GDN_DV
CONV_W = 4
CONV_CH = 2 * GDN_QK + GDN_V
D_FF = 4 * D_MODEL
CHUNK = 64
ROPE_BASE = 10000.0
EPS = 1e-6

SUBLANES = 8
LANES = 128
MXU_COLS = 256
STEP_ROWS = 64
EPILOGUE_ROWS = 64
VMEM_LIMIT = 56 * 1024 * 1024

QKV_W = 2 * RET_QK + RET_V
GATE_W = 4 * D_MODEL
AB_W = 2 * LANES


def _params(sem):
    return pltpu.CompilerParams(dimension_semantics=sem, vmem_limit_bytes=VMEM_LIMIT)


def _const_spec(shape):
    zeros = (0,) * len(shape)
    return pl.BlockSpec(shape, lambda *_: zeros, pipeline_mode=pl.Buffered(1))


def _dot(a, b):
    return jnp.dot(a, b, preferred_element_type=F32)


def _dot_nt(a, b):
    return lax.dot_general(a, b, (((1,), (1,)), ((), ())), preferred_element_type=F32)


def _dot_tn(a, b):
    return lax.dot_general(a, b, (((0,), (0,)), ((), ())), preferred_element_type=F32)


def _bf(x):
    return x.astype(BF16)


def _rows(parts):
    return jnp.concatenate(parts, axis=0)


def _rms(x, g):
    return x * lax.rsqrt(jnp.mean(x * x, axis=-1, keepdims=True) + EPS) * g


def _silu(x):
    return x * jax.nn.sigmoid(x)


def _in_proj_kernel(x_ref, ln_ref, wqkv_ref, wconv_ref, wab_ref, cos_ref, sin_ref, cw_ref, hist_ref,
                    alog_ref, dtb_ref, qk_ref, v_ref, conv_ref, gb_ref, tail_ref, h_ref, carry_ref,
                    *, seq_is_group, tiles_per_seq):
    i = pl.program_id(0)
    tm = x_ref.shape[0]
    row_blocks = [slice(b, b + EPILOGUE_ROWS) for b in range(0, tm, EPILOGUE_ROWS)]

    @pl.when(i == 0)
    def _():
        carry_ref[...] = jnp.zeros_like(carry_ref)

    h_ref[...] = _bf(_rms(x_ref[...], ln_ref[...]))
    ab = _dot(h_ref[...], wab_ref[...])
    gb_ref[:, :LANES] = -jnp.exp(alog_ref[...]) * jax.nn.softplus(ab[:, :LANES] + dtb_ref[...])
    gb_ref[:, LANES:] = jax.nn.sigmoid(ab[:, LANES:])

    def rope_slab(s):
        sl = slice(s * MXU_COLS, (s + 1) * MXU_COLS)
        r = _dot(h_ref[...], wqkv_ref[:, sl])
        for rows in row_blocks:
            for part in range(MXU_COLS // RET_DK):
                x = r[rows, part * RET_DK:(part + 1) * RET_DK]
                y = x * cos_ref[rows, :] + pltpu.roll(x, RET_DK // 2, 1) * sin_ref[rows, :]
                start = sl.start + part * RET_DK
                if start >= RET_QK:
                    y = y * (RET_DK ** -0.5)
                qk_ref[rows, start:start + RET_DK] = _bf(y)

    def v_slab(s):
        sl = slice(s * MXU_COLS, (s + 1) * MXU_COLS)
        v_ref[:, sl] = _bf(_dot(h_ref[...], wqkv_ref[:, 2 * RET_QK + sl.start:2 * RET_QK + sl.stop]))

    sub = lax.broadcasted_iota(jnp.int32, (1, SUBLANES, MXU_COLS), 1)

    def conv_slab(s):
        sl = slice(s * MXU_COLS, (s + 1) * MXU_COLS)
        w = [cw_ref[tap:tap + 1, sl][None] for tap in range(CONV_W)]
        r = _dot(h_ref[...], wconv_ref[:, sl])
        if seq_is_group:
            tail_ref[:, sl] = r
        else:
            tail_ref[:, :, sl] = r[tm - SUBLANES:][None]
            x_last = jnp.where(i % tiles_per_seq == 0, hist_ref[:, :, sl], carry_ref[:, sl][None])
            carry_ref[:, sl] = r[tm - SUBLANES:]
            x_last1 = pltpu.roll(x_last, 1, 1)
            pair_last2 = pltpu.roll(w[1] * x_last + w[0] * x_last1, 2, 1)
        for rows in row_blocks:
            g = (rows.stop - rows.start) // SUBLANES
            r3 = r[rows].reshape(g, SUBLANES, MXU_COLS)
            cur1 = pltpu.roll(r3, 1, 1)
            if seq_is_group:
                hist3 = hist_ref[rows, sl].reshape(g, SUBLANES, MXU_COLS)
                prev1 = pltpu.roll(hist3, 1, 1)
                prev2 = pltpu.roll(w[1] * hist3 + w[0] * prev1, 2, 1)
            else:
                prev1 = jnp.concatenate([x_last1, cur1[:g - 1]], axis=0)
            x1 = jnp.where(sub < 1, prev1, cur1)
            pair = w[1] * r3 + w[0] * x1
            cur2 = pltpu.roll(pair, 2, 1)
            if not seq_is_group:
                prev2 = jnp.concatenate([pair_last2, cur2[:g - 1]], axis=0)
                x_last1, pair_last2 = cur1[g - 1:], cur2[g - 1:]
            y = (w[3] * r3 + w[2] * x1) + jnp.where(sub < 2, prev2, cur2)
            conv_ref[rows, sl] = _silu(y).reshape(g * SUBLANES, MXU_COLS)

    light = ([functools.partial(rope_slab, s) for s in range(2 * RET_QK // MXU_COLS)]
             + [functools.partial(v_slab, s) for s in range(RET_V // MXU_COLS)])
    for s in range(CONV_CH // MXU_COLS):
        conv_slab(s)
        if s < len(light):
            light[s]()


def _in_proj(x2d, t, pos, conv0, wts):
    ln1, w_qkv, w_conv, w_ab, conv_w, a_log, dt_bias = wts
    n = x2d.shape[0]
    nb = n // t
    seq_is_group = t == SUBLANES
    tm = min(256, n) if seq_is_group else min(512, t)
    assert n % tm == 0 and (seq_is_group or t % tm == 0), (n, t, tm)
    tiles_per_seq = max(t // tm, 1)
    inv = ROPE_BASE ** (-jnp.arange(0, RET_DK, 2, dtype=F32) / RET_DK)
    ang = jnp.tile(pos, max(tm // t, 1))[:, None] * inv[None, :]
    cosf = jnp.concatenate([jnp.cos(ang), jnp.cos(ang)], axis=-1)
    sinf = jnp.concatenate([-jnp.sin(ang), jnp.sin(ang)], axis=-1)
    pos_blocks = cosf.shape[0] // tm
    hist = jnp.pad(conv0, ((0, 0), (SUBLANES - (CONV_W - 1), 0), (0, 0)))
    rows = lambda cols: pl.BlockSpec((tm, cols), lambda i: (i, 0))
    if seq_is_group:
        hist = hist.reshape(n, CONV_CH)
        hist_spec = rows(CONV_CH)
        tail_spec = hist_spec
        tail_shape = jax.ShapeDtypeStruct((n, CONV_CH), F32)
    else:
        hist_spec = pl.BlockSpec((1, SUBLANES, CONV_CH), lambda i: (i // tiles_per_seq, 0, 0))
        tail_spec = pl.BlockSpec((1, SUBLANES, CONV_CH), lambda i: (i, 0, 0))
        tail_shape = jax.ShapeDtypeStruct((n // tm, SUBLANES, CONV_CH), F32)
    pad_lanes = lambda x: jnp.pad(x.reshape(1, GDN_HEADS), ((0, 0), (0, LANES - GDN_HEADS)))
    pos_spec = pl.BlockSpec((tm, RET_DK), lambda i: (i % pos_blocks, 0))
    qk, v, conv, gb, tail = pl.pallas_call(
        functools.partial(_in_proj_kernel, seq_is_group=seq_is_group, tiles_per_seq=tiles_per_seq),
        grid=(n // tm,),
        in_specs=[
            rows(D_MODEL),
            _const_spec((1, D_MODEL)),
            _const_spec((D_MODEL, QKV_W)),
            _const_spec((D_MODEL, CONV_CH)),
            _const_spec((D_MODEL, AB_W)),
            pos_spec,
            pos_spec,
            _const_spec((CONV_W, CONV_CH)),
            hist_spec,
            _const_spec((1, LANES)),
            _const_spec((1, LANES)),
        ],
        out_specs=[rows(2 * RET_QK), rows(RET_V), rows(CONV_CH), rows(AB_W), tail_spec],
        out_shape=[
            jax.ShapeDtypeStruct((n, 2 * RET_QK), BF16),
            jax.ShapeDtypeStruct((n, RET_V), BF16),
            jax.ShapeDtypeStruct((n, CONV_CH), F32),
            jax.ShapeDtypeStruct((n, AB_W), F32),
            tail_shape,
        ],
        scratch_shapes=[pltpu.VMEM((tm, D_MODEL), BF16), pltpu.VMEM((SUBLANES, CONV_CH), F32)],
        compiler_params=_params(("arbitrary",)),
        name="in_proj",
    )(x2d, ln1, w_qkv, w_conv, w_ab, cosf, sinf, conv_w, hist, pad_lanes(a_log), pad_lanes(dt_bias))
    conv_new = tail.reshape(nb, -1, SUBLANES, CONV_CH)[:, -1, SUBLANES - (CONV_W - 1):, :]
    return qk, v, conv, gb, conv_new


def _step_layout(nb, t, c):
    ns = STEP_ROWS // c
    if ns == 1:
        par = max(p for p in (8, 4, 2, 1) if nb % p == 0)
        return nb, t, ns, par
    assert t == c and nb % ns == 0, (nb, t, c)
    return nb // ns, STEP_ROWS, ns, 1


def _retention_kernel(q_ref, k_ref, v_ref, dintra_ref, dq_ref, dk_ref, dc_ref, s0_ref, o_ref, s_ref,
                      *, par, ns, c):
    @pl.when(pl.program_id(1) == 0)
    def _():
        s_ref[...] = s0_ref[...]

    units = [(p, h) for p in range(par) for h in range(RET_HEADS)]
    ids = range(len(units))
    dk_sl = lambda h: slice(h * RET_DK, (h + 1) * RET_DK)
    dv_sl = lambda h: slice(h * RET_DV, (h + 1) * RET_DV)
    qb = [q_ref[p, :, dk_sl(h)] for p, h in units]
    kb = [k_ref[p, :, dk_sl(h)] for p, h in units]
    vb = [v_ref[p, :, dv_sl(h)] for p, h in units]
    scores = [_dot_nt(qb[u], kb[u]) * dintra_ref[units[u][1]] for u in ids]
    kd = [kb[u].astype(F32) * dk_ref[units[u][1]] for u in ids]
    if ns == 1:
        s = [s_ref[p, h] for p, h in units]
        qs = [_dot(qb[u], _bf(s[u])) for u in ids]
        upd = [_dot(_bf(_rows([kd[u].T, scores[u]])), vb[u]) for u in ids]
        for u, (p, h) in enumerate(units):
            o_ref[p, :, dv_sl(h)] = upd[u][RET_DK:] + qs[u] * dq_ref[h]
            s_ref[p, h] = s[u] * dc_ref[h] + upd[u][:RET_DK]
    else:
        seqs = range(ns)
        rows = lambda x, j: x[j * c:(j + 1) * c]
        q32 = [qb[u].astype(F32) for u in ids]
        v32 = [vb[u].astype(F32) for u in ids]
        s = [[s_ref[j, h] for j in seqs] for _, h in units]
        qs = [_rows([_dot(_bf(rows(q32[u], j)), _bf(s[u][j])) for j in seqs]) for u in ids]
        intra = [_dot(_bf(scores[u]), vb[u]) for u in ids]
        for u, (p, h) in enumerate(units):
            o_ref[p, :, dv_sl(h)] = intra[u] + qs[u] * dq_ref[h]
        ktv = [[_dot_tn(_bf(rows(kd[u], j)), _bf(rows(v32[u], j))) for j in seqs] for u in ids]
        for u, (p, h) in enumerate(units):
            for j in seqs:
                s_ref[j, h] = s[u][j] * dc_ref[h] + ktv[u][j]


def _retention(qk, v, s0, t, c):
    nb = s0.shape[0]
    groups, rows, ns, par = _step_layout(nb, t, c)
    log_g = jnp.log1p(-jnp.exp2(-5.0 - jnp.arange(RET_HEADS, dtype=F32)))
    idx = jnp.arange(STEP_ROWS)
    same = (idx[:, None] // c) == (idx[None, :] // c)
    off = (idx % c).astype(F32)
    diff = off[:, None] - off[None, :]
    causal = same & (diff >= 0)
    d_intra = jnp.where(causal[None], jnp.exp(log_g[:, None, None] * jnp.where(causal, diff, 0.0)[None]), 0.0)
    d_q = jnp.broadcast_to(jnp.exp(log_g[:, None] * (off + 1.0)[None, :])[..., None], (RET_HEADS, STEP_ROWS, RET_DV))
    d_k = jnp.broadcast_to(jnp.exp(log_g[:, None] * (c - 1.0 - off)[None, :])[..., None],
                           (RET_HEADS, STEP_ROWS, RET_DK))
    d_c = jnp.broadcast_to(jnp.exp(log_g * c)[:, None, None], (RET_HEADS, 1, RET_DV))
    blk = lambda cols, col_blk: pl.BlockSpec((par, STEP_ROWS, cols), lambda g, i: (g, i, col_blk))
    state_spec = pl.BlockSpec((par * ns, RET_HEADS, RET_DK, RET_DV), lambda g, i: (g, 0, 0, 0))
    qk3 = qk.reshape(groups, rows, 2 * RET_QK)
    o, s_new = pl.pallas_call(
        functools.partial(_retention_kernel, par=par, ns=ns, c=c),
        grid=(groups // par, rows // STEP_ROWS),
        in_specs=[
            blk(RET_QK, 0),
            blk(RET_QK, 1),
            blk(RET_V, 0),
            _const_spec((RET_HEADS, STEP_ROWS, STEP_ROWS)),
            _const_spec((RET_HEADS, STEP_ROWS, RET_DV)),
            _const_spec((RET_HEADS, STEP_ROWS, RET_DK)),
            _const_spec((RET_HEADS, 1, RET_DV)),
            state_spec,
        ],
        out_specs=[blk(RET_V, 0), state_spec],
        out_shape=[jax.ShapeDtypeStruct((groups, rows, RET_V), F32), jax.ShapeDtypeStruct(s0.shape, F32)],
        compiler_params=_params(("arbitrary", "arbitrary")),
        name="retention",
    )(qk3, qk3, v.reshape(groups, rows, RET_V), d_intra, d_q, d_k, d_c, s0)
    return o.reshape(nb * t, RET_V), s_new


def _solve_correction(a, c):
    heads = range(len(a))
    m = a[0].shape[0]
    n = [-x for x in a]
    ab = [_bf(x) for x in a]
    p = [_dot(ab[h], ab[h]) for h in heads]
    span = 2
    while span < c:
        pb = [_bf(x) for x in p]
        last = 2 * span >= c
        if last:
            prod = [_dot(_bf(n[h]), pb[h]) for h in heads]
            n = [n[h] + p[h] + prod[h] for h in heads]
        else:
            prod = [_dot(_bf(_rows([n[h], p[h]])), pb[h]) for h in heads]
            n = [n[h] + p[h] + prod[h][:m] for h in heads]
            p = [prod[h][m:] for h in heads]
        span *= 2
    return n


def _gdn_kernel(qkv_ref, gb_ref, s0_ref, o_ref, s_ref, *, par, ns, c):
    @pl.when(pl.program_id(1) == 0)
    def _():
        s_ref[...] = s0_ref[...]

    m = STEP_ROWS

    def l2norm(x):
        return x * lax.rsqrt(jnp.sum(x * x, axis=-1, keepdims=True) + EPS)

    lane = lax.broadcasted_iota(jnp.int32, (m, LANES), 1)
    off = lax.broadcasted_iota(jnp.int32, (m, LANES), 0) % c
    r = lax.broadcasted_iota(jnp.int32, (m, m), 0)
    cc = lax.broadcasted_iota(jnp.int32, (m, m), 1)
    same = (r // c) == (cc // c)
    tril = same & (r >= cc)
    strict = same & (r > cc)

    def gates(p):
        cum = jnp.where(lane < GDN_HEADS, gb_ref[p, :, :LANES], 0.0)
        shift = 1
        while shift < c:
            cum = cum + jnp.where(off >= shift, pltpu.roll(cum, shift, 0), 0.0)
            shift *= 2
        last = cum.reshape(ns, c, LANES)[:, c - 1:c, :]
        e_rest = jnp.exp((last - cum.reshape(ns, c, LANES)).reshape(m, LANES))
        return dict(cum=cum, cum_t=cum.T, e_cum=jnp.exp(cum), e_rest=e_rest, e_last=jnp.exp(last),
                    beta=gb_ref[p, :, LANES:])

    gt = [gates(p) for p in range(par)]
    units = [(p, h) for p in range(par) for h in range(GDN_HEADS)]
    ids = range(len(units))
    col = lambda x, h: x[:, h:h + 1]
    q = [l2norm(qkv_ref[p, :, h * GDN_DK:(h + 1) * GDN_DK]) * (GDN_DK ** -0.5) for p, h in units]
    k = [l2norm(qkv_ref[p, :, GDN_QK + h * GDN_DK:GDN_QK + (h + 1) * GDN_DK]) for p, h in units]
    v = [qkv_ref[p, :, 2 * GDN_QK + h * GDN_DV:2 * GDN_QK + (h + 1) * GDN_DV] for p, h in units]
    beta = [col(gt[p]["beta"], h) for p, h in units]
    eg = [col(gt[p]["e_cum"], h) for p, h in units]
    decay = [jnp.exp(jnp.where(tril, col(gt[p]["cum"], h) - gt[p]["cum_t"][h:h + 1, :], -jnp.inf)) for p, h in units]
    kb = [k[u] * beta[u] for u in ids]
    kbf = [_bf(k[u]) for u in ids]
    raw = [_dot_nt(_bf(_rows([kb[u], q[u]])), kbf[u]) for u in ids]
    a = [jnp.where(strict, raw[u][:m] * decay[u], 0.0) for u in ids]
    attn = [raw[u][m:] * decay[u] for u in ids]
    n = _solve_correction(a, c)
    rhs = [jnp.concatenate([v[u] * beta[u], kb[u] * eg[u]], axis=-1) for u in ids]
    sol = [rhs[u] + _dot(_bf(n[u]), _bf(rhs[u])) for u in ids]
    uu = [sol[u][:, :GDN_DV] for u in ids]
    w = [sol[u][:, GDN_DV:] for u in ids]
    qe = [q[u] * eg[u] for u in ids]
    kd = [k[u] * col(gt[p]["e_rest"], h) for u, (p, h) in enumerate(units)]
    dv_sl = lambda h: slice(h * GDN_DV, (h + 1) * GDN_DV)
    if ns == 1:
        s = [s_ref[p, h] for p, h in units]
        ws_qs = [_dot(_bf(_rows([w[u], qe[u]])), _bf(s[u])) for u in ids]
        v_new = [uu[u] - ws_qs[u][:m] for u in ids]
        upd = [_dot(_bf(_rows([kd[u].T, attn[u]])), _bf(v_new[u])) for u in ids]
        for u, (p, h) in enumerate(units):
            o_ref[p, :, dv_sl(h)] = ws_qs[u][m:] + upd[u][GDN_DK:]
            s_ref[p, h] = s[u] * gt[p]["e_last"][0, :, h:h + 1] + upd[u][:GDN_DK]
    else:
        seqs = range(ns)
        rows = lambda x, j: x[j * c:(j + 1) * c]
        s = [[s_ref[j, h] for j in seqs] for _, h in units]
        ws_qs = [[_dot(_bf(_rows([rows(w[u], j), rows(qe[u], j)])), _bf(s[u][j])) for j in seqs] for u in ids]
        v_new = [_rows([rows(uu[u], j) - ws_qs[u][j][:c] for j in seqs]) for u in ids]
        qs = [_rows([ws_qs[u][j][c:] for j in seqs]) for u in ids]
        intra = [_dot(_bf(attn[u]), _bf(v_new[u])) for u in ids]
        for u, (p, h) in enumerate(units):
            o_ref[p, :, dv_sl(h)] = qs[u] + intra[u]
        ktv = [[_dot_tn(_bf(rows(kd[u], j)), _bf(rows(v_new[u], j))) for j in seqs] for u in ids]
        for u, (p, h) in enumerate(units):
            for j in seqs:
                s_ref[j, h] = s[u][j] * gt[p]["e_last"][j, :, h:h + 1] + ktv[u][j]


def _gdn(conv, gb, s0, t, c):
    nb = s0.shape[0]
    groups, rows, ns, par = _step_layout(nb, t, c)
    blk = lambda cols: pl.BlockSpec((par, STEP_ROWS, cols), lambda g, i: (g, i, 0))
    state_spec = pl.BlockSpec((par * ns, GDN_HEADS, GDN_DK, GDN_DV), lambda g, i: (g, 0, 0, 0))
    o, s_new = pl.pallas_call(
        functools.partial(_gdn_kernel, par=par, ns=ns, c=c),
        grid=(groups // par, rows // STEP_ROWS),
        in_specs=[blk(CONV_CH), blk(AB_W), state_spec],
        out_specs=[blk(GDN_V), state_spec],
        out_shape=[jax.ShapeDtypeStruct((groups, rows, GDN_V), F32), jax.ShapeDtypeStruct(s0.shape, F32)],
        compiler_params=_params(("arbitrary", "arbitrary")),
        name="gdn",
    )(conv.reshape(groups, rows, CONV_CH), gb.reshape(groups, rows, AB_W), s0)
    return o.reshape(nb * t, GDN_V), s_new


def _attn_out_kernel(x_ref, or_ref, og_ref, ln1_ref, wgate_ref, rn_ref, gn_ref, wr_ref, wg_ref, wo_ref, o_ref,
                     h_ref, lhs_ref, br_ref):
    tm = x_ref.shape[0]
    h_ref[...] = _bf(_rms(x_ref[...], ln1_ref[...]))
    slabs = [slice(s * MXU_COLS, (s + 1) * MXU_COLS) for s in range(D_MODEL // MXU_COLS)]
    row_blocks = [slice(b, b + EPILOGUE_ROWS) for b in range(0, tm, EPILOGUE_ROWS)]

    def gate(block, sl):
        return _dot(h_ref[...], wgate_ref[:, block * D_MODEL + sl.start:block * D_MODEL + sl.stop])

    def branch(src_ref, head_dim, gain_ref, block, w_ref):
        for sl in slabs:
            g = gate(block, sl)
            for rows in row_blocks:
                o = src_ref[rows, sl]
                heads = [o[:, k:k + head_dim] for k in range(0, MXU_COLS, head_dim)]
                normed = [oh * lax.rsqrt(jnp.mean(oh * oh, axis=-1, keepdims=True) + EPS) for oh in heads]
                lhs_ref[rows, sl] = _bf(jnp.concatenate(normed, axis=-1) * gain_ref[:, sl] * _silu(g[rows]))
        return _dot(lhs_ref[...], w_ref[...])

    br_ref[0] = branch(or_ref, RET_DV, rn_ref, 0, wr_ref)
    br_ref[1] = branch(og_ref, GDN_DV, gn_ref, 1, wg_ref)
    for sl in slabs:
        g_r = gate(2, sl)
        g_g = gate(3, sl)
        for rows in row_blocks:
            lhs_ref[rows, sl] = _bf(jax.nn.sigmoid(g_r[rows]) * br_ref[0, rows, sl]
                                    + jax.nn.sigmoid(g_g[rows]) * br_ref[1, rows, sl])
    o_ref[...] = x_ref[...] + _dot(lhs_ref[...], wo_ref[...])


def _attn_out(x2d, o_r, o_g, ln1, w_gate, ret_norm, gdn_norm, w_ret_br, w_gdn_br, w_out):
    n = x2d.shape[0]
    tm = min(512, n)
    rows = pl.BlockSpec((tm, D_MODEL), lambda i: (i, 0))
    vec = _const_spec((1, D_MODEL))
    wspec = _const_spec((D_MODEL, D_MODEL))
    return pl.pallas_call(
        _attn_out_kernel,
        grid=(n // tm,),
        in_specs=[rows, rows, rows, vec, _const_spec((D_MODEL, GATE_W)), vec, vec, wspec, wspec, wspec],
        out_specs=rows,
        out_shape=jax.ShapeDtypeStruct((n, D_MODEL), F32),
        scratch_shapes=[pltpu.VMEM((tm, D_MODEL), BF16), pltpu.VMEM((tm, D_MODEL), BF16),
                        pltpu.VMEM((2, tm, D_MODEL), F32)],
        compiler_params=_params(("arbitrary",)),
        name="attn_out",
    )(x2d, o_r, o_g, ln1, w_gate, ret_norm, gdn_norm, w_ret_br, w_gdn_br, w_out)


FF_BLOCK = 1024


def _mlp_kernel(x_ref, ln2_ref, wu_ref, wd_ref, lnf_ref, o_ref):
    x = x_ref[...]
    hb = _bf(_rms(x, ln2_ref[...]))
    acc = x
    for f in range(D_FF // FF_BLOCK):
        sl = slice(f * FF_BLOCK, (f + 1) * FF_BLOCK)
        up = jnp.maximum(_dot(hb, wu_ref[:, sl]), 0.0)
        acc = acc + _dot(_bf(up * up), wd_ref[sl, :])
    o_ref[...] = _rms(acc, lnf_ref[...])


def _mlp(x2d, ln2, w_up, w_down, ln_f):
    n = x2d.shape[0]
    tm = min(1024, n)
    rows = pl.BlockSpec((tm, D_MODEL), lambda i: (i, 0))
    return pl.pallas_call(
        _mlp_kernel,
        grid=(n // tm,),
        in_specs=[rows, _const_spec((1, D_MODEL)), _const_spec((D_MODEL, D_FF)), _const_spec((D_FF, D_MODEL)),
                  _const_spec((1, D_MODEL))],
        out_specs=rows,
        out_shape=jax.ShapeDtypeStruct((n, D_MODEL), F32),
        compiler_params=_params(("arbitrary",)),
        name="mlp",
    )(x2d, ln2, w_up, w_down, ln_f)


def _group(x, pos, s_ret, s_gdn, s_conv, wts):
    (in_wts, w_gate, ret_norm, gdn_norm, w_ret_br, w_gdn_br, w_out, ln2, w_up, w_down, ln_f) = wts
    nb, t, _ = x.shape
    c = CHUNK if t % CHUNK == 0 else t
    assert STEP_ROWS % c == 0 and nb % (STEP_ROWS // c) == 0, (nb, t)
    x2d = x.reshape(nb * t, D_MODEL)
    qk, v, conv, gb, conv_new = _in_proj(x2d, t, pos, s_conv, in_wts)
    o_r, ret_new = _retention(qk, v, s_ret, t, c)
    o_g, gdn_new = _gdn(conv, gb, s_gdn, t, c)
    x1 = _attn_out(x2d, o_r, o_g, in_wts[0], w_gate, ret_norm, gdn_norm, w_ret_br, w_gdn_br, w_out)
    y = _mlp(x1, ln2, w_up, w_down, ln_f)
    return y.reshape(x.shape), ret_new, gdn_new, conv_new


def kernel(x_prompt, x_sample, state_ret, state_gdn, state_conv, ln1, w_in, conv_w, a_log, dt_bias, ret_norm,
           gdn_norm, w_ret_br, w_gdn_br, w_out, ln2, w_up, w_down, ln_f):
    depth = w_in.shape[0]
    assert depth == 1, "single-layer trunk"
    bp, tp, _ = x_prompt.shape
    ts = x_sample.shape[1]
    w = w_in[0]
    o_gr = 2 * RET_QK + RET_V
    o_qkv = o_gr + RET_V
    o_a = o_qkv + CONV_CH
    o_z = o_a + 2 * GDN_HEADS
    w_qkv = w[:, :o_gr].astype(BF16)
    w_conv = w[:, o_qkv:o_a].astype(BF16)
    w_gate = jnp.concatenate([w[:, o_gr:o_qkv], w[:, o_z:]], axis=1).astype(BF16)
    w_ab = jnp.zeros((D_MODEL, AB_W), F32)
    w_ab = w_ab.at[:, :GDN_HEADS].set(w[:, o_a:o_a + GDN_HEADS])
    w_ab = w_ab.at[:, LANES:LANES + GDN_HEADS].set(w[:, o_a + GDN_HEADS:o_z]).astype(BF16)
    vec = lambda v: v.reshape(1, -1)
    in_wts = (vec(ln1[0]), w_qkv, w_conv, w_ab, conv_w[0], a_log[0], dt_bias[0])
    wts = (in_wts, w_gate, vec(ret_norm[0]), vec(gdn_norm[0]), w_ret_br[0].astype(BF16), w_gdn_br[0].astype(BF16),
           w_out[0].astype(BF16), vec(ln2[0]), w_up[0].astype(BF16), w_down[0].astype(BF16), vec(ln_f))
    pos_p = jnp.arange(tp, dtype=F32)
    pos_s = PAST_LEN + jnp.arange(ts, dtype=F32)
    zeros = lambda *shape: jnp.zeros((bp,) + shape, F32)
    yp, ret_p, gdn_p, conv_p = _group(x_prompt, pos_p, zeros(RET_HEADS, RET_DK, RET_DV),
                                      zeros(GDN_HEADS, GDN_DK, GDN_DV), zeros(CONV_W - 1, CONV_CH), wts)
    ys, ret_s, gdn_s, conv_s = _group(x_sample, pos_s, state_ret[0], state_gdn[0], state_conv[0], wts)
    return (yp, ys, ret_p[None], gdn_p[None], conv_p[None], ret_s[None], gdn_s[None], conv_s[None])
```

```python
import functools

import jax
import jax.numpy as jnp
from jax import lax
from jax.experimental import pallas as pl
from jax.experimental.pallas import tpu as pltpu

F32 = jnp.float32
BF16 = jnp.bfloat16

D_MODEL = 1024
PAST_LEN = 16384
RET_HEADS, RET_DK, RET_DV = 4, 128, 256
RET_QK = RET_HEADS * RET_DK
RET_V = RET_HEADS * RET_DV
GDN_HEADS, GDN_DK, GDN_DV = 8, 128, 128
GDN_QK = GDN_HEADS * GDN_DK
GDN_V = GDN_HEADS * GDN_DV
CONV_W = 4
CONV_CH = 2 * GDN_QK + GDN_V
D_FF = 4 * D_MODEL
CHUNK = 64
ROPE_BASE = 10000.0
EPS = 1e-6

SUBLANES = 8
LANES = 128
MXU_COLS = 256
STEP_ROWS = 64
MIX_GROUP = 2
VMEM_LIMIT = 56 * 1024 * 1024
MIX_VMEM_LIMIT = 61 * 1024 * 1024

QKV_W = 2 * RET_QK + RET_V
GATE_W = 4 * D_MODEL
AB_W = 2 * LANES


def _params(sem, vmem_limit=VMEM_LIMIT):
    return pltpu.CompilerParams(dimension_semantics=sem, vmem_limit_bytes=vmem_limit)


def _const_spec(shape):
    zeros = (0,) * len(shape)
    return pl.BlockSpec(shape, lambda *_: zeros, pipeline_mode=pl.Buffered(1))


def _dot(a, b):
    return jnp.dot(a, b, preferred_element_type=F32)


def _dot_nt(a, b):
    return lax.dot_general(a, b, (((1,), (1,)), ((), ())), preferred_element_type=F32)


def _dot_tn(a, b):
    return lax.dot_general(a, b, (((0,), (0,)), ((), ())), preferred_element_type=F32)


def _bf(x):
    return x.astype(BF16)


def _rows(parts):
    return jnp.concatenate(parts, axis=0)


def _rms(x, g):
    return x * lax.rsqrt(jnp.mean(x * x, axis=-1, keepdims=True) + EPS) * g


def _silu(x):
    return x * jax.nn.sigmoid(x)


def _unit(ref, p, cols=slice(None)):
    if len(ref.shape) == 3:
        return (p, slice(None), cols)
    return (slice(p * STEP_ROWS, (p + 1) * STEP_ROWS), cols)


def _run_interleaved(main, *side_lists):
    slots = [[] for _ in main]
    for side in side_lists:
        for k, stage in enumerate(side):
            slots[k * len(main) // len(side)].append(stage)
    for stage, extra in zip(main, slots):
        stage()
        for side_stage in extra:
            side_stage()


def _proj_stages(x, ln_ref, wqkv_ref, wconv_ref, wab_ref, cos_ref, sin_ref, cw_ref, alog_ref, dtb_ref, hist_ref,
                 qk_ref, v_ref, conv_ref, gb_ref, tail_ref, h_ref, *, seq_is_group):
    tm = x.shape[0]
    blocks = range(tm // STEP_ROWS)
    rows_of = lambda p: slice(p * STEP_ROWS, (p + 1) * STEP_ROWS)
    groups = STEP_ROWS // SUBLANES
    sub = lax.broadcasted_iota(jnp.int32, (1, SUBLANES, MXU_COLS), 1)

    def prologue():
        h_ref[...] = _bf(_rms(x, ln_ref[...]))
        ab = _dot(h_ref[...], wab_ref[...])
        g = -jnp.exp(alog_ref[...]) * jax.nn.softplus(ab[:, :LANES] + dtb_ref[...])
        beta = jax.nn.sigmoid(ab[:, LANES:])
        for p in blocks:
            gb_ref[_unit(gb_ref, p, slice(0, LANES))] = g[rows_of(p)]
            gb_ref[_unit(gb_ref, p, slice(LANES, AB_W))] = beta[rows_of(p)]

    def rope_slab(s):
        sl = slice(s * MXU_COLS, (s + 1) * MXU_COLS)
        r = _dot(h_ref[...], wqkv_ref[:, sl])
        for p in blocks:
            for part in range(MXU_COLS // RET_DK):
                xh = r[rows_of(p), part * RET_DK:(part + 1) * RET_DK]
                y = xh * cos_ref[...] + pltpu.roll(xh, RET_DK // 2, 1) * sin_ref[...]
                start = sl.start + part * RET_DK
                if start >= RET_QK:
                    y = y * (RET_DK ** -0.5)
                qk_ref[_unit(qk_ref, p, slice(start, start + RET_DK))] = _bf(y)

    def v_slab(s):
        sl = slice(s * MXU_COLS, (s + 1) * MXU_COLS)
        r = _bf(_dot(h_ref[...], wqkv_ref[:, 2 * RET_QK + sl.start:2 * RET_QK + sl.stop]))
        for p in blocks:
            v_ref[_unit(v_ref, p, sl)] = r[rows_of(p)]

    def conv_slab(s):
        sl = slice(s * MXU_COLS, (s + 1) * MXU_COLS)
        w = [cw_ref[tap:tap + 1, sl][None] for tap in range(CONV_W)]
        r = _dot(h_ref[...], wconv_ref[:, sl])
        for p in blocks:
            r3 = r[rows_of(p)].reshape(groups, SUBLANES, MXU_COLS)
            cur1 = pltpu.roll(r3, 1, 1)
            if seq_is_group:
                tail_ref[rows_of(p), sl] = r[rows_of(p)]
                hist3 = hist_ref[rows_of(p), sl].reshape(groups, SUBLANES, MXU_COLS)
                prev1 = pltpu.roll(hist3, 1, 1)
                prev2 = pltpu.roll(w[1] * hist3 + w[0] * prev1, 2, 1)
            else:
                x_last = hist_ref[p, :, sl][None]
                x_last1 = pltpu.roll(x_last, 1, 1)
                prev1 = jnp.concatenate([x_last1, cur1[:groups - 1]], axis=0)
                hist_ref[p, :, sl] = r3[groups - 1]
                tail_ref[p, :, sl] = r3[groups - 1]
            x1 = jnp.where(sub < 1, prev1, cur1)
            pair = w[1] * r3 + w[0] * x1
            cur2 = pltpu.roll(pair, 2, 1)
            if not seq_is_group:
                pair_last2 = pltpu.roll(w[1] * x_last + w[0] * x_last1, 2, 1)
                prev2 = jnp.concatenate([pair_last2, cur2[:groups - 1]], axis=0)
            y = (w[3] * r3 + w[2] * x1) + jnp.where(sub < 2, prev2, cur2)
            conv_ref[_unit(conv_ref, p, sl)] = _silu(y).reshape(STEP_ROWS, MXU_COLS)

    light = ([functools.partial(rope_slab, s) for s in range(2 * RET_QK // MXU_COLS)]
             + [functools.partial(v_slab, s) for s in range(RET_V // MXU_COLS)])
    stages = [prologue]
    for s in range(CONV_CH // MXU_COLS):
        stages.append(functools.partial(conv_slab, s))
        if s < len(light):
            stages.append(light[s])
    return stages


def _rope_tables(pos):
    inv = ROPE_BASE ** (-jnp.arange(0, RET_DK, 2, dtype=F32) / RET_DK)
    ang = pos[:, None] * inv[None, :]
    return (jnp.concatenate([jnp.cos(ang), jnp.cos(ang)], axis=-1),
            jnp.concatenate([-jnp.sin(ang), jnp.sin(ang)], axis=-1))


def _pad_lanes(x):
    return jnp.pad(x.reshape(1, GDN_HEADS), ((0, 0), (0, LANES - GDN_HEADS)))


def _in_proj_kernel(x_ref, ln_ref, wqkv_ref, wconv_ref, wab_ref, cos_ref, sin_ref, cw_ref, alog_ref, dtb_ref,
                    hist_ref, qk_ref, v_ref, conv_ref, gb_ref, tail_ref, h_ref):
    for stage in _proj_stages(x_ref[...], ln_ref, wqkv_ref, wconv_ref, wab_ref, cos_ref, sin_ref, cw_ref,
                              alog_ref, dtb_ref, hist_ref, qk_ref, v_ref, conv_ref, gb_ref, tail_ref, h_ref,
                              seq_is_group=True):
        stage()


def _in_proj(x2d, pos, conv0, wts):
    ln1, w_qkv, w_conv, w_ab, conv_w, a_log, dt_bias = wts
    n = x2d.shape[0]
    nb = n // SUBLANES
    tm = min(256, n)
    assert n % tm == 0 and tm % STEP_ROWS == 0, (n, tm)
    cosf, sinf = _rope_tables(jnp.tile(pos, STEP_ROWS // SUBLANES))
    hist = jnp.pad(conv0, ((0, 0), (SUBLANES - (CONV_W - 1), 0), (0, 0))).reshape(n, CONV_CH)
    rows = lambda cols: pl.BlockSpec((tm, cols), lambda i: (i, 0))
    qk, v, conv, gb, tail = pl.pallas_call(
        _in_proj_kernel,
        grid=(n // tm,),
        in_specs=[
            rows(D_MODEL),
            _const_spec((1, D_MODEL)),
            _const_spec((D_MODEL, QKV_W)),
            _const_spec((D_MODEL, CONV_CH)),
            _const_spec((D_MODEL, AB_W)),
            _const_spec((STEP_ROWS, RET_DK)),
            _const_spec((STEP_ROWS, RET_DK)),
            _const_spec((CONV_W, CONV_CH)),
            _const_spec((1, LANES)),
            _const_spec((1, LANES)),
            rows(CONV_CH),
        ],
        out_specs=[rows(2 * RET_QK), rows(RET_V), rows(CONV_CH), rows(AB_W), rows(CONV_CH)],
        out_shape=[
            jax.ShapeDtypeStruct((n, 2 * RET_QK), BF16),
            jax.ShapeDtypeStruct((n, RET_V), BF16),
            jax.ShapeDtypeStruct((n, CONV_CH), F32),
            jax.ShapeDtypeStruct((n, AB_W), F32),
            jax.ShapeDtypeStruct((n, CONV_CH), F32),
        ],
        scratch_shapes=[pltpu.VMEM((tm, D_MODEL), BF16)],
        compiler_params=_params(("arbitrary",)),
        name="in_proj",
    )(x2d, ln1, w_qkv, w_conv, w_ab, cosf, sinf, conv_w, _pad_lanes(a_log), _pad_lanes(dt_bias), hist)
    conv_new = tail.reshape(nb, SUBLANES, CONV_CH)[:, SUBLANES - (CONV_W - 1):, :]
    return qk, v, conv, gb, conv_new


def _retention_consts(c):
    log_g = jnp.log1p(-jnp.exp2(-5.0 - jnp.arange(RET_HEADS, dtype=F32)))
    idx = jnp.arange(STEP_ROWS)
    same = (idx[:, None] // c) == (idx[None, :] // c)
    off = (idx % c).astype(F32)
    diff = off[:, None] - off[None, :]
    causal = same & (diff >= 0)
    d_intra = jnp.where(causal[None], jnp.exp(log_g[:, None, None] * jnp.where(causal, diff, 0.0)[None]), 0.0)
    d_q = jnp.broadcast_to(jnp.exp(log_g[:, None] * (off + 1.0)[None, :])[..., None], (RET_HEADS, STEP_ROWS, RET_DV))
    d_k = jnp.broadcast_to(jnp.exp(log_g[:, None] * (c - 1.0 - off)[None, :])[..., None],
                           (RET_HEADS, STEP_ROWS, RET_DK))
    d_c = jnp.broadcast_to(jnp.exp(log_g * c)[:, None, None], (RET_HEADS, 1, RET_DV))
    return d_intra, d_q, d_k, d_c


_RET_CONST_SPECS = lambda: [_const_spec((RET_HEADS, STEP_ROWS, STEP_ROWS)), _const_spec((RET_HEADS, STEP_ROWS, RET_DV)),
                            _const_spec((RET_HEADS, STEP_ROWS, RET_DK)), _const_spec((RET_HEADS, 1, RET_DV))]


def _retention_stages(qk_ref, v_ref, dintra_ref, dq_ref, dk_ref, dc_ref, o_ref, s_ref, *, groups, ns, c):
    units = [(p, h) for p in groups for h in range(RET_HEADS)]
    ids = range(len(units))
    seqs = range(ns)
    rows = lambda x, j: x[j * c:(j + 1) * c]
    dk_sl = lambda h: slice(h * RET_DK, (h + 1) * RET_DK)
    dv_sl = lambda h: slice(h * RET_DV, (h + 1) * RET_DV)
    st = {}

    def scores():
        st["qb"] = [qk_ref[p, :, dk_sl(h)] for p, h in units]
        kb = [qk_ref[p, :, RET_QK + h * RET_DK:RET_QK + (h + 1) * RET_DK] for p, h in units]
        st["vb"] = [v_ref[p, :, dv_sl(h)] for p, h in units]
        st["scores"] = [_dot_nt(st["qb"][u], kb[u]) * dintra_ref[units[u][1]] for u in ids]
        st["kd"] = [kb[u].astype(F32) * dk_ref[units[u][1]] for u in ids]

    def state_read():
        qb = st["qb"]
        if ns == 1:
            st["s"] = [s_ref[p, h] for p, h in units]
            st["qs"] = [_dot(qb[u], _bf(st["s"][u])) for u in ids]
        else:
            q32 = [qb[u].astype(F32) for u in ids]
            st["s"] = [[s_ref[j, h] for j in seqs] for _, h in units]
            st["qs"] = [_rows([_dot(_bf(rows(q32[u], j)), _bf(st["s"][u][j])) for j in seqs]) for u in ids]

    def state_write():
        s, qs, kd, vb, sc = st["s"], st["qs"], st["kd"], st["vb"], st["scores"]
        if ns == 1:
            upd = [_dot(_bf(_rows([kd[u].T, sc[u]])), vb[u]) for u in ids]
            for u, (p, h) in enumerate(units):
                o_ref[p, :, dv_sl(h)] = upd[u][RET_DK:] + qs[u] * dq_ref[h]
                s_ref[p, h] = s[u] * dc_ref[h] + upd[u][:RET_DK]
        else:
            v32 = [vb[u].astype(F32) for u in ids]
            intra = [_dot(_bf(sc[u]), vb[u]) for u in ids]
            for u, (p, h) in enumerate(units):
                o_ref[p, :, dv_sl(h)] = intra[u] + qs[u] * dq_ref[h]
            ktv = [[_dot_tn(_bf(rows(kd[u], j)), _bf(rows(v32[u], j))) for j in seqs] for u in ids]
            for u, (p, h) in enumerate(units):
                for j in seqs:
                    s_ref[j, h] = s[u][j] * dc_ref[h] + ktv[u][j]

    return [scores, state_read, state_write]


def _gdn_stages(qkv_ref, gb_ref, o_ref, s_ref, *, groups, ns, c):
    m = STEP_ROWS
    units = [(p, h) for p in groups for h in range(GDN_HEADS)]
    ids = range(len(units))
    seqs = range(ns)
    rows = lambda x, j: x[j * c:(j + 1) * c]
    col = lambda x, h: x[:, h:h + 1]
    dv_sl = lambda h: slice(h * GDN_DV, (h + 1) * GDN_DV)
    st = {}

    def l2norm(x):
        return x * lax.rsqrt(jnp.sum(x * x, axis=-1, keepdims=True) + EPS)

    def gates(p, lane, off):
        cum = jnp.where(lane < GDN_HEADS, gb_ref[p, :, :LANES], 0.0)
        shift = 1
        while shift < c:
            cum = cum + jnp.where(off >= shift, pltpu.roll(cum, shift, 0), 0.0)
            shift *= 2
        last = cum.reshape(ns, c, LANES)[:, c - 1:c, :]
        e_rest = jnp.exp((last - cum.reshape(ns, c, LANES)).reshape(m, LANES))
        return dict(cum=cum, cum_t=cum.T, e_cum=jnp.exp(cum), e_rest=e_rest, e_last=jnp.exp(last),
                    beta=gb_ref[p, :, LANES:])

    def prep():
        lane = lax.broadcasted_iota(jnp.int32, (m, LANES), 1)
        off = lax.broadcasted_iota(jnp.int32, (m, LANES), 0) % c
        r = lax.broadcasted_iota(jnp.int32, (m, m), 0)
        cc = lax.broadcasted_iota(jnp.int32, (m, m), 1)
        same = (r // c) == (cc // c)
        tril = same & (r >= cc)
        st["strict"] = same & (r > cc)
        gt = {p: gates(p, lane, off) for p in groups}
        q = [l2norm(qkv_ref[p, :, h * GDN_DK:(h + 1) * GDN_DK]) * (GDN_DK ** -0.5) for p, h in units]
        k = [l2norm(qkv_ref[p, :, GDN_QK + h * GDN_DK:GDN_QK + (h + 1) * GDN_DK]) for p, h in units]
        st["v"] = [qkv_ref[p, :, 2 * GDN_QK + h * GDN_DV:2 * GDN_QK + (h + 1) * GDN_DV] for p, h in units]
        st["beta"] = [col(gt[p]["beta"], h) for p, h in units]
        st["eg"] = [col(gt[p]["e_cum"], h) for p, h in units]
        st["decay"] = [jnp.exp(jnp.where(tril, col(gt[p]["cum"], h) - gt[p]["cum_t"][h:h + 1, :], -jnp.inf))
                       for p, h in units]
        st["kb"] = [k[u] * st["beta"][u] for u in ids]
        st["raw"] = [_dot_nt(_bf(_rows([st["kb"][u], q[u]])), _bf(k[u])) for u in ids]
        st["q"], st["k"], st["gt"] = q, k, gt

    def first_power():
        raw, decay = st["raw"], st["decay"]
        a = [jnp.where(st["strict"], raw[u][:m] * decay[u], 0.0) for u in ids]
        st["attn"] = [raw[u][m:] * decay[u] for u in ids]
        st["n"] = [-x for x in a]
        ab = [_bf(x) for x in a]
        st["p"] = [_dot(ab[u], ab[u]) for u in ids]

    def level(last):
        n, p = st["n"], st["p"]
        pb = [_bf(x) for x in p]
        if last:
            prod = [_dot(_bf(n[u]), pb[u]) for u in ids]
            st["n"] = [n[u] + p[u] + prod[u] for u in ids]
        else:
            prod = [_dot(_bf(_rows([n[u], p[u]])), pb[u]) for u in ids]
            st["n"] = [n[u] + p[u] + prod[u][:m] for u in ids]
            st["p"] = [prod[u][m:] for u in ids]

    def solve():
        v, beta, kb, eg, n = st["v"], st["beta"], st["kb"], st["eg"], st["n"]
        rhs = [jnp.concatenate([v[u] * beta[u], kb[u] * eg[u]], axis=-1) for u in ids]
        sol = [rhs[u] + _dot(_bf(n[u]), _bf(rhs[u])) for u in ids]
        st["uu"] = [sol[u][:, :GDN_DV] for u in ids]
        st["w"] = [sol[u][:, GDN_DV:] for u in ids]
        st["qe"] = [st["q"][u] * eg[u] for u in ids]
        st["kd"] = [st["k"][u] * col(st["gt"][p]["e_rest"], h) for u, (p, h) in enumerate(units)]

    def state_read():
        w, qe, uu = st["w"], st["qe"], st["uu"]
        if ns == 1:
            st["s"] = [s_ref[p, h] for p, h in units]
            st["ws_qs"] = [_dot(_bf(_rows([w[u], qe[u]])), _bf(st["s"][u])) for u in ids]
            st["v_new"] = [uu[u] - st["ws_qs"][u][:m] for u in ids]
        else:
            st["s"] = [[s_ref[j, h] for j in seqs] for _, h in units]
            ws_qs = [[_dot(_bf(_rows([rows(w[u], j), rows(qe[u], j)])), _bf(st["s"][u][j])) for j in seqs]
                     for u in ids]
            st["v_new"] = [_rows([rows(uu[u], j) - ws_qs[u][j][:c] for j in seqs]) for u in ids]
            st["qs"] = [_rows([ws_qs[u][j][c:] for j in seqs]) for u in ids]

    def state_write():
        s, kd, attn, v_new, gt = st["s"], st["kd"], st["attn"], st["v_new"], st["gt"]
        if ns == 1:
            upd = [_dot(_bf(_rows([kd[u].T, attn[u]])), _bf(v_new[u])) for u in ids]
            for u, (p, h) in enumerate(units):
                o_ref[p, :, dv_sl(h)] = st["ws_qs"][u][m:] + upd[u][GDN_DK:]
                s_ref[p, h] = s[u] * gt[p]["e_last"][0, :, h:h + 1] + upd[u][:GDN_DK]
        else:
            intra = [_dot(_bf(attn[u]), _bf(v_new[u])) for u in ids]
            for u, (p, h) in enumerate(units):
                o_ref[p, :, dv_sl(h)] = st["qs"][u] + intra[u]
            ktv = [[_dot_tn(_bf(rows(kd[u], j)), _bf(rows(v_new[u], j))) for j in seqs] for u in ids]
            for u, (p, h) in enumerate(units):
                for j in seqs:
                    s_ref[j, h] = s[u][j] * gt[p]["e_last"][j, :, h:h + 1] + ktv[u][j]

    levels = []
    span = 2
    while span < c:
        levels.append(functools.partial(level, 2 * span >= c))
        span *= 2
    return [prep, first_power, *levels, solve, state_read, state_write]


def _retention_kernel(qk_ref, v_ref, dintra_ref, dq_ref, dk_ref, dc_ref, s0_ref, o_ref, s_ref, *, ns, c):
    s_ref[...] = s0_ref[...]
    for stage in _retention_stages(qk_ref, v_ref, dintra_ref, dq_ref, dk_ref, dc_ref, o_ref, s_ref,
                                   groups=(0,), ns=ns, c=c):
        stage()


def _gdn_kernel(qkv_ref, gb_ref, s0_ref, o_ref, s_ref, *, ns, c):
    s_ref[...] = s0_ref[...]
    for stage in _gdn_stages(qkv_ref, gb_ref, o_ref, s_ref, groups=(0,), ns=ns, c=c):
        stage()


def _sample_recurrences(qk, v, conv, gb, s_ret, s_gdn, c):
    nb = s_ret.shape[0]
    ns = STEP_ROWS // c
    assert nb % ns == 0 and qk.shape[0] == nb * c, (nb, c)
    steps = nb // ns
    blk = lambda cols: pl.BlockSpec((1, STEP_ROWS, cols), lambda g: (g, 0, 0))
    view = lambda x: x.reshape(steps, STEP_ROWS, x.shape[-1])
    ret_state = pl.BlockSpec((ns, RET_HEADS, RET_DK, RET_DV), lambda g: (g, 0, 0, 0))
    o_r, ret_new = pl.pallas_call(
        functools.partial(_retention_kernel, ns=ns, c=c),
        grid=(steps,),
        in_specs=[blk(2 * RET_QK), blk(RET_V), *_RET_CONST_SPECS(), ret_state],
        out_specs=[blk(RET_V), ret_state],
        out_shape=[jax.ShapeDtypeStruct((steps, STEP_ROWS, RET_V), F32), jax.ShapeDtypeStruct(s_ret.shape, F32)],
        compiler_params=_params(("arbitrary",)),
        name="retention",
    )(view(qk), view(v), *_retention_consts(c), s_ret)
    gdn_state = pl.BlockSpec((ns, GDN_HEADS, GDN_DK, GDN_DV), lambda g: (g, 0, 0, 0))
    o_g, gdn_new = pl.pallas_call(
        functools.partial(_gdn_kernel, ns=ns, c=c),
        grid=(steps,),
        in_specs=[blk(CONV_CH), blk(AB_W), gdn_state],
        out_specs=[blk(GDN_V), gdn_state],
        out_shape=[jax.ShapeDtypeStruct((steps, STEP_ROWS, GDN_V), F32), jax.ShapeDtypeStruct(s_gdn.shape, F32)],
        compiler_params=_params(("arbitrary",)),
        name="gdn",
    )(view(conv), view(gb), s_gdn)
    return o_r.reshape(nb * c, RET_V), o_g.reshape(nb * c, GDN_V), ret_new, gdn_new


def _prompt_mix_kernel(x_ref, ln_ref, wqkv_ref, wconv_ref, wab_ref, cos_ref, sin_ref, cw_ref, alog_ref, dtb_ref,
                       dintra_ref, dq_ref, dk_ref, dc_ref, or_ref, og_ref, sret_ref, sgdn_ref, tail_ref,
                       h_ref, qk_buf, v_buf, conv_buf, gb_buf, hist_ref):
    n = pl.program_id(0)
    nb = x_ref.shape[0]
    fill = n % 2
    drain = 1 - fill

    @pl.when(n == 0)
    def _():
        hist_ref[...] = jnp.zeros_like(hist_ref)
        sret_ref[...] = jnp.zeros_like(sret_ref)
        sgdn_ref[...] = jnp.zeros_like(sgdn_ref)
        qk_buf[1] = jnp.zeros(qk_buf.shape[1:], qk_buf.dtype)
        v_buf[1] = jnp.zeros(v_buf.shape[1:], v_buf.dtype)
        conv_buf[1] = jnp.zeros(conv_buf.shape[1:], conv_buf.dtype)
        gb_buf[1] = jnp.zeros(gb_buf.shape[1:], gb_buf.dtype)

    x = x_ref[...].reshape(nb * STEP_ROWS, D_MODEL)
    proj = _proj_stages(x, ln_ref, wqkv_ref, wconv_ref, wab_ref, cos_ref, sin_ref, cw_ref, alog_ref, dtb_ref,
                        hist_ref, qk_buf.at[fill], v_buf.at[fill], conv_buf.at[fill], gb_buf.at[fill], tail_ref,
                        h_ref, seq_is_group=False)
    def alternate(a, b):
        return [stage for pair in zip(a, b) for stage in pair]

    gdn, ret = [], []
    for p0 in range(0, nb, 2 * MIX_GROUP):
        halves = [range(p0, min(p0 + MIX_GROUP, nb)), range(min(p0 + MIX_GROUP, nb), min(p0 + 2 * MIX_GROUP, nb))]
        halves = [h for h in halves if len(h)]
        g = [_gdn_stages(conv_buf.at[drain], gb_buf.at[drain], og_ref, sgdn_ref, groups=h, ns=1, c=STEP_ROWS)
             for h in halves]
        r = [_retention_stages(qk_buf.at[drain], v_buf.at[drain], dintra_ref, dq_ref, dk_ref, dc_ref, or_ref,
                               sret_ref, groups=h, ns=1, c=STEP_ROWS) for h in halves]
        gdn += alternate(*g) if len(g) == 2 else g[0]
        ret += alternate(*r) if len(r) == 2 else r[0]
    _run_interleaved(proj, gdn, ret)


def _prompt_mix(x, pos, wts):
    ln1, w_qkv, w_conv, w_ab, conv_w, a_log, dt_bias = wts
    nb, t, _ = x.shape
    nc = t // STEP_ROWS
    cosf, sinf = _rope_tables(pos)
    this = lambda n: jnp.minimum(n, nc - 1)
    prev = lambda n: jnp.maximum(n - 1, 0)
    chunk = lambda cols, which: pl.BlockSpec((nb, STEP_ROWS, cols), lambda n: (0, which(n), 0),
                                             pipeline_mode=pl.Buffered(1))
    pos_spec = pl.BlockSpec((STEP_ROWS, RET_DK), lambda n: (this(n), 0))
    o_r, o_g, ret_new, gdn_new, tail = pl.pallas_call(
        _prompt_mix_kernel,
        grid=(nc + 1,),
        in_specs=[
            chunk(D_MODEL, this),
            _const_spec((1, D_MODEL)),
            _const_spec((D_MODEL, QKV_W)),
            _const_spec((D_MODEL, CONV_CH)),
            _const_spec((D_MODEL, AB_W)),
            pos_spec,
            pos_spec,
            _const_spec((CONV_W, CONV_CH)),
            _const_spec((1, LANES)),
            _const_spec((1, LANES)),
            *_RET_CONST_SPECS(),
        ],
        out_specs=[
            chunk(RET_V, prev),
            chunk(GDN_V, prev),
            _const_spec((nb, RET_HEADS, RET_DK, RET_DV)),
            _const_spec((nb, GDN_HEADS, GDN_DK, GDN_DV)),
            _const_spec((nb, SUBLANES, CONV_CH)),
        ],
        out_shape=[
            jax.ShapeDtypeStruct((nb, t, RET_V), F32),
            jax.ShapeDtypeStruct((nb, t, GDN_V), F32),
            jax.ShapeDtypeStruct((nb, RET_HEADS, RET_DK, RET_DV), F32),
            jax.ShapeDtypeStruct((nb, GDN_HEADS, GDN_DK, GDN_DV), F32),
            jax.ShapeDtypeStruct((nb, SUBLANES, CONV_CH), F32),
        ],
        scratch_shapes=[
            pltpu.VMEM((nb * STEP_ROWS, D_MODEL), BF16),
            pltpu.VMEM((2, nb, STEP_ROWS, 2 * RET_QK), BF16),
            pltpu.VMEM((2, nb, STEP_ROWS, RET_V), BF16),
            pltpu.VMEM((2, nb, STEP_ROWS, CONV_CH), F32),
            pltpu.VMEM((2, nb, STEP_ROWS, AB_W), F32),
            pltpu.VMEM((nb, SUBLANES, CONV_CH), F32),
        ],
        compiler_params=_params(("arbitrary",), MIX_VMEM_LIMIT),
        name="prompt_mix",
    )(x, ln1, w_qkv, w_conv, w_ab, cosf, sinf, conv_w, _pad_lanes(a_log), _pad_lanes(dt_bias),
      *_retention_consts(STEP_ROWS))
    conv_new = tail[:, SUBLANES - (CONV_W - 1):, :]
    return o_r.reshape(nb * t, RET_V), o_g.reshape(nb * t, GDN_V), ret_new, gdn_new, conv_new


def _attn_out_kernel(x_ref, or_ref, og_ref, ln1_ref, wgate_ref, rn_ref, gn_ref, wr_ref, wg_ref, wo_ref, o_ref,
                     h_ref, lhs_ref, br_ref):
    tm = x_ref.shape[0]
    h_ref[...] = _bf(_rms(x_ref[...], ln1_ref[...]))
    slabs = [slice(s * MXU_COLS, (s + 1) * MXU_COLS) for s in range(D_MODEL // MXU_COLS)]
    row_blocks = [slice(b, b + STEP_ROWS) for b in range(0, tm, STEP_ROWS)]

    def gate(block, sl):
        return _dot(h_ref[...], wgate_ref[:, block * D_MODEL + sl.start:block * D_MODEL + sl.stop])

    def branch(src_ref, head_dim, gain_ref, block, w_ref):
        for sl in slabs:
            g = gate(block, sl)
            for rows in row_blocks:
                o = src_ref[rows, sl]
                heads = [o[:, k:k + head_dim] for k in range(0, MXU_COLS, head_dim)]
                normed = [oh * lax.rsqrt(jnp.mean(oh * oh, axis=-1, keepdims=True) + EPS) for oh in heads]
                lhs_ref[rows, sl] = _bf(jnp.concatenate(normed, axis=-1) * gain_ref[:, sl] * _silu(g[rows]))
        return _dot(lhs_ref[...], w_ref[...])

    br_ref[0] = branch(or_ref, RET_DV, rn_ref, 0, wr_ref)
    br_ref[1] = branch(og_ref, GDN_DV, gn_ref, 1, wg_ref)
    for sl in slabs:
        g_r = gate(2, sl)
        g_g = gate(3, sl)
        for rows in row_blocks:
            lhs_ref[rows, sl] = _bf(jax.nn.sigmoid(g_r[rows]) * br_ref[0, rows, sl]
                                    + jax.nn.sigmoid(g_g[rows]) * br_ref[1, rows, sl])
    o_ref[...] = x_ref[...] + _dot(lhs_ref[...], wo_ref[...])


def _attn_out(x2d, o_r, o_g, ln1, w_gate, ret_norm, gdn_norm, w_ret_br, w_gdn_br, w_out):
    n = x2d.shape[0]
    tm = min(512, n)
    rows = pl.BlockSpec((tm, D_MODEL), lambda i: (i, 0))
    vec = _const_spec((1, D_MODEL))
    wspec = _const_spec((D_MODEL, D_MODEL))
    return pl.pallas_call(
        _attn_out_kernel,
        grid=(n // tm,),
        in_specs=[rows, rows, rows, vec, _const_spec((D_MODEL, GATE_W)), vec, vec, wspec, wspec, wspec],
        out_specs=rows,
        out_shape=jax.ShapeDtypeStruct((n, D_MODEL), F32),
        scratch_shapes=[pltpu.VMEM((tm, D_MODEL), BF16), pltpu.VMEM((tm, D_MODEL), BF16),
                        pltpu.VMEM((2, tm, D_MODEL), F32)],
        compiler_params=_params(("arbitrary",)),
        name="attn_out",
    )(x2d, o_r, o_g, ln1, w_gate, ret_norm, gdn_norm, w_ret_br, w_gdn_br, w_out)


FF_BLOCK = 1024


def _mlp_kernel(x_ref, ln2_ref, wu_ref, wd_ref, lnf_ref, o_ref):
    x = x_ref[...]
    hb = _bf(_rms(x, ln2_ref[...]))
    acc = x
    for f in range(D_FF // FF_BLOCK):
        sl = slice(f * FF_BLOCK, (f + 1) * FF_BLOCK)
        up = jnp.maximum(_dot(hb, wu_ref[:, sl]), 0.0)
        acc = acc + _dot(_bf(up * up), wd_ref[sl, :])
    o_ref[...] = _rms(acc, lnf_ref[...])


def _mlp(x2d, ln2, w_up, w_down, ln_f):
    n = x2d.shape[0]
    tm = min(1024, n)
    rows = pl.BlockSpec((tm, D_MODEL), lambda i: (i, 0))
    return pl.pallas_call(
        _mlp_kernel,
        grid=(n // tm,),
        in_specs=[rows, _const_spec((1, D_MODEL)), _const_spec((D_MODEL, D_FF)), _const_spec((D_FF, D_MODEL)),
                  _const_spec((1, D_MODEL))],
        out_specs=rows,
        out_shape=jax.ShapeDtypeStruct((n, D_MODEL), F32),
        compiler_params=_params(("arbitrary",)),
        name="mlp",
    )(x2d, ln2, w_up, w_down, ln_f)


def _finish(x, o_r, o_g, wts):
    (in_wts, w_gate, ret_norm, gdn_norm, w_ret_br, w_gdn_br, w_out, ln2, w_up, w_down, ln_f) = wts
    x2d = x.reshape(-1, D_MODEL)
    x1 = _attn_out(x2d, o_r, o_g, in_wts[0], w_gate, ret_norm, gdn_norm, w_ret_br, w_gdn_br, w_out)
    return _mlp(x1, ln2, w_up, w_down, ln_f).reshape(x.shape)


def kernel(x_prompt, x_sample, state_ret, state_gdn, state_conv, ln1, w_in, conv_w, a_log, dt_bias, ret_norm,
           gdn_norm, w_ret_br, w_gdn_br, w_out, ln2, w_up, w_down, ln_f):
    depth = w_in.shape[0]
    assert depth == 1, "single-layer trunk"
    tp = x_prompt.shape[1]
    nbs, ts, _ = x_sample.shape
    assert tp % CHUNK == 0 and ts == SUBLANES, "prompt in 64-token chunks, sample in 8-token sequences"
    w = w_in[0]
    o_gr = 2 * RET_QK + RET_V
    o_qkv = o_gr + RET_V
    o_a = o_qkv + CONV_CH
    o_z = o_a + 2 * GDN_HEADS
    w_qkv = w[:, :o_gr].astype(BF16)
    w_conv = w[:, o_qkv:o_a].astype(BF16)
    w_gate = jnp.concatenate([w[:, o_gr:o_qkv], w[:, o_z:]], axis=1).astype(BF16)
    w_ab = jnp.zeros((D_MODEL, AB_W), F32)
    w_ab = w_ab.at[:, :GDN_HEADS].set(w[:, o_a:o_a + GDN_HEADS])
    w_ab = w_ab.at[:, LANES:LANES + GDN_HEADS].set(w[:, o_a + GDN_HEADS:o_z]).astype(BF16)
    vec = lambda v: v.reshape(1, -1)
    in_wts = (vec(ln1[0]), w_qkv, w_conv, w_ab, conv_w[0], a_log[0], dt_bias[0])
    wts = (in_wts, w_gate, vec(ret_norm[0]), vec(gdn_norm[0]), w_ret_br[0].astype(BF16), w_gdn_br[0].astype(BF16),
           w_out[0].astype(BF16), vec(ln2[0]), w_up[0].astype(BF16), w_down[0].astype(BF16), vec(ln_f))

    o_r, o_g, ret_p, gdn_p, conv_p = _prompt_mix(x_prompt, jnp.arange(tp, dtype=F32), in_wts)
    yp = _finish(x_prompt, o_r, o_g, wts)

    pos_s = PAST_LEN + jnp.arange(ts, dtype=F32)
    qk, v, conv, gb, conv_s = _in_proj(x_sample.reshape(nbs * ts, D_MODEL), pos_s, state_conv[0], in_wts)
    o_r, o_g, ret_s, gdn_s = _sample_recurrences(qk, v, conv, gb, state_ret[0], state_gdn[0], ts)
    ys = _finish(x_sample, o_r, o_g, wts)
    return (yp, ys, ret_p[None], gdn_p[None], conv_p[None], ret_s[None], gdn_s[None], conv_s[None])
```

```python
import functools

import jax
import jax.numpy as jnp
from jax import lax
from jax.experimental import pallas as pl
from jax.experimental.pallas import tpu as pltpu

F32 = jnp.float32
BF16 = jnp.bfloat16

D_MODEL = 1024
PAST_LEN = 16384
RET_HEADS, RET_DK, RET_DV = 4, 128, 256
RET_QK = RET_HEADS * RET_DK
RET_V = RET_HEADS * RET_DV
GDN_HEADS, GDN_DK, GDN_DV = 8, 128, 128
GDN_QK = GDN_HEADS * GDN_DK
GDN_V = GDN_HEADS * GDN_DV
CONV_W = 4
CONV_CH = 2 * GDN_QK + GDN_V
D_FF = 4 * D_MODEL
CHUNK = 64
ROPE_BASE = 10000.0
EPS = 1e-6

SUBLANES = 8
LANES = 128
MXU_COLS = 256
STEP_ROWS = 64
EPILOGUE_ROWS = 64
VMEM_LIMIT = 56 * 1024 * 1024

QKV_W = 2 * RET_QK + RET_V
GATE_W = 4 * D_MODEL
AB_W = 2 * LANES


def _params(sem):
    return pltpu.CompilerParams(dimension_semantics=sem, vmem_limit_bytes=VMEM_LIMIT)


def _const_spec(shape):
    zeros = (0,) * len(shape)
    return pl.BlockSpec(shape, lambda *_: zeros, pipeline_mode=pl.Buffered(1))


def _dot(a, b):
    return jnp.dot(a, b, preferred_element_type=F32)


def _dot_nt(a, b):
    return lax.dot_general(a, b, (((1,), (1,)), ((), ())), preferred_element_type=F32)


def _dot_tn(a, b):
    return lax.dot_general(a, b, (((0,), (0,)), ((), ())), preferred_element_type=F32)


def _bf(x):
    return x.astype(BF16)


def _rows(parts):
    return jnp.concatenate(parts, axis=0)


def _rms(x, g):
    return x * lax.rsqrt(jnp.mean(x * x, axis=-1, keepdims=True) + EPS) * g


def _silu(x):
    return x * jax.nn.sigmoid(x)


def _in_proj_kernel(x_ref, ln_ref, wqkv_ref, wconv_ref, wab_ref, cos_ref, sin_ref, cw_ref, hist_ref,
                    alog_ref, dtb_ref, qk_ref, v_ref, conv_ref, gb_ref, tail_ref, h_ref, carry_ref,
                    *, seq_is_group, tiles_per_seq):
    i = pl.program_id(0)
    tm = x_ref.shape[0]
    row_blocks = [slice(b, b + EPILOGUE_ROWS) for b in range(0, tm, EPILOGUE_ROWS)]

    @pl.when(i == 0)
    def _():
        carry_ref[...] = jnp.zeros_like(carry_ref)

    h_ref[...] = _bf(_rms(x_ref[...], ln_ref[...]))
    ab = _dot(h_ref[...], wab_ref[...])
    gb_ref[:, :LANES] = -jnp.exp(alog_ref[...]) * jax.nn.softplus(ab[:, :LANES] + dtb_ref[...])
    gb_ref[:, LANES:] = jax.nn.sigmoid(ab[:, LANES:])

    def rope_slab(s):
        sl = slice(s * MXU_COLS, (s + 1) * MXU_COLS)
        r = _dot(h_ref[...], wqkv_ref[:, sl])
        for rows in row_blocks:
            for part in range(MXU_COLS // RET_DK):
                x = r[rows, part * RET_DK:(part + 1) * RET_DK]
                y = x * cos_ref[rows, :] + pltpu.roll(x, RET_DK // 2, 1) * sin_ref[rows, :]
                start = sl.start + part * RET_DK
                if start >= RET_QK:
                    y = y * (RET_DK ** -0.5)
                qk_ref[rows, start:start + RET_DK] = _bf(y)

    def v_slab(s):
        sl = slice(s * MXU_COLS, (s + 1) * MXU_COLS)
        v_ref[:, sl] = _bf(_dot(h_ref[...], wqkv_ref[:, 2 * RET_QK + sl.start:2 * RET_QK + sl.stop]))

    sub = lax.broadcasted_iota(jnp.int32, (1, SUBLANES, MXU_COLS), 1)

    def conv_slab(s):
        sl = slice(s * MXU_COLS, (s + 1) * MXU_COLS)
        w = [cw_ref[tap:tap + 1, sl][None] for tap in range(CONV_W)]
        r = _dot(h_ref[...], wconv_ref[:, sl])
        if seq_is_group:
            tail_ref[:, sl] = r
        else:
            tail_ref[:, :, sl] = r[tm - SUBLANES:][None]
            x_last = jnp.where(i % tiles_per_seq == 0, hist_ref[:, :, sl], carry_ref[:, sl][None])
            carry_ref[:, sl] = r[tm - SUBLANES:]
            x_last1 = pltpu.roll(x_last, 1, 1)
            pair_last2 = pltpu.roll(w[1] * x_last + w[0] * x_last1, 2, 1)
        for rows in row_blocks:
            g = (rows.stop - rows.start) // SUBLANES
            r3 = r[rows].reshape(g, SUBLANES, MXU_COLS)
            cur1 = pltpu.roll(r3, 1, 1)
            if seq_is_group:
                hist3 = hist_ref[rows, sl].reshape(g, SUBLANES, MXU_COLS)
                prev1 = pltpu.roll(hist3, 1, 1)
                prev2 = pltpu.roll(w[1] * hist3 + w[0] * prev1, 2, 1)
            else:
                prev1 = jnp.concatenate([x_last1, cur1[:g - 1]], axis=0)
            x1 = jnp.where(sub < 1, prev1, cur1)
            pair = w[1] * r3 + w[0] * x1
            cur2 = pltpu.roll(pair, 2, 1)
            if not seq_is_group:
                prev2 = jnp.concatenate([pair_last2, cur2[:g - 1]], axis=0)
                x_last1, pair_last2 = cur1[g - 1:], cur2[g - 1:]
            y = (w[3] * r3 + w[2] * x1) + jnp.where(sub < 2, prev2, cur2)
            conv_ref[rows, sl] = _silu(y).reshape(g * SUBLANES, MXU_COLS)

    light = ([functools.partial(rope_slab, s) for s in range(2 * RET_QK // MXU_COLS)]
             + [functools.partial(v_slab, s) for s in range(RET_V // MXU_COLS)])
    for s in range(CONV_CH // MXU_COLS):
        conv_slab(s)
        if s < len(light):
            light[s]()


def _in_proj(x2d, t, pos, conv0, wts):
    ln1, w_qkv, w_conv, w_ab, conv_w, a_log, dt_bias = wts
    n = x2d.shape[0]
    nb = n // t
    seq_is_group = t == SUBLANES
    tm = min(256, n) if seq_is_group else min(512, t)
    assert n % tm == 0 and (seq_is_group or t % tm == 0), (n, t, tm)
    tiles_per_seq = max(t // tm, 1)
    inv = ROPE_BASE ** (-jnp.arange(0, RET_DK, 2, dtype=F32) / RET_DK)
    ang = jnp.tile(pos, max(tm // t, 1))[:, None] * inv[None, :]
    cosf = jnp.concatenate([jnp.cos(ang), jnp.cos(ang)], axis=-1)
    sinf = jnp.concatenate([-jnp.sin(ang), jnp.sin(ang)], axis=-1)
    pos_blocks = cosf.shape[0] // tm
    hist = jnp.pad(conv0, ((0, 0), (SUBLANES - (CONV_W - 1), 0), (0, 0)))
    rows = lambda cols: pl.BlockSpec((tm, cols), lambda i: (i, 0))
    if seq_is_group:
        hist = hist.reshape(n, CONV_CH)
        hist_spec = rows(CONV_CH)
        tail_spec = hist_spec
        tail_shape = jax.ShapeDtypeStruct((n, CONV_CH), F32)
    else:
        hist_spec = pl.BlockSpec((1, SUBLANES, CONV_CH), lambda i: (i // tiles_per_seq, 0, 0))
        tail_spec = pl.BlockSpec((1, SUBLANES, CONV_CH), lambda i: (i, 0, 0))
        tail_shape = jax.ShapeDtypeStruct((n // tm, SUBLANES, CONV_CH), F32)
    pad_lanes = lambda x: jnp.pad(x.reshape(1, GDN_HEADS), ((0, 0), (0, LANES - GDN_HEADS)))
    pos_spec = pl.BlockSpec((tm, RET_DK), lambda i: (i % pos_blocks, 0))
    qk, v, conv, gb, tail = pl.pallas_call(
        functools.partial(_in_proj_kernel, seq_is_group=seq_is_group, tiles_per_seq=tiles_per_seq),
        grid=(n // tm,),
        in_specs=[
            rows(D_MODEL),
            _const_spec((1, D_MODEL)),
            _const_spec((D_MODEL, QKV_W)),
            _const_spec((D_MODEL, CONV_CH)),
            _const_spec((D_MODEL, AB_W)),
            pos_spec,
            pos_spec,
            _const_spec((CONV_W, CONV_CH)),
            hist_spec,
            _const_spec((1, LANES)),
            _const_spec((1, LANES)),
        ],
        out_specs=[rows(2 * RET_QK), rows(RET_V), rows(CONV_CH), rows(AB_W), tail_spec],
        out_shape=[
            jax.ShapeDtypeStruct((n, 2 * RET_QK), BF16),
            jax.ShapeDtypeStruct((n, RET_V), BF16),
            jax.ShapeDtypeStruct((n, CONV_CH), F32),
            jax.ShapeDtypeStruct((n, AB_W), F32),
            tail_shape,
        ],
        scratch_shapes=[pltpu.VMEM((tm, D_MODEL), BF16), pltpu.VMEM((SUBLANES, CONV_CH), F32)],
        compiler_params=_params(("arbitrary",)),
        name="in_proj",
    )(x2d, ln1, w_qkv, w_conv, w_ab, cosf, sinf, conv_w, hist, pad_lanes(a_log), pad_lanes(dt_bias))
    conv_new = tail.reshape(nb, -1, SUBLANES, CONV_CH)[:, -1, SUBLANES - (CONV_W - 1):, :]
    return qk, v, conv, gb, conv_new


def _step_layout(nb, t, c):
    ns = STEP_ROWS // c
    if ns == 1:
        par = max(p for p in (8, 4, 2, 1) if nb % p == 0)
        return nb, t, ns, par
    assert t == c and nb % ns == 0, (nb, t, c)
    return nb // ns, STEP_ROWS, ns, 1


def _retention_kernel(q_ref, k_ref, v_ref, dintra_ref, dq_ref, dk_ref, dc_ref, s0_ref, o_ref, s_ref,
                      *, par, ns, c):
    @pl.when(pl.program_id(1) == 0)
    def _():
        s_ref[...] = s0_ref[...]

    units = [(p, h) for p in range(par) for h in range(RET_HEADS)]
    ids = range(len(units))
    dk_sl = lambda h: slice(h * RET_DK, (h + 1) * RET_DK)
    dv_sl = lambda h: slice(h * RET_DV, (h + 1) * RET_DV)
    qb = [q_ref[p, :, dk_sl(h)] for p, h in units]
    kb = [k_ref[p, :, dk_sl(h)] for p, h in units]
    vb = [v_ref[p, :, dv_sl(h)] for p, h in units]
    scores = [_dot_nt(qb[u], kb[u]) * dintra_ref[units[u][1]] for u in ids]
    kd = [kb[u].astype(F32) * dk_ref[units[u][1]] for u in ids]
    if ns == 1:
        s = [s_ref[p, h] for p, h in units]
        qs = [_dot(qb[u], _bf(s[u])) for u in ids]
        upd = [_dot(_bf(_rows([kd[u].T, scores[u]])), vb[u]) for u in ids]
        for u, (p, h) in enumerate(units):
            o_ref[p, :, dv_sl(h)] = upd[u][RET_DK:] + qs[u] * dq_ref[h]
            s_ref[p, h] = s[u] * dc_ref[h] + upd[u][:RET_DK]
    else:
        seqs = range(ns)
        rows = lambda x, j: x[j * c:(j + 1) * c]
        q32 = [qb[u].astype(F32) for u in ids]
        v32 = [vb[u].astype(F32) for u in ids]
        s = [[s_ref[j, h] for j in seqs] for _, h in units]
        qs = [_rows([_dot(_bf(rows(q32[u], j)), _bf(s[u][j])) for j in seqs]) for u in ids]
        intra = [_dot(_bf(scores[u]), vb[u]) for u in ids]
        for u, (p, h) in enumerate(units):
            o_ref[p, :, dv_sl(h)] = intra[u] + qs[u] * dq_ref[h]
        ktv = [[_dot_tn(_bf(rows(kd[u], j)), _bf(rows(v32[u], j))) for j in seqs] for u in ids]
        for u, (p, h) in enumerate(units):
            for j in seqs:
                s_ref[j, h] = s[u][j] * dc_ref[h] + ktv[u][j]


def _retention(qk, v, s0, t, c):
    nb = s0.shape[0]
    groups, rows, ns, par = _step_layout(nb, t, c)
    log_g = jnp.log1p(-jnp.exp2(-5.0 - jnp.arange(RET_HEADS, dtype=F32)))
    idx = jnp.arange(STEP_ROWS)
    same = (idx[:, None] // c) == (idx[None, :] // c)
    off = (idx % c).astype(F32)
    diff = off[:, None] - off[None, :]
    causal = same & (diff >= 0)
    d_intra = jnp.where(causal[None], jnp.exp(log_g[:, None, None] * jnp.where(causal, diff, 0.0)[None]), 0.0)
    d_q = jnp.broadcast_to(jnp.exp(log_g[:, None] * (off + 1.0)[None, :])[..., None], (RET_HEADS, STEP_ROWS, RET_DV))
    d_k = jnp.broadcast_to(jnp.exp(log_g[:, None] * (c - 1.0 - off)[None, :])[..., None],
                           (RET_HEADS, STEP_ROWS, RET_DK))
    d_c = jnp.broadcast_to(jnp.exp(log_g * c)[:, None, None], (RET_HEADS, 1, RET_DV))
    blk = lambda cols, col_blk: pl.BlockSpec((par, STEP_ROWS, cols), lambda g, i: (g, i, col_blk))
    state_spec = pl.BlockSpec((par * ns, RET_HEADS, RET_DK, RET_DV), lambda g, i: (g, 0, 0, 0))
    qk3 = qk.reshape(groups, rows, 2 * RET_QK)
    o, s_new = pl.pallas_call(
        functools.partial(_retention_kernel, par=par, ns=ns, c=c),
        grid=(groups // par, rows // STEP_ROWS),
        in_specs=[
            blk(RET_QK, 0),
            blk(RET_QK, 1),
            blk(RET_V, 0),
            _const_spec((RET_HEADS, STEP_ROWS, STEP_ROWS)),
            _const_spec((RET_HEADS, STEP_ROWS, RET_DV)),
            _const_spec((RET_HEADS, STEP_ROWS, RET_DK)),
            _const_spec((RET_HEADS, 1, RET_DV)),
            state_spec,
        ],
        out_specs=[blk(RET_V, 0), state_spec],
        out_shape=[jax.ShapeDtypeStruct((groups, rows, RET_V), F32), jax.ShapeDtypeStruct(s0.shape, F32)],
        compiler_params=_params(("arbitrary", "arbitrary")),
        name="retention",
    )(qk3, qk3, v.reshape(groups, rows, RET_V), d_intra, d_q, d_k, d_c, s0)
    return o.reshape(nb * t, RET_V), s_new


def _block_diag(x2):
    m = x2.shape[0]
    r = lax.broadcasted_iota(jnp.int32, (2 * m, 2 * m), 0)
    cc = lax.broadcasted_iota(jnp.int32, (2 * m, 2 * m), 1)
    return jnp.where((r < m) == (cc < m), _rows([x2, x2]), jnp.zeros((), x2.dtype))


def _solve_correction(a, c):
    ids = range(len(a))
    m = a[0].shape[0]
    n = [-x for x in a]
    ab = [_bf(x) for x in a]
    p = [_dot(ab[i], _block_diag(ab[i])) for i in ids]
    span = 2
    while span < c:
        pb = [_block_diag(_bf(x)) for x in p]
        last = 2 * span >= c
        if last:
            prod = [_dot(_bf(n[i]), pb[i]) for i in ids]
            n = [n[i] + p[i] + prod[i] for i in ids]
        else:
            prod = [_dot(_bf(_rows([n[i], p[i]])), pb[i]) for i in ids]
            n = [n[i] + p[i] + prod[i][:m] for i in ids]
            p = [prod[i][m:] for i in ids]
        span *= 2
    return n


def _gdn_kernel(qkv_ref, gb_ref, s0_ref, o_ref, s_ref, *, par, ns, c):
    @pl.when(pl.program_id(1) == 0)
    def _():
        s_ref[...] = s0_ref[...]

    m = STEP_ROWS

    def l2norm(x):
        return x * lax.rsqrt(jnp.sum(x * x, axis=-1, keepdims=True) + EPS)

    lane = lax.broadcasted_iota(jnp.int32, (m, LANES), 1)
    off = lax.broadcasted_iota(jnp.int32, (m, LANES), 0) % c
    r2 = lax.broadcasted_iota(jnp.int32, (m, 2 * m), 0)
    lane2 = lax.broadcasted_iota(jnp.int32, (m, 2 * m), 1)
    c2 = lane2 % m
    same = (r2 // c) == (c2 // c)
    tril = same & (r2 >= c2)
    strict = same & (r2 > c2)
    even = lane2 < m
    pick = (lax.broadcasted_iota(jnp.int32, (2 * m, 2 * GDN_DK), 0) < m) == (
        lax.broadcasted_iota(jnp.int32, (2 * m, 2 * GDN_DK), 1) < GDN_DK)

    def gates(p):
        cum = jnp.where(lane < GDN_HEADS, gb_ref[p, :, :LANES], 0.0)
        shift = 1
        while shift < c:
            cum = cum + jnp.where(off >= shift, pltpu.roll(cum, shift, 0), 0.0)
            shift *= 2
        last = cum.reshape(ns, c, LANES)[:, c - 1:c, :]
        e_rest = jnp.exp((last - cum.reshape(ns, c, LANES)).reshape(m, LANES))
        return dict(cum=cum, cum_t=cum.T, e_cum=jnp.exp(cum), e_rest=e_rest, e_last=jnp.exp(last),
                    beta=gb_ref[p, :, LANES:])

    def padded(x, h):
        z = jnp.zeros_like(x)
        return _rows([x, z]) if h % 2 == 0 else _rows([z, x])

    gt = [gates(p) for p in range(par)]
    units = [(p, h) for p in range(par) for h in range(GDN_HEADS)]
    ids = range(len(units))
    pairs = range(len(units) // 2)
    pair_of = lambda u: u // 2
    both = lambda xs, i: jnp.concatenate([xs[2 * i], xs[2 * i + 1]], axis=-1)
    col = lambda x, h: x[:, h:h + 1]
    q = [l2norm(qkv_ref[p, :, h * GDN_DK:(h + 1) * GDN_DK]) * (GDN_DK ** -0.5) for p, h in units]
    k = [l2norm(qkv_ref[p, :, GDN_QK + h * GDN_DK:GDN_QK + (h + 1) * GDN_DK]) for p, h in units]
    v = [qkv_ref[p, :, 2 * GDN_QK + h * GDN_DV:2 * GDN_QK + (h + 1) * GDN_DV] for p, h in units]
    beta = [col(gt[p]["beta"], h) for p, h in units]
    eg = [col(gt[p]["e_cum"], h) for p, h in units]
    cum_col = [col(gt[p]["cum"], h) for p, h in units]
    cum_row = [gt[p]["cum_t"][h:h + 1, :] for p, h in units]
    decay = [jnp.exp(jnp.where(tril, jnp.where(even, cum_col[2 * i], cum_col[2 * i + 1]) - both(cum_row, i), -jnp.inf))
             for i in pairs]
    kb = [k[u] * beta[u] for u in ids]
    kpick = [jnp.where(pick, _rows([_bf(both(k, i))] * 2), jnp.zeros((), BF16)) for i in pairs]
    raw = [_dot_nt(_bf(_rows([both(kb, i), both(q, i)])), kpick[i]) for i in pairs]
    a = [jnp.where(strict, raw[i][:m] * decay[i], 0.0) for i in pairs]
    attn = [raw[i][m:] * decay[i] for i in pairs]
    n = _solve_correction(a, c)
    rhs = [jnp.concatenate([v[u] * beta[u], kb[u] * eg[u]], axis=-1) for u in ids]
    nb16 = [_bf(x) for x in n]
    sol = [rhs[u] + _dot(nb16[pair_of(u)], padded(_bf(rhs[u]), units[u][1])) for u in ids]
    uu = [sol[u][:, :GDN_DV] for u in ids]
    w = [sol[u][:, GDN_DV:] for u in ids]
    qe = [q[u] * eg[u] for u in ids]
    kd = [k[u] * col(gt[p]["e_rest"], h) for u, (p, h) in enumerate(units)]
    dv_sl = lambda h: slice(h * GDN_DV, (h + 1) * GDN_DV)
    if ns == 1:
        s = [s_ref[p, h] for p, h in units]
        ws_qs = [_dot(_bf(_rows([w[u], qe[u]])), _bf(s[u])) for u in ids]
        v_new = [uu[u] - ws_qs[u][:m] for u in ids]
        upd = [_dot(_bf(_rows([padded(kd[u], h).T, attn[pair_of(u)]])), padded(_bf(v_new[u]), h))
               for u, (p, h) in enumerate(units)]
        for u, (p, h) in enumerate(units):
            o_ref[p, :, dv_sl(h)] = ws_qs[u][m:] + upd[u][GDN_DK:]
            s_ref[p, h] = s[u] * gt[p]["e_last"][0, :, h:h + 1] + upd[u][:GDN_DK]
    else:
        seqs = range(ns)
        rows = lambda x, j: x[j * c:(j + 1) * c]
        s = [[s_ref[j, h] for j in seqs] for _, h in units]
        ws_qs = [[_dot(_bf(_rows([rows(w[u], j), rows(qe[u], j)])), _bf(s[u][j])) for j in seqs] for u in ids]
        v_new = [_rows([rows(uu[u], j) - ws_qs[u][j][:c] for j in seqs]) for u in ids]
        qs = [_rows([ws_qs[u][j][c:] for j in seqs]) for u in ids]
        intra = [_dot(_bf(attn[pair_of(u)]), padded(_bf(v_new[u]), units[u][1])) for u in ids]
        for u, (p, h) in enumerate(units):
            o_ref[p, :, dv_sl(h)] = qs[u] + intra[u]
        ktv = [[_dot_tn(_bf(rows(kd[u], j)), _bf(rows(v_new[u], j))) for j in seqs] for u in ids]
        for u, (p, h) in enumerate(units):
            for j in seqs:
                s_ref[j, h] = s[u][j] * gt[p]["e_last"][j, :, h:h + 1] + ktv[u][j]


def _gdn(conv, gb, s0, t, c):
    nb = s0.shape[0]
    groups, rows, ns, par = _step_layout(nb, t, c)
    blk = lambda cols: pl.BlockSpec((par, STEP_ROWS, cols), lambda g, i: (g, i, 0))
    state_spec = pl.BlockSpec((par * ns, GDN_HEADS, GDN_DK, GDN_DV), lambda g, i: (g, 0, 0, 0))
    o, s_new = pl.pallas_call(
        functools.partial(_gdn_kernel, par=par, ns=ns, c=c),
        grid=(groups // par, rows // STEP_ROWS),
        in_specs=[blk(CONV_CH), blk(AB_W), state_spec],
        out_specs=[blk(GDN_V), state_spec],
        out_shape=[jax.ShapeDtypeStruct((groups, rows, GDN_V), F32), jax.ShapeDtypeStruct(s0.shape, F32)],
        compiler_params=_params(("arbitrary", "arbitrary")),
        name="gdn",
    )(conv.reshape(groups, rows, CONV_CH), gb.reshape(groups, rows, AB_W), s0)
    return o.reshape(nb * t, GDN_V), s_new


def _attn_out_kernel(x_ref, or_ref, og_ref, ln1_ref, wgate_ref, rn_ref, gn_ref, wr_ref, wg_ref, wo_ref, o_ref,
                     h_ref, lhs_ref, br_ref):
    tm = x_ref.shape[0]
    h_ref[...] = _bf(_rms(x_ref[...], ln1_ref[...]))
    slabs = [slice(s * MXU_COLS, (s + 1) * MXU_COLS) for s in range(D_MODEL // MXU_COLS)]
    row_blocks = [slice(b, b + EPILOGUE_ROWS) for b in range(0, tm, EPILOGUE_ROWS)]

    def gate(block, sl):
        return _dot(h_ref[...], wgate_ref[:, block * D_MODEL + sl.start:block * D_MODEL + sl.stop])

    def branch(src_ref, head_dim, gain_ref, block, w_ref):
        for sl in slabs:
            g = gate(block, sl)
            for rows in row_blocks:
                o = src_ref[rows, sl]
                heads = [o[:, k:k + head_dim] for k in range(0, MXU_COLS, head_dim)]
                normed = [oh * lax.rsqrt(jnp.mean(oh * oh, axis=-1, keepdims=True) + EPS) for oh in heads]
                lhs_ref[rows, sl] = _bf(jnp.concatenate(normed, axis=-1) * gain_ref[:, sl] * _silu(g[rows]))
        return _dot(lhs_ref[...], w_ref[...])

    br_ref[0] = branch(or_ref, RET_DV, rn_ref, 0, wr_ref)
    br_ref[1] = branch(og_ref, GDN_DV, gn_ref, 1, wg_ref)
    for sl in slabs:
        g_r = gate(2, sl)
        g_g = gate(3, sl)
        for rows in row_blocks:
            lhs_ref[rows, sl] = _bf(jax.nn.sigmoid(g_r[rows]) * br_ref[0, rows, sl]
                                    + jax.nn.sigmoid(g_g[rows]) * br_ref[1, rows, sl])
    o_ref[...] = x_ref[...] + _dot(lhs_ref[...], wo_ref[...])


def _attn_out(x2d, o_r, o_g, ln1, w_gate, ret_norm, gdn_norm, w_ret_br, w_gdn_br, w_out):
    n = x2d.shape[0]
    tm = min(512, n)
    rows = pl.BlockSpec((tm, D_MODEL), lambda i: (i, 0))
    vec = _const_spec((1, D_MODEL))
    wspec = _const_spec((D_MODEL, D_MODEL))
    return pl.pallas_call(
        _attn_out_kernel,
        grid=(n // tm,),
        in_specs=[rows, rows, rows, vec, _const_spec((D_MODEL, GATE_W)), vec, vec, wspec, wspec, wspec],
        out_specs=rows,
        out_shape=jax.ShapeDtypeStruct((n, D_MODEL), F32),
        scratch_shapes=[pltpu.VMEM((tm, D_MODEL), BF16), pltpu.VMEM((tm, D_MODEL), BF16),
                        pltpu.VMEM((2, tm, D_MODEL), F32)],
        compiler_params=_params(("arbitrary",)),
        name="attn_out",
    )(x2d, o_r, o_g, ln1, w_gate, ret_norm, gdn_norm, w_ret_br, w_gdn_br, w_out)


FF_BLOCK = 1024


def _mlp_kernel(x_ref, ln2_ref, wu_ref, wd_ref, lnf_ref, o_ref):
    x = x_ref[...]
    hb = _bf(_rms(x, ln2_ref[...]))
    acc = x
    for f in range(D_FF // FF_BLOCK):
        sl = slice(f * FF_BLOCK, (f + 1) * FF_BLOCK)
        up = jnp.maximum(_dot(hb, wu_ref[:, sl]), 0.0)
        acc = acc + _dot(_bf(up * up), wd_ref[sl, :])
    o_ref[...] = _rms(acc, lnf_ref[...])


def _mlp(x2d, ln2, w_up, w_down, ln_f):
    n = x2d.shape[0]
    tm = min(1024, n)
    rows = pl.BlockSpec((tm, D_MODEL), lambda i: (i, 0))
    return pl.pallas_call(
        _mlp_kernel,
        grid=(n // tm,),
        in_specs=[rows, _const_spec((1, D_MODEL)), _const_spec((D_MODEL, D_FF)), _const_spec((D_FF, D_MODEL)),
                  _const_spec((1, D_MODEL))],
        out_specs=rows,
        out_shape=jax.ShapeDtypeStruct((n, D_MODEL), F32),
        compiler_params=_params(("arbitrary",)),
        name="mlp",
    )(x2d, ln2, w_up, w_down, ln_f)


def _group(x, pos, s_ret, s_gdn, s_conv, wts):
    (in_wts, w_gate, ret_norm, gdn_norm, w_ret_br, w_gdn_br, w_out, ln2, w_up, w_down, ln_f) = wts
    nb, t, _ = x.shape
    c = CHUNK if t % CHUNK == 0 else t
    assert STEP_ROWS % c == 0 and nb % (STEP_ROWS // c) == 0, (nb, t)
    x2d = x.reshape(nb * t, D_MODEL)
    qk, v, conv, gb, conv_new = _in_proj(x2d, t, pos, s_conv, in_wts)
    o_r, ret_new = _retention(qk, v, s_ret, t, c)
    o_g, gdn_new = _gdn(conv, gb, s_gdn, t, c)
    x1 = _attn_out(x2d, o_r, o_g, in_wts[0], w_gate, ret_norm, gdn_norm, w_ret_br, w_gdn_br, w_out)
    y = _mlp(x1, ln2, w_up, w_down, ln_f)
    return y.reshape(x.shape), ret_new, gdn_new, conv_new


def kernel(x_prompt, x_sample, state_ret, state_gdn, state_conv, ln1, w_in, conv_w, a_log, dt_bias, ret_norm,
           gdn_norm, w_ret_br, w_gdn_br, w_out, ln2, w_up, w_down, ln_f):
    depth = w_in.shape[0]
    assert depth == 1, "single-layer trunk"
    bp, tp, _ = x_prompt.shape
    ts = x_sample.shape[1]
    w = w_in[0]
    o_gr = 2 * RET_QK + RET_V
    o_qkv = o_gr + RET_V
    o_a = o_qkv + CONV_CH
    o_z = o_a + 2 * GDN_HEADS
    w_qkv = w[:, :o_gr].astype(BF16)
    w_conv = w[:, o_qkv:o_a].astype(BF16)
    w_gate = jnp.concatenate([w[:, o_gr:o_qkv], w[:, o_z:]], axis=1).astype(BF16)
    w_ab = jnp.zeros((D_MODEL, AB_W), F32)
    w_ab = w_ab.at[:, :GDN_HEADS].set(w[:, o_a:o_a + GDN_HEADS])
    w_ab = w_ab.at[:, LANES:LANES + GDN_HEADS].set(w[:, o_a + GDN_HEADS:o_z]).astype(BF16)
    vec = lambda v: v.reshape(1, -1)
    in_wts = (vec(ln1[0]), w_qkv, w_conv, w_ab, conv_w[0], a_log[0], dt_bias[0])
    wts = (in_wts, w_gate, vec(ret_norm[0]), vec(gdn_norm[0]), w_ret_br[0].astype(BF16), w_gdn_br[0].astype(BF16),
           w_out[0].astype(BF16), vec(ln2[0]), w_up[0].astype(BF16), w_down[0].astype(BF16), vec(ln_f))
    pos_p = jnp.arange(tp, dtype=F32)
    pos_s = PAST_LEN + jnp.arange(ts, dtype=F32)
    zeros = lambda *shape: jnp.zeros((bp,) + shape, F32)
    yp, ret_p, gdn_p, conv_p = _group(x_prompt, pos_p, zeros(RET_HEADS, RET_DK, RET_DV),
                                      zeros(GDN_HEADS, GDN_DK, GDN_DV), zeros(CONV_W - 1, CONV_CH), wts)
    ys, ret_s, gdn_s, conv_s = _group(x_sample, pos_s, state_ret[0], state_gdn[0], state_conv[0], wts)
    return (yp, ys, ret_p[None], gdn_p[None], conv_p[None], ret_s[None], gdn_s[None], conv_s[None])
```

```python
import functools

import jax
import jax.numpy as jnp
from jax import lax
from jax.experimental import pallas as pl
from jax.experimental.pallas import tpu as pltpu

F32 = jnp.float32
BF16 = jnp.bfloat16

D_MODEL = 1024
PAST_LEN = 16384
RET_HEADS, RET_DK, RET_DV = 4, 128, 256
RET_QK = RET_HEADS * RET_DK
RET_V = RET_HEADS * RET_DV
GDN_HEADS, GDN_DK, GDN_DV = 8, 128, 128
GDN_QK = GDN_HEADS * GDN_DK
GDN_V = GDN_HEADS * GDN_DV
CONV_W = 4
CONV_CH = 2 * GDN_QK + GDN_V
D_FF = 4 * D_MODEL
CHUNK = 64
ROPE_BASE = 10000.0
EPS = 1e-6

SUBLANES = 8
LANES = 128
MXU_COLS = 256
STEP_ROWS = 64
EPILOGUE_ROWS = 64
VMEM_LIMIT = 56 * 1024 * 1024

QKV_W = 2 * RET_QK + RET_V
GATE_W = 4 * D_MODEL
AB_W = 2 * LANES


def _params(sem):
    return pltpu.CompilerParams(dimension_semantics=sem, vmem_limit_bytes=VMEM_LIMIT)


def _const_spec(shape):
    zeros = (0,) * len(shape)
    return pl.BlockSpec(shape, lambda *_: zeros, pipeline_mode=pl.Buffered(1))


def _dot(a, b):
    return jnp.dot(a, b, preferred_element_type=F32)


def _dot_nt(a, b):
    return lax.dot_general(a, b, (((1,), (1,)), ((), ())), preferred_element_type=F32)


def _dot_tn(a, b):
    return lax.dot_general(a, b, (((0,), (0,)), ((), ())), preferred_element_type=F32)


def _bf(x):
    return x.astype(BF16)


def _rows(parts):
    return jnp.concatenate(parts, axis=0)


def _rms(x, g):
    return x * lax.rsqrt(jnp.mean(x * x, axis=-1, keepdims=True) + EPS) * g


def _silu(x):
    return x * jax.nn.sigmoid(x)


def _in_proj_kernel(x_ref, ln_ref, wqkv_ref, wconv_ref, wab_ref, cos_ref, sin_ref, cw_ref, hist_ref,
                    alog_ref, dtb_ref, qk_ref, v_ref, conv_ref, gb_ref, tail_ref, h_ref, carry_ref,
                    *, seq_is_group, tiles_per_seq):
    i = pl.program_id(0)
    tm = x_ref.shape[0]
    row_blocks = [slice(b, b + EPILOGUE_ROWS) for b in range(0, tm, EPILOGUE_ROWS)]

    @pl.when(i == 0)
    def _():
        carry_ref[...] = jnp.zeros_like(carry_ref)

    h_ref[...] = _bf(_rms(x_ref[...], ln_ref[...]))
    ab = _dot(h_ref[...], wab_ref[...])
    gb_ref[:, :LANES] = -jnp.exp(alog_ref[...]) * jax.nn.softplus(ab[:, :LANES] + dtb_ref[...])
    gb_ref[:, LANES:] = jax.nn.sigmoid(ab[:, LANES:])

    def rope_slab(s):
        sl = slice(s * MXU_COLS, (s + 1) * MXU_COLS)
        r = _dot(h_ref[...], wqkv_ref[:, sl])
        for rows in row_blocks:
            for part in range(MXU_COLS // RET_DK):
                x = r[rows, part * RET_DK:(part + 1) * RET_DK]
                y = x * cos_ref[rows, :] + pltpu.roll(x, RET_DK // 2, 1) * sin_ref[rows, :]
                start = sl.start + part * RET_DK
                if start >= RET_QK:
                    y = y * (RET_DK ** -0.5)
                qk_ref[rows, start:start + RET_DK] = _bf(y)

    def v_slab(s):
        sl = slice(s * MXU_COLS, (s + 1) * MXU_COLS)
        v_ref[:, sl] = _bf(_dot(h_ref[...], wqkv_ref[:, 2 * RET_QK + sl.start:2 * RET_QK + sl.stop]))

    sub = lax.broadcasted_iota(jnp.int32, (1, SUBLANES, MXU_COLS), 1)

    def conv_slab(s):
        sl = slice(s * MXU_COLS, (s + 1) * MXU_COLS)
        w = [cw_ref[tap:tap + 1, sl][None] for tap in range(CONV_W)]
        r = _dot(h_ref[...], wconv_ref[:, sl])
        if seq_is_group:
            tail_ref[:, sl] = r
        else:
            tail_ref[:, :, sl] = r[tm - SUBLANES:][None]
            x_last = jnp.where(i % tiles_per_seq == 0, hist_ref[:, :, sl], carry_ref[:, sl][None])
            carry_ref[:, sl] = r[tm - SUBLANES:]
            x_last1 = pltpu.roll(x_last, 1, 1)
            pair_last2 = pltpu.roll(w[1] * x_last + w[0] * x_last1, 2, 1)
        for rows in row_blocks:
            g = (rows.stop - rows.start) // SUBLANES
            r3 = r[rows].reshape(g, SUBLANES, MXU_COLS)
            cur1 = pltpu.roll(r3, 1, 1)
            if seq_is_group:
                hist3 = hist_ref[rows, sl].reshape(g, SUBLANES, MXU_COLS)
                prev1 = pltpu.roll(hist3, 1, 1)
                prev2 = pltpu.roll(w[1] * hist3 + w[0] * prev1, 2, 1)
            else:
                prev1 = jnp.concatenate([x_last1, cur1[:g - 1]], axis=0)
            x1 = jnp.where(sub < 1, prev1, cur1)
            pair = w[1] * r3 + w[0] * x1
            cur2 = pltpu.roll(pair, 2, 1)
            if not seq_is_group:
                prev2 = jnp.concatenate([pair_last2, cur2[:g - 1]], axis=0)
                x_last1, pair_last2 = cur1[g - 1:], cur2[g - 1:]
            y = (w[3] * r3 + w[2] * x1) + jnp.where(sub < 2, prev2, cur2)
            conv_ref[rows, sl] = _silu(y).reshape(g * SUBLANES, MXU_COLS)

    light = ([functools.partial(rope_slab, s) for s in range(2 * RET_QK // MXU_COLS)]
             + [functools.partial(v_slab, s) for s in range(RET_V // MXU_COLS)])
    for s in range(CONV_CH // MXU_COLS):
        conv_slab(s)
        if s < len(light):
            light[s]()


def _in_proj(x2d, t, pos, conv0, wts):
    ln1, w_qkv, w_conv, w_ab, conv_w, a_log, dt_bias = wts
    n = x2d.shape[0]
    nb = n // t
    seq_is_group = t == SUBLANES
    tm = min(256, n) if seq_is_group else min(512, t)
    assert n % tm == 0 and (seq_is_group or t % tm == 0), (n, t, tm)
    tiles_per_seq = max(t // tm, 1)
    inv = ROPE_BASE ** (-jnp.arange(0, RET_DK, 2, dtype=F32) / RET_DK)
    ang = jnp.tile(pos, max(tm // t, 1))[:, None] * inv[None, :]
    cosf = jnp.concatenate([jnp.cos(ang), jnp.cos(ang)], axis=-1)
    sinf = jnp.concatenate([-jnp.sin(ang), jnp.sin(ang)], axis=-1)
    pos_blocks = cosf.shape[0] // tm
    hist = jnp.pad(conv0, ((0, 0), (SUBLANES - (CONV_W - 1), 0), (0, 0)))
    rows = lambda cols: pl.BlockSpec((tm, cols), lambda i: (i, 0))
    if seq_is_group:
        hist = hist.reshape(n, CONV_CH)
        hist_spec = rows(CONV_CH)
        tail_spec = hist_spec
        tail_shape = jax.ShapeDtypeStruct((n, CONV_CH), F32)
    else:
        hist_spec = pl.BlockSpec((1, SUBLANES, CONV_CH), lambda i: (i // tiles_per_seq, 0, 0))
        tail_spec = pl.BlockSpec((1, SUBLANES, CONV_CH), lambda i: (i, 0, 0))
        tail_shape = jax.ShapeDtypeStruct((n // tm, SUBLANES, CONV_CH), F32)
    pad_lanes = lambda x: jnp.pad(x.reshape(1, GDN_HEADS), ((0, 0), (0, LANES - GDN_HEADS)))
    pos_spec = pl.BlockSpec((tm, RET_DK), lambda i: (i % pos_blocks, 0))
    qk, v, conv, gb, tail = pl.pallas_call(
        functools.partial(_in_proj_kernel, seq_is_group=seq_is_group, tiles_per_seq=tiles_per_seq),
        grid=(n // tm,),
        in_specs=[
            rows(D_MODEL),
            _const_spec((1, D_MODEL)),
            _const_spec((D_MODEL, QKV_W)),
            _const_spec((D_MODEL, CONV_CH)),
            _const_spec((D_MODEL, AB_W)),
            pos_spec,
            pos_spec,
            _const_spec((CONV_W, CONV_CH)),
            hist_spec,
            _const_spec((1, LANES)),
            _const_spec((1, LANES)),
        ],
        out_specs=[rows(2 * RET_QK), rows(RET_V), rows(CONV_CH), rows(AB_W), tail_spec],
        out_shape=[
            jax.ShapeDtypeStruct((n, 2 * RET_QK), BF16),
            jax.ShapeDtypeStruct((n, RET_V), BF16),
            jax.ShapeDtypeStruct((n, CONV_CH), F32),
            jax.ShapeDtypeStruct((n, AB_W), F32),
            tail_shape,
        ],
        scratch_shapes=[pltpu.VMEM((tm, D_MODEL), BF16), pltpu.VMEM((SUBLANES, CONV_CH), F32)],
        compiler_params=_params(("arbitrary",)),
        name="in_proj",
    )(x2d, ln1, w_qkv, w_conv, w_ab, cosf, sinf, conv_w, hist, pad_lanes(a_log), pad_lanes(dt_bias))
    conv_new = tail.reshape(nb, -1, SUBLANES, CONV_CH)[:, -1, SUBLANES - (CONV_W - 1):, :]
    return qk, v, conv, gb, conv_new


def _step_layout(nb, t, c):
    ns = STEP_ROWS // c
    if ns == 1:
        par = max(p for p in (8, 4, 2, 1) if nb % p == 0)
        return nb, t, ns, par
    assert t == c and nb % ns == 0, (nb, t, c)
    return nb // ns, STEP_ROWS, ns, 1


def _retention_stages(q_ref, k_ref, v_ref, dintra_ref, dq_ref, dk_ref, dc_ref, o_ref, s_ref, *, par, ns, c):
    units = [(p, h) for p in range(par) for h in range(RET_HEADS)]
    ids = range(len(units))
    seqs = range(ns)
    rows = lambda x, j: x[j * c:(j + 1) * c]
    dk_sl = lambda h: slice(h * RET_DK, (h + 1) * RET_DK)
    dv_sl = lambda h: slice(h * RET_DV, (h + 1) * RET_DV)
    st = {}

    def scores():
        st["qb"] = [q_ref[p, :, dk_sl(h)] for p, h in units]
        kb = [k_ref[p, :, dk_sl(h)] for p, h in units]
        st["vb"] = [v_ref[p, :, dv_sl(h)] for p, h in units]
        st["scores"] = [_dot_nt(st["qb"][u], kb[u]) * dintra_ref[units[u][1]] for u in ids]
        st["kd"] = [kb[u].astype(F32) * dk_ref[units[u][1]] for u in ids]

    def state_read():
        qb = st["qb"]
        if ns == 1:
            st["s"] = [s_ref[p, h] for p, h in units]
            st["qs"] = [_dot(qb[u], _bf(st["s"][u])) for u in ids]
        else:
            q32 = [qb[u].astype(F32) for u in ids]
            st["s"] = [[s_ref[j, h] for j in seqs] for _, h in units]
            st["qs"] = [_rows([_dot(_bf(rows(q32[u], j)), _bf(st["s"][u][j])) for j in seqs]) for u in ids]

    def state_write():
        s_, qs, kd, vb, sc = st["s"], st["qs"], st["kd"], st["vb"], st["scores"]
        if ns == 1:
            upd = [_dot(_bf(_rows([kd[u].T, sc[u]])), vb[u]) for u in ids]
            for u, (p, h) in enumerate(units):
                o_ref[p, :, dv_sl(h)] = upd[u][RET_DK:] + qs[u] * dq_ref[h]
                s_ref[p, h] = s_[u] * dc_ref[h] + upd[u][:RET_DK]
        else:
            v32 = [vb[u].astype(F32) for u in ids]
            intra = [_dot(_bf(sc[u]), vb[u]) for u in ids]
            for u, (p, h) in enumerate(units):
                o_ref[p, :, dv_sl(h)] = intra[u] + qs[u] * dq_ref[h]
            ktv = [[_dot_tn(_bf(rows(kd[u], j)), _bf(rows(v32[u], j))) for j in seqs] for u in ids]
            for u, (p, h) in enumerate(units):
                for j in seqs:
                    s_ref[j, h] = s_[u][j] * dc_ref[h] + ktv[u][j]

    return [scores, state_read, state_write]


def _retention_consts(c):
    log_g = jnp.log1p(-jnp.exp2(-5.0 - jnp.arange(RET_HEADS, dtype=F32)))
    idx = jnp.arange(STEP_ROWS)
    same = (idx[:, None] // c) == (idx[None, :] // c)
    off = (idx % c).astype(F32)
    diff = off[:, None] - off[None, :]
    causal = same & (diff >= 0)
    d_intra = jnp.where(causal[None], jnp.exp(log_g[:, None, None] * jnp.where(causal, diff, 0.0)[None]), 0.0)
    d_q = jnp.broadcast_to(jnp.exp(log_g[:, None] * (off + 1.0)[None, :])[..., None], (RET_HEADS, STEP_ROWS, RET_DV))
    d_k = jnp.broadcast_to(jnp.exp(log_g[:, None] * (c - 1.0 - off)[None, :])[..., None],
                           (RET_HEADS, STEP_ROWS, RET_DK))
    d_c = jnp.broadcast_to(jnp.exp(log_g * c)[:, None, None], (RET_HEADS, 1, RET_DV))
    return d_intra, d_q, d_k, d_c


def _block_diag(x2):
    m = x2.shape[0]
    r = lax.broadcasted_iota(jnp.int32, (2 * m, 2 * m), 0)
    cc = lax.broadcasted_iota(jnp.int32, (2 * m, 2 * m), 1)
    return jnp.where((r < m) == (cc < m), _rows([x2, x2]), jnp.zeros((), x2.dtype))


def _solve_correction(a, c):
    ids = range(len(a))
    m = a[0].shape[0]
    n = [-x for x in a]
    ab = [_bf(x) for x in a]
    p = [_dot(ab[i], _block_diag(ab[i])) for i in ids]
    span = 2
    while span < c:
        pb = [_block_diag(_bf(x)) for x in p]
        last = 2 * span >= c
        if last:
            prod = [_dot(_bf(n[i]), pb[i]) for i in ids]
            n = [n[i] + p[i] + prod[i] for i in ids]
        else:
            prod = [_dot(_bf(_rows([n[i], p[i]])), pb[i]) for i in ids]
            n = [n[i] + p[i] + prod[i][:m] for i in ids]
            p = [prod[i][m:] for i in ids]
        span *= 2
    return n


def _recurrence_kernel(*refs, par, ns, c, zero_init):
    (qkv_ref, gb_ref, q_ref, k_ref, v_ref, dintra_ref, dq_ref, dk_ref, dc_ref), refs = refs[:9], refs[9:]
    init_refs, (o_ref, s_ref, oret_ref, sret_ref) = refs[:-4], refs[-4:]

    @pl.when(pl.program_id(1) == 0)
    def _():
        if zero_init:
            s_ref[...] = jnp.zeros_like(s_ref)
            sret_ref[...] = jnp.zeros_like(sret_ref)
        else:
            s_ref[...] = init_refs[0][...]
            sret_ref[...] = init_refs[1][...]

    retention = _retention_stages(q_ref, k_ref, v_ref, dintra_ref, dq_ref, dk_ref, dc_ref, oret_ref, sret_ref,
                                  par=par, ns=ns, c=c)
    m = STEP_ROWS

    def l2norm(x):
        return x * lax.rsqrt(jnp.sum(x * x, axis=-1, keepdims=True) + EPS)

    lane = lax.broadcasted_iota(jnp.int32, (m, LANES), 1)
    off = lax.broadcasted_iota(jnp.int32, (m, LANES), 0) % c
    r2 = lax.broadcasted_iota(jnp.int32, (m, 2 * m), 0)
    lane2 = lax.broadcasted_iota(jnp.int32, (m, 2 * m), 1)
    c2 = lane2 % m
    same = (r2 // c) == (c2 // c)
    tril = same & (r2 >= c2)
    strict = same & (r2 > c2)
    even = lane2 < m
    pick = (lax.broadcasted_iota(jnp.int32, (2 * m, 2 * GDN_DK), 0) < m) == (
        lax.broadcasted_iota(jnp.int32, (2 * m, 2 * GDN_DK), 1) < GDN_DK)

    def gates(p):
        cum = jnp.where(lane < GDN_HEADS, gb_ref[p, :, :LANES], 0.0)
        shift = 1
        while shift < c:
            cum = cum + jnp.where(off >= shift, pltpu.roll(cum, shift, 0), 0.0)
            shift *= 2
        last = cum.reshape(ns, c, LANES)[:, c - 1:c, :]
        e_rest = jnp.exp((last - cum.reshape(ns, c, LANES)).reshape(m, LANES))
        return dict(cum=cum, cum_t=cum.T, e_cum=jnp.exp(cum), e_rest=e_rest, e_last=jnp.exp(last),
                    beta=gb_ref[p, :, LANES:])

    def padded(x, h):
        z = jnp.zeros_like(x)
        return _rows([x, z]) if h % 2 == 0 else _rows([z, x])

    gt = [gates(p) for p in range(par)]
    units = [(p, h) for p in range(par) for h in range(GDN_HEADS)]
    ids = range(len(units))
    pairs = range(len(units) // 2)
    pair_of = lambda u: u // 2
    both = lambda xs, i: jnp.concatenate([xs[2 * i], xs[2 * i + 1]], axis=-1)
    col = lambda x, h: x[:, h:h + 1]
    q = [l2norm(qkv_ref[p, :, h * GDN_DK:(h + 1) * GDN_DK]) * (GDN_DK ** -0.5) for p, h in units]
    k = [l2norm(qkv_ref[p, :, GDN_QK + h * GDN_DK:GDN_QK + (h + 1) * GDN_DK]) for p, h in units]
    v = [qkv_ref[p, :, 2 * GDN_QK + h * GDN_DV:2 * GDN_QK + (h + 1) * GDN_DV] for p, h in units]
    beta = [col(gt[p]["beta"], h) for p, h in units]
    eg = [col(gt[p]["e_cum"], h) for p, h in units]
    cum_col = [col(gt[p]["cum"], h) for p, h in units]
    cum_row = [gt[p]["cum_t"][h:h + 1, :] for p, h in units]
    decay = [jnp.exp(jnp.where(tril, jnp.where(even, cum_col[2 * i], cum_col[2 * i + 1]) - both(cum_row, i), -jnp.inf))
             for i in pairs]
    kb = [k[u] * beta[u] for u in ids]
    kpick = [jnp.where(pick, _rows([_bf(both(k, i))] * 2), jnp.zeros((), BF16)) for i in pairs]
    raw = [_dot_nt(_bf(_rows([both(kb, i), both(q, i)])), kpick[i]) for i in pairs]
    a = [jnp.where(strict, raw[i][:m] * decay[i], 0.0) for i in pairs]
    attn = [raw[i][m:] * decay[i] for i in pairs]
    retention[0]()
    n = _solve_correction(a, c)
    retention[1]()
    rhs = [jnp.concatenate([v[u] * beta[u], kb[u] * eg[u]], axis=-1) for u in ids]
    nb16 = [_bf(x) for x in n]
    sol = [rhs[u] + _dot(nb16[pair_of(u)], padded(_bf(rhs[u]), units[u][1])) for u in ids]
    uu = [sol[u][:, :GDN_DV] for u in ids]
    w = [sol[u][:, GDN_DV:] for u in ids]
    qe = [q[u] * eg[u] for u in ids]
    kd = [k[u] * col(gt[p]["e_rest"], h) for u, (p, h) in enumerate(units)]
    retention[2]()
    dv_sl = lambda h: slice(h * GDN_DV, (h + 1) * GDN_DV)
    if ns == 1:
        s = [s_ref[p, h] for p, h in units]
        ws_qs = [_dot(_bf(_rows([w[u], qe[u]])), _bf(s[u])) for u in ids]
        v_new = [uu[u] - ws_qs[u][:m] for u in ids]
        upd = [_dot(_bf(_rows([padded(kd[u], h).T, attn[pair_of(u)]])), padded(_bf(v_new[u]), h))
               for u, (p, h) in enumerate(units)]
        for u, (p, h) in enumerate(units):
            o_ref[p, :, dv_sl(h)] = ws_qs[u][m:] + upd[u][GDN_DK:]
            s_ref[p, h] = s[u] * gt[p]["e_last"][0, :, h:h + 1] + upd[u][:GDN_DK]
    else:
        seqs = range(ns)
        rows = lambda x, j: x[j * c:(j + 1) * c]
        s = [[s_ref[j, h] for j in seqs] for _, h in units]
        ws_qs = [[_dot(_bf(_rows([rows(w[u], j), rows(qe[u], j)])), _bf(s[u][j])) for j in seqs] for u in ids]
        v_new = [_rows([rows(uu[u], j) - ws_qs[u][j][:c] for j in seqs]) for u in ids]
        qs = [_rows([ws_qs[u][j][c:] for j in seqs]) for u in ids]
        intra = [_dot(_bf(attn[pair_of(u)]), padded(_bf(v_new[u]), units[u][1])) for u in ids]
        for u, (p, h) in enumerate(units):
            o_ref[p, :, dv_sl(h)] = qs[u] + intra[u]
        ktv = [[_dot_tn(_bf(rows(kd[u], j)), _bf(rows(v_new[u], j))) for j in seqs] for u in ids]
        for u, (p, h) in enumerate(units):
            for j in seqs:
                s_ref[j, h] = s[u][j] * gt[p]["e_last"][j, :, h:h + 1] + ktv[u][j]


def _recurrences(qk, v, conv, gb, s_ret, s_gdn, nb, t, c):
    groups, rows, ns, par = _step_layout(nb, t, c)
    zero_init = s_ret is None
    blk = lambda cols, col_blk=0: pl.BlockSpec((par, STEP_ROWS, cols), lambda g, i: (g, i, col_blk))
    ret_state = pl.BlockSpec((par * ns, RET_HEADS, RET_DK, RET_DV), lambda g, i: (g, 0, 0, 0))
    gdn_state = pl.BlockSpec((par * ns, GDN_HEADS, GDN_DK, GDN_DV), lambda g, i: (g, 0, 0, 0))
    view = lambda x: x.reshape(groups, rows, x.shape[-1])
    qk3 = view(qk)
    o_g, gdn_new, o_r, ret_new = pl.pallas_call(
        functools.partial(_recurrence_kernel, par=par, ns=ns, c=c, zero_init=zero_init),
        grid=(groups // par, rows // STEP_ROWS),
        in_specs=[
            blk(CONV_CH), blk(AB_W), blk(RET_QK, 0), blk(RET_QK, 1), blk(RET_V),
            _const_spec((RET_HEADS, STEP_ROWS, STEP_ROWS)),
            _const_spec((RET_HEADS, STEP_ROWS, RET_DV)),
            _const_spec((RET_HEADS, STEP_ROWS, RET_DK)),
            _const_spec((RET_HEADS, 1, RET_DV)),
        ] + ([] if zero_init else [gdn_state, ret_state]),
        out_specs=[blk(GDN_V), gdn_state, blk(RET_V), ret_state],
        out_shape=[
            jax.ShapeDtypeStruct((groups, rows, GDN_V), F32),
            jax.ShapeDtypeStruct((nb, GDN_HEADS, GDN_DK, GDN_DV), F32),
            jax.ShapeDtypeStruct((groups, rows, RET_V), F32),
            jax.ShapeDtypeStruct((nb, RET_HEADS, RET_DK, RET_DV), F32),
        ],
        compiler_params=_params(("arbitrary", "arbitrary")),
        name="recurrences",
    )(view(conv), view(gb), qk3, qk3, view(v), *_retention_consts(c), *(() if zero_init else (s_gdn, s_ret)))
    return o_r.reshape(nb * t, RET_V), o_g.reshape(nb * t, GDN_V), ret_new, gdn_new


def _attn_out_kernel(x_ref, or_ref, og_ref, ln1_ref, wgate_ref, rn_ref, gn_ref, wr_ref, wg_ref, wo_ref, o_ref,
                     h_ref, lhs_ref, br_ref):
    tm = x_ref.shape[0]
    h_ref[...] = _bf(_rms(x_ref[...], ln1_ref[...]))
    slabs = [slice(s * MXU_COLS, (s + 1) * MXU_COLS) for s in range(D_MODEL // MXU_COLS)]
    row_blocks = [slice(b, b + EPILOGUE_ROWS) for b in range(0, tm, EPILOGUE_ROWS)]

    def gate(block, sl):
        return _dot(h_ref[...], wgate_ref[:, block * D_MODEL + sl.start:block * D_MODEL + sl.stop])

    def branch(src_ref, head_dim, gain_ref, block, w_ref):
        for sl in slabs:
            g = gate(block, sl)
            for rows in row_blocks:
                o = src_ref[rows, sl]
                heads = [o[:, k:k + head_dim] for k in range(0, MXU_COLS, head_dim)]
                normed = [oh * lax.rsqrt(jnp.mean(oh * oh, axis=-1, keepdims=True) + EPS) for oh in heads]
                lhs_ref[rows, sl] = _bf(jnp.concatenate(normed, axis=-1) * gain_ref[:, sl] * _silu(g[rows]))
        return _dot(lhs_ref[...], w_ref[...])

    br_ref[0] = branch(or_ref, RET_DV, rn_ref, 0, wr_ref)
    br_ref[1] = branch(og_ref, GDN_DV, gn_ref, 1, wg_ref)
    for sl in slabs:
        g_r = gate(2, sl)
        g_g = gate(3, sl)
        for rows in row_blocks:
            lhs_ref[rows, sl] = _bf(jax.nn.sigmoid(g_r[rows]) * br_ref[0, rows, sl]
                                    + jax.nn.sigmoid(g_g[rows]) * br_ref[1, rows, sl])
    o_ref[...] = x_ref[...] + _dot(lhs_ref[...], wo_ref[...])


def _attn_out(x2d, o_r, o_g, ln1, w_gate, ret_norm, gdn_norm, w_ret_br, w_gdn_br, w_out):
    n = x2d.shape[0]
    tm = min(512, n)
    rows = pl.BlockSpec((tm, D_MODEL), lambda i: (i, 0))
    vec = _const_spec((1, D_MODEL))
    wspec = _const_spec((D_MODEL, D_MODEL))
    return pl.pallas_call(
        _attn_out_kernel,
        grid=(n // tm,),
        in_specs=[rows, rows, rows, vec, _const_spec((D_MODEL, GATE_W)), vec, vec, wspec, wspec, wspec],
        out_specs=rows,
        out_shape=jax.ShapeDtypeStruct((n, D_MODEL), F32),
        scratch_shapes=[pltpu.VMEM((tm, D_MODEL), BF16), pltpu.VMEM((tm, D_MODEL), BF16),
                        pltpu.VMEM((2, tm, D_MODEL), F32)],
        compiler_params=_params(("arbitrary",)),
        name="attn_out",
    )(x2d, o_r, o_g, ln1, w_gate, ret_norm, gdn_norm, w_ret_br, w_gdn_br, w_out)


FF_BLOCK = 1024


def _mlp_kernel(x_ref, ln2_ref, wu_ref, wd_ref, lnf_ref, o_ref):
    x = x_ref[...]
    hb = _bf(_rms(x, ln2_ref[...]))
    acc = x
    for f in range(D_FF // FF_BLOCK):
        sl = slice(f * FF_BLOCK, (f + 1) * FF_BLOCK)
        up = jnp.maximum(_dot(hb, wu_ref[:, sl]), 0.0)
        acc = acc + _dot(_bf(up * up), wd_ref[sl, :])
    o_ref[...] = _rms(acc, lnf_ref[...])


def _mlp(x2d, ln2, w_up, w_down, ln_f):
    n = x2d.shape[0]
    tm = min(1024, n)
    rows = pl.BlockSpec((tm, D_MODEL), lambda i: (i, 0))
    return pl.pallas_call(
        _mlp_kernel,
        grid=(n // tm,),
        in_specs=[rows, _const_spec((1, D_MODEL)), _const_spec((D_MODEL, D_FF)), _const_spec((D_FF, D_MODEL)),
                  _const_spec((1, D_MODEL))],
        out_specs=rows,
        out_shape=jax.ShapeDtypeStruct((n, D_MODEL), F32),
        compiler_params=_params(("arbitrary",)),
        name="mlp",
    )(x2d, ln2, w_up, w_down, ln_f)


def _group(x, pos, s_ret, s_gdn, s_conv, wts):
    (in_wts, w_gate, ret_norm, gdn_norm, w_ret_br, w_gdn_br, w_out, ln2, w_up, w_down, ln_f) = wts
    nb, t, _ = x.shape
    c = CHUNK if t % CHUNK == 0 else t
    assert STEP_ROWS % c == 0 and nb % (STEP_ROWS // c) == 0, (nb, t)
    x2d = x.reshape(nb * t, D_MODEL)
    qk, v, conv, gb, conv_new = _in_proj(x2d, t, pos, s_conv, in_wts)
    o_r, o_g, ret_new, gdn_new = _recurrences(qk, v, conv, gb, s_ret, s_gdn, nb, t, c)
    x1 = _attn_out(x2d, o_r, o_g, in_wts[0], w_gate, ret_norm, gdn_norm, w_ret_br, w_gdn_br, w_out)
    y = _mlp(x1, ln2, w_up, w_down, ln_f)
    return y.reshape(x.shape), ret_new, gdn_new, conv_new


def kernel(x_prompt, x_sample, state_ret, state_gdn, state_conv, ln1, w_in, conv_w, a_log, dt_bias, ret_norm,
           gdn_norm, w_ret_br, w_gdn_br, w_out, ln2, w_up, w_down, ln_f):
    depth = w_in.shape[0]
    assert depth == 1, "single-layer trunk"
    bp, tp, _ = x_prompt.shape
    ts = x_sample.shape[1]
    w = w_in[0]
    o_gr = 2 * RET_QK + RET_V
    o_qkv = o_gr + RET_V
    o_a = o_qkv + CONV_CH
    o_z = o_a + 2 * GDN_HEADS
    w_qkv = w[:, :o_gr].astype(BF16)
    w_conv = w[:, o_qkv:o_a].astype(BF16)
    w_gate = jnp.concatenate([w[:, o_gr:o_qkv], w[:, o_z:]], axis=1).astype(BF16)
    w_ab = jnp.zeros((D_MODEL, AB_W), F32)
    w_ab = w_ab.at[:, :GDN_HEADS].set(w[:, o_a:o_a + GDN_HEADS])
    w_ab = w_ab.at[:, LANES:LANES + GDN_HEADS].set(w[:, o_a + GDN_HEADS:o_z]).astype(BF16)
    vec = lambda v: v.reshape(1, -1)
    in_wts = (vec(ln1[0]), w_qkv, w_conv, w_ab, conv_w[0], a_log[0], dt_bias[0])
    wts = (in_wts, w_gate, vec(ret_norm[0]), vec(gdn_norm[0]), w_ret_br[0].astype(BF16), w_gdn_br[0].astype(BF16),
           w_out[0].astype(BF16), vec(ln2[0]), w_up[0].astype(BF16), w_down[0].astype(BF16), vec(ln_f))
    pos_p = jnp.arange(tp, dtype=F32)
    pos_s = PAST_LEN + jnp.arange(ts, dtype=F32)
    yp, ret_p, gdn_p, conv_p = _group(x_prompt, pos_p, None, None, jnp.zeros((bp, CONV_W - 1, CONV_CH), F32), wts)
    ys, ret_s, gdn_s, conv_s = _group(x_sample, pos_s, state_ret[0], state_gdn[0], state_conv[0], wts)
    return (yp, ys, ret_p[None], gdn_p[None], conv_p[None], ret_s[None], gdn_s[None], conv_s[None])
```

```python
import functools

import jax
import jax.numpy as jnp
from jax import lax
from jax.experimental import pallas as pl
from jax.experimental.pallas import tpu as pltpu

F32 = jnp.float32
BF16 = jnp.bfloat16

D_MODEL = 1024
PAST_LEN = 16384
RET_HEADS, RET_DK, RET_DV = 4, 128, 256
RET_QK = RET_HEADS * RET_DK
RET_V = RET_HEADS * RET_DV
GDN_HEADS, GDN_DK, GDN_DV = 8, 128, 128
GDN_QK = GDN_HEADS * GDN_DK
GDN_V = GDN_HEADS * GDN_DV
CONV_W = 4
CONV_CH = 2 * GDN_QK + GDN_V
D_FF = 4 * D_MODEL
CHUNK = 64
ROPE_BASE = 10000.0
EPS = 1e-6

SUBLANES = 8
LANES = 128
MXU_COLS = 256
STEP_ROWS = 64
EPILOGUE_ROWS = 64
VMEM_LIMIT = 56 * 1024 * 1024

QKV_W = 2 * RET_QK + RET_V
GATE_W = 4 * D_MODEL
AB_W = 2 * LANES


def _params(sem):
    return pltpu.CompilerParams(dimension_semantics=sem, vmem_limit_bytes=VMEM_LIMIT)


def _const_spec(shape):
    zeros = (0,) * len(shape)
    return pl.BlockSpec(shape, lambda *_: zeros, pipeline_mode=pl.Buffered(1))


def _dot(a, b):
    return jnp.dot(a, b, preferred_element_type=F32)


def _dot_nt(a, b):
    return lax.dot_general(a, b, (((1,), (1,)), ((), ())), preferred_element_type=F32)


def _dot_tn(a, b):
    return lax.dot_general(a, b, (((0,), (0,)), ((), ())), preferred_element_type=F32)


def _bf(x):
    return x.astype(BF16)


def _rows(parts):
    return jnp.concatenate(parts, axis=0)


def _rms(x, g):
    return x * lax.rsqrt(jnp.mean(x * x, axis=-1, keepdims=True) + EPS) * g


def _silu(x):
    return x * jax.nn.sigmoid(x)


def _in_proj_kernel(x_ref, ln_ref, wqkv_ref, wconv_ref, wab_ref, cos_ref, sin_ref, cw_ref, hist_ref,
                    alog_ref, dtb_ref, qk_ref, v_ref, conv_ref, gb_ref, tail_ref, h_ref, carry_ref,
                    *, seq_is_group, tiles_per_seq):
    i = pl.program_id(0)
    tm = x_ref.shape[0]
    row_blocks = [slice(b, b + EPILOGUE_ROWS) for b in range(0, tm, EPILOGUE_ROWS)]

    @pl.when(i == 0)
    def _():
        carry_ref[...] = jnp.zeros_like(carry_ref)

    h_ref[...] = _bf(_rms(x_ref[...], ln_ref[...]))
    ab = _dot(h_ref[...], wab_ref[...])
    gb_ref[:, :LANES] = -jnp.exp(alog_ref[...]) * jax.nn.softplus(ab[:, :LANES] + dtb_ref[...])
    gb_ref[:, LANES:] = jax.nn.sigmoid(ab[:, LANES:])

    def rope_slab(s):
        sl = slice(s * MXU_COLS, (s + 1) * MXU_COLS)
        r = _dot(h_ref[...], wqkv_ref[:, sl])
        for rows in row_blocks:
            for part in range(MXU_COLS // RET_DK):
                x = r[rows, part * RET_DK:(part + 1) * RET_DK]
                y = x * cos_ref[rows, :] + pltpu.roll(x, RET_DK // 2, 1) * sin_ref[rows, :]
                start = sl.start + part * RET_DK
                if start >= RET_QK:
                    y = y * (RET_DK ** -0.5)
                qk_ref[rows, start:start + RET_DK] = _bf(y)

    def v_slab(s):
        sl = slice(s * MXU_COLS, (s + 1) * MXU_COLS)
        v_ref[:, sl] = _bf(_dot(h_ref[...], wqkv_ref[:, 2 * RET_QK + sl.start:2 * RET_QK + sl.stop]))

    sub = lax.broadcasted_iota(jnp.int32, (1, SUBLANES, MXU_COLS), 1)

    def conv_slab(s):
        sl = slice(s * MXU_COLS, (s + 1) * MXU_COLS)
        w = [cw_ref[tap:tap + 1, sl][None] for tap in range(CONV_W)]
        r = _dot(h_ref[...], wconv_ref[:, sl])
        if not seq_is_group:
            tail_ref[:, :, sl] = r[tm - SUBLANES:][None]
            x_last = jnp.where(i % tiles_per_seq == 0, hist_ref[:, :, sl], carry_ref[:, sl][None])
            carry_ref[:, sl] = r[tm - SUBLANES:]
            x_last1 = pltpu.roll(x_last, 1, 1)
            pair_last2 = pltpu.roll(w[1] * x_last + w[0] * x_last1, 2, 1)
        for rows in row_blocks:
            g = (rows.stop - rows.start) // SUBLANES
            r3 = r[rows].reshape(g, SUBLANES, MXU_COLS)
            cur1 = pltpu.roll(r3, 1, 1)
            if seq_is_group:
                seqs = slice(rows.start // SUBLANES, rows.stop // SUBLANES)
                x3, x2, prev1 = (hist_ref[seqs, j:j + 1, sl] for j in range(CONV_W - 1))
                prev2 = jnp.where(sub < 1, w[1] * x2 + w[0] * x3, w[1] * prev1 + w[0] * x2)
                tail_ref[seqs, :, sl] = r3[:, SUBLANES - (CONV_W - 1):, :]
            else:
                prev1 = jnp.concatenate([x_last1, cur1[:g - 1]], axis=0)
            x1 = jnp.where(sub < 1, prev1, cur1)
            pair = w[1] * r3 + w[0] * x1
            cur2 = pltpu.roll(pair, 2, 1)
            if not seq_is_group:
                prev2 = jnp.concatenate([pair_last2, cur2[:g - 1]], axis=0)
                x_last1, pair_last2 = cur1[g - 1:], cur2[g - 1:]
            y = (w[3] * r3 + w[2] * x1) + jnp.where(sub < 2, prev2, cur2)
            conv_ref[rows, sl] = _silu(y).reshape(g * SUBLANES, MXU_COLS)

    light = ([functools.partial(rope_slab, s) for s in range(2 * RET_QK // MXU_COLS)]
             + [functools.partial(v_slab, s) for s in range(RET_V // MXU_COLS)])
    for s in range(CONV_CH // MXU_COLS):
        conv_slab(s)
        if s < len(light):
            light[s]()


def _w_in_block(width, index):
    assert QKV_W + D_MODEL == CONV_CH and QKV_W == 2 * D_MODEL
    return pl.BlockSpec((D_MODEL, width), lambda i: (0, index), pipeline_mode=pl.Buffered(1))


def _in_proj(x2d, t, pos, conv0, wts):
    ln1, w_bf, w_ab, conv_w, a_log, dt_bias = wts
    n = x2d.shape[0]
    nb = n // t
    seq_is_group = t == SUBLANES
    tm = min(256, n) if seq_is_group else min(512, t)
    assert n % tm == 0 and (seq_is_group or t % tm == 0), (n, t, tm)
    tiles_per_seq = max(t // tm, 1)
    inv = ROPE_BASE ** (-jnp.arange(0, RET_DK, 2, dtype=F32) / RET_DK)
    ang = jnp.tile(pos, max(tm // t, 1))[:, None] * inv[None, :]
    cosf = jnp.concatenate([jnp.cos(ang), jnp.cos(ang)], axis=-1)
    sinf = jnp.concatenate([-jnp.sin(ang), jnp.sin(ang)], axis=-1)
    pos_blocks = cosf.shape[0] // tm
    rows = lambda cols: pl.BlockSpec((tm, cols), lambda i: (i, 0))
    if seq_is_group:
        hist = conv0
        hist_spec = pl.BlockSpec((tm // SUBLANES, CONV_W - 1, CONV_CH), lambda i: (i, 0, 0))
        tail_spec = hist_spec
        tail_shape = jax.ShapeDtypeStruct(conv0.shape, F32)
    else:
        hist = jnp.pad(conv0, ((0, 0), (SUBLANES - (CONV_W - 1), 0), (0, 0)))
        hist_spec = pl.BlockSpec((1, SUBLANES, CONV_CH), lambda i: (i // tiles_per_seq, 0, 0))
        tail_spec = pl.BlockSpec((1, SUBLANES, CONV_CH), lambda i: (i, 0, 0))
        tail_shape = jax.ShapeDtypeStruct((n // tm, SUBLANES, CONV_CH), F32)
    pad_lanes = lambda x: jnp.pad(x.reshape(1, GDN_HEADS), ((0, 0), (0, LANES - GDN_HEADS)))
    pos_spec = pl.BlockSpec((tm, RET_DK), lambda i: (i % pos_blocks, 0))
    qk, v, conv, gb, tail = pl.pallas_call(
        functools.partial(_in_proj_kernel, seq_is_group=seq_is_group, tiles_per_seq=tiles_per_seq),
        grid=(n // tm,),
        in_specs=[
            rows(D_MODEL),
            _const_spec((1, D_MODEL)),
            _w_in_block(QKV_W, 0),
            _w_in_block(CONV_CH, 1),
            _const_spec((D_MODEL, AB_W)),
            pos_spec,
            pos_spec,
            _const_spec((CONV_W, CONV_CH)),
            hist_spec,
            _const_spec((1, LANES)),
            _const_spec((1, LANES)),
        ],
        out_specs=[rows(2 * RET_QK), rows(RET_V), rows(CONV_CH), rows(AB_W), tail_spec],
        out_shape=[
            jax.ShapeDtypeStruct((n, 2 * RET_QK), BF16),
            jax.ShapeDtypeStruct((n, RET_V), BF16),
            jax.ShapeDtypeStruct((n, CONV_CH), F32),
            jax.ShapeDtypeStruct((n, AB_W), F32),
            tail_shape,
        ],
        scratch_shapes=[pltpu.VMEM((tm, D_MODEL), BF16), pltpu.VMEM((SUBLANES, CONV_CH), F32)],
        compiler_params=_params(("arbitrary",)),
        name="in_proj",
    )(x2d, ln1, w_bf, w_bf, w_ab, cosf, sinf, conv_w, hist, pad_lanes(a_log), pad_lanes(dt_bias))
    if not seq_is_group:
        tail = tail.reshape(nb, tiles_per_seq, SUBLANES, CONV_CH)[:, -1, SUBLANES - (CONV_W - 1):, :]
    return qk, v, conv, gb, tail


def _step_layout(nb, t, c):
    ns = STEP_ROWS // c
    if ns == 1:
        par = max(p for p in (8, 4, 2, 1) if nb % p == 0)
        return nb, t, ns, par
    assert t == c and nb % ns == 0, (nb, t, c)
    return nb // ns, STEP_ROWS, ns, 1


def _retention_stages(q_ref, k_ref, v_ref, dintra_ref, dq_ref, dk_ref, dc_ref, o_ref, s_ref, *, par, ns, c):
    units = [(p, h) for p in range(par) for h in range(RET_HEADS)]
    ids = range(len(units))
    seqs = range(ns)
    rows = lambda x, j: x[j * c:(j + 1) * c]
    dk_sl = lambda h: slice(h * RET_DK, (h + 1) * RET_DK)
    dv_sl = lambda h: slice(h * RET_DV, (h + 1) * RET_DV)
    st = {}

    def scores():
        st["qb"] = [q_ref[p, :, dk_sl(h)] for p, h in units]
        kb = [k_ref[p, :, dk_sl(h)] for p, h in units]
        st["vb"] = [v_ref[p, :, dv_sl(h)] for p, h in units]
        st["scores"] = [_dot_nt(st["qb"][u], kb[u]) * dintra_ref[units[u][1]] for u in ids]
        st["kd"] = [kb[u].astype(F32) * dk_ref[units[u][1]] for u in ids]

    def state_read():
        qb = st["qb"]
        if ns == 1:
            st["s"] = [s_ref[p, h] for p, h in units]
            st["qs"] = [_dot(qb[u], _bf(st["s"][u])) for u in ids]
        else:
            q32 = [qb[u].astype(F32) for u in ids]
            st["s"] = [[s_ref[j, h] for j in seqs] for _, h in units]
            st["qs"] = [_rows([_dot(_bf(rows(q32[u], j)), _bf(st["s"][u][j])) for j in seqs]) for u in ids]

    def state_write():
        s_, qs, kd, vb, sc = st["s"], st["qs"], st["kd"], st["vb"], st["scores"]
        if ns == 1:
            upd = [_dot(_bf(_rows([kd[u].T, sc[u]])), vb[u]) for u in ids]
            for u, (p, h) in enumerate(units):
                o_ref[p, :, dv_sl(h)] = upd[u][RET_DK:] + qs[u] * dq_ref[h]
                s_ref[p, h] = s_[u] * dc_ref[h] + upd[u][:RET_DK]
        else:
            v32 = [vb[u].astype(F32) for u in ids]
            intra = [_dot(_bf(sc[u]), vb[u]) for u in ids]
            for u, (p, h) in enumerate(units):
                o_ref[p, :, dv_sl(h)] = intra[u] + qs[u] * dq_ref[h]
            ktv = [[_dot_tn(_bf(rows(kd[u], j)), _bf(rows(v32[u], j))) for j in seqs] for u in ids]
            for u, (p, h) in enumerate(units):
                for j in seqs:
                    s_ref[j, h] = s_[u][j] * dc_ref[h] + ktv[u][j]

    return [scores, state_read, state_write]


def _retention_consts(c):
    log_g = jnp.log1p(-jnp.exp2(-5.0 - jnp.arange(RET_HEADS, dtype=F32)))
    idx = jnp.arange(STEP_ROWS)
    same = (idx[:, None] // c) == (idx[None, :] // c)
    off = (idx % c).astype(F32)
    diff = off[:, None] - off[None, :]
    causal = same & (diff >= 0)
    d_intra = jnp.where(causal[None], jnp.exp(log_g[:, None, None] * jnp.where(causal, diff, 0.0)[None]), 0.0)
    d_q = jnp.broadcast_to(jnp.exp(log_g[:, None] * (off + 1.0)[None, :])[..., None], (RET_HEADS, STEP_ROWS, RET_DV))
    d_k = jnp.broadcast_to(jnp.exp(log_g[:, None] * (c - 1.0 - off)[None, :])[..., None],
                           (RET_HEADS, STEP_ROWS, RET_DK))
    d_c = jnp.broadcast_to(jnp.exp(log_g * c)[:, None, None], (RET_HEADS, 1, RET_DV))
    return d_intra, d_q, d_k, d_c


def _block_diag(x2):
    m = x2.shape[0]
    r = lax.broadcasted_iota(jnp.int32, (2 * m, 2 * m), 0)
    cc = lax.broadcasted_iota(jnp.int32, (2 * m, 2 * m), 1)
    return jnp.where((r < m) == (cc < m), _rows([x2, x2]), jnp.zeros((), x2.dtype))


def _solve_correction(a, c):
    ids = range(len(a))
    m = a[0].shape[0]
    n = [-x for x in a]
    ab = [_bf(x) for x in a]
    p = [_dot(ab[i], _block_diag(ab[i])) for i in ids]
    span = 2
    while span < c:
        pb = [_block_diag(_bf(x)) for x in p]
        last = 2 * span >= c
        if last:
            prod = [_dot(_bf(n[i]), pb[i]) for i in ids]
            n = [n[i] + p[i] + prod[i] for i in ids]
        else:
            prod = [_dot(_bf(_rows([n[i], p[i]])), pb[i]) for i in ids]
            n = [n[i] + p[i] + prod[i][:m] for i in ids]
            p = [prod[i][m:] for i in ids]
        span *= 2
    return n


def _recurrence_kernel(*refs, par, ns, c, zero_init):
    (qkv_ref, gb_ref, q_ref, k_ref, v_ref, dintra_ref, dq_ref, dk_ref, dc_ref), refs = refs[:9], refs[9:]
    init_refs, (o_ref, s_ref, oret_ref, sret_ref) = refs[:-4], refs[-4:]

    @pl.when(pl.program_id(1) == 0)
    def _():
        if zero_init:
            s_ref[...] = jnp.zeros_like(s_ref)
            sret_ref[...] = jnp.zeros_like(sret_ref)
        else:
            s_ref[...] = init_refs[0][...]
            sret_ref[...] = init_refs[1][...]

    retention = _retention_stages(q_ref, k_ref, v_ref, dintra_ref, dq_ref, dk_ref, dc_ref, oret_ref, sret_ref,
                                  par=par, ns=ns, c=c)
    m = STEP_ROWS

    def l2norm(x):
        return x * lax.rsqrt(jnp.sum(x * x, axis=-1, keepdims=True) + EPS)

    lane = lax.broadcasted_iota(jnp.int32, (m, LANES), 1)
    off = lax.broadcasted_iota(jnp.int32, (m, LANES), 0) % c
    r2 = lax.broadcasted_iota(jnp.int32, (m, 2 * m), 0)
    lane2 = lax.broadcasted_iota(jnp.int32, (m, 2 * m), 1)
    c2 = lane2 % m
    same = (r2 // c) == (c2 // c)
    tril = same & (r2 >= c2)
    strict = same & (r2 > c2)
    even = lane2 < m
    pick = (lax.broadcasted_iota(jnp.int32, (2 * m, 2 * GDN_DK), 0) < m) == (
        lax.broadcasted_iota(jnp.int32, (2 * m, 2 * GDN_DK), 1) < GDN_DK)

    def gates(p):
        cum = jnp.where(lane < GDN_HEADS, gb_ref[p, :, :LANES], 0.0)
        shift = 1
        while shift < c:
            cum = cum + jnp.where(off >= shift, pltpu.roll(cum, shift, 0), 0.0)
            shift *= 2
        last = cum.reshape(ns, c, LANES)[:, c - 1:c, :]
        e_rest = jnp.exp((last - cum.reshape(ns, c, LANES)).reshape(m, LANES))
        return dict(cum=cum, cum_t=cum.T, e_cum=jnp.exp(cum), e_rest=e_rest, e_last=jnp.exp(last),
                    beta=gb_ref[p, :, LANES:])

    def padded(x, h):
        z = jnp.zeros_like(x)
        return _rows([x, z]) if h % 2 == 0 else _rows([z, x])

    gt = [gates(p) for p in range(par)]
    units = [(p, h) for p in range(par) for h in range(GDN_HEADS)]
    ids = range(len(units))
    pairs = range(len(units) // 2)
    pair_of = lambda u: u // 2
    both = lambda xs, i: jnp.concatenate([xs[2 * i], xs[2 * i + 1]], axis=-1)
    col = lambda x, h: x[:, h:h + 1]
    q = [l2norm(qkv_ref[p, :, h * GDN_DK:(h + 1) * GDN_DK]) * (GDN_DK ** -0.5) for p, h in units]
    k = [l2norm(qkv_ref[p, :, GDN_QK + h * GDN_DK:GDN_QK + (h + 1) * GDN_DK]) for p, h in units]
    v = [qkv_ref[p, :, 2 * GDN_QK + h * GDN_DV:2 * GDN_QK + (h + 1) * GDN_DV] for p, h in units]
    beta = [col(gt[p]["beta"], h) for p, h in units]
    eg = [col(gt[p]["e_cum"], h) for p, h in units]
    cum_col = [col(gt[p]["cum"], h) for p, h in units]
    cum_row = [gt[p]["cum_t"][h:h + 1, :] for p, h in units]
    decay = [jnp.exp(jnp.where(tril, jnp.where(even, cum_col[2 * i], cum_col[2 * i + 1]) - both(cum_row, i), -jnp.inf))
             for i in pairs]
    kb = [k[u] * beta[u] for u in ids]
    kpick = [jnp.where(pick, _rows([_bf(both(k, i))] * 2), jnp.zeros((), BF16)) for i in pairs]
    raw = [_dot_nt(_bf(_rows([both(kb, i), both(q, i)])), kpick[i]) for i in pairs]
    a = [jnp.where(strict, raw[i][:m] * decay[i], 0.0) for i in pairs]
    attn = [raw[i][m:] * decay[i] for i in pairs]
    retention[0]()
    n = _solve_correction(a, c)
    retention[1]()
    rhs = [jnp.concatenate([v[u] * beta[u], kb[u] * eg[u]], axis=-1) for u in ids]
    nb16 = [_bf(x) for x in n]
    sol = [rhs[u] + _dot(nb16[pair_of(u)], padded(_bf(rhs[u]), units[u][1])) for u in ids]
    uu = [sol[u][:, :GDN_DV] for u in ids]
    w = [sol[u][:, GDN_DV:] for u in ids]
    qe = [q[u] * eg[u] for u in ids]
    kd = [k[u] * col(gt[p]["e_rest"], h) for u, (p, h) in enumerate(units)]
    retention[2]()
    dv_sl = lambda h: slice(h * GDN_DV, (h + 1) * GDN_DV)
    if ns == 1:
        s = [s_ref[p, h] for p, h in units]
        ws_qs = [_dot(_bf(_rows([w[u], qe[u]])), _bf(s[u])) for u in ids]
        v_new = [uu[u] - ws_qs[u][:m] for u in ids]
        upd = [_dot(_bf(_rows([padded(kd[u], h).T, attn[pair_of(u)]])), padded(_bf(v_new[u]), h))
               for u, (p, h) in enumerate(units)]
        for u, (p, h) in enumerate(units):
            o_ref[p, :, dv_sl(h)] = ws_qs[u][m:] + upd[u][GDN_DK:]
            s_ref[p, h] = s[u] * gt[p]["e_last"][0, :, h:h + 1] + upd[u][:GDN_DK]
    else:
        seqs = range(ns)
        rows = lambda x, j: x[j * c:(j + 1) * c]
        s = [[s_ref[j, h] for j in seqs] for _, h in units]
        ws_qs = [[_dot(_bf(_rows([rows(w[u], j), rows(qe[u], j)])), _bf(s[u][j])) for j in seqs] for u in ids]
        v_new = [_rows([rows(uu[u], j) - ws_qs[u][j][:c] for j in seqs]) for u in ids]
        qs = [_rows([ws_qs[u][j][c:] for j in seqs]) for u in ids]
        intra = [_dot(_bf(attn[pair_of(u)]), padded(_bf(v_new[u]), units[u][1])) for u in ids]
        for u, (p, h) in enumerate(units):
            o_ref[p, :, dv_sl(h)] = qs[u] + intra[u]
        ktv = [[_dot_tn(_bf(rows(kd[u], j)), _bf(rows(v_new[u], j))) for j in seqs] for u in ids]
        for u, (p, h) in enumerate(units):
            for j in seqs:
                s_ref[j, h] = s[u][j] * gt[p]["e_last"][j, :, h:h + 1] + ktv[u][j]


def _recurrences(qk, v, conv, gb, s_ret, s_gdn, nb, t, c):
    groups, rows, ns, par = _step_layout(nb, t, c)
    zero_init = s_ret is None
    blk = lambda cols, col_blk=0: pl.BlockSpec((par, STEP_ROWS, cols), lambda g, i: (g, i, col_blk))
    ret_state = pl.BlockSpec((par * ns, RET_HEADS, RET_DK, RET_DV), lambda g, i: (g, 0, 0, 0))
    gdn_state = pl.BlockSpec((par * ns, GDN_HEADS, GDN_DK, GDN_DV), lambda g, i: (g, 0, 0, 0))
    view = lambda x: x.reshape(groups, rows, x.shape[-1])
    qk3 = view(qk)
    o_g, gdn_new, o_r, ret_new = pl.pallas_call(
        functools.partial(_recurrence_kernel, par=par, ns=ns, c=c, zero_init=zero_init),
        grid=(groups // par, rows // STEP_ROWS),
        in_specs=[
            blk(CONV_CH), blk(AB_W), blk(RET_QK, 0), blk(RET_QK, 1), blk(RET_V),
            _const_spec((RET_HEADS, STEP_ROWS, STEP_ROWS)),
            _const_spec((RET_HEADS, STEP_ROWS, RET_DV)),
            _const_spec((RET_HEADS, STEP_ROWS, RET_DK)),
            _const_spec((RET_HEADS, 1, RET_DV)),
        ] + ([] if zero_init else [gdn_state, ret_state]),
        out_specs=[blk(GDN_V), gdn_state, blk(RET_V), ret_state],
        out_shape=[
            jax.ShapeDtypeStruct((groups, rows, GDN_V), F32),
            jax.ShapeDtypeStruct((nb, GDN_HEADS, GDN_DK, GDN_DV), F32),
            jax.ShapeDtypeStruct((groups, rows, RET_V), F32),
            jax.ShapeDtypeStruct((nb, RET_HEADS, RET_DK, RET_DV), F32),
        ],
        compiler_params=_params(("arbitrary", "arbitrary")),
        name="recurrences",
    )(view(conv), view(gb), qk3, qk3, view(v), *_retention_consts(c), *(() if zero_init else (s_gdn, s_ret)))
    return o_r.reshape(nb * t, RET_V), o_g.reshape(nb * t, GDN_V), ret_new, gdn_new


def _attn_out_kernel(x_ref, or_ref, og_ref, ln1_ref, wgr_ref, wzgg_ref, rn_ref, gn_ref, wr_ref, wg_ref, wo_ref,
                     o_ref, h_ref, lhs_ref, br_ref):
    tm = x_ref.shape[0]
    h_ref[...] = _bf(_rms(x_ref[...], ln1_ref[...]))
    slabs = [slice(s * MXU_COLS, (s + 1) * MXU_COLS) for s in range(D_MODEL // MXU_COLS)]
    row_blocks = [slice(b, b + EPILOGUE_ROWS) for b in range(0, tm, EPILOGUE_ROWS)]

    def gate(block, sl):
        if block == 0:
            return _dot(h_ref[...], wgr_ref[:, sl])
        return _dot(h_ref[...], wzgg_ref[:, (block - 1) * D_MODEL + sl.start:(block - 1) * D_MODEL + sl.stop])

    def branch(src_ref, head_dim, gain_ref, block, w_ref):
        for sl in slabs:
            g = gate(block, sl)
            for rows in row_blocks:
                o = src_ref[rows, sl]
                heads = [o[:, k:k + head_dim] for k in range(0, MXU_COLS, head_dim)]
                normed = [oh * lax.rsqrt(jnp.mean(oh * oh, axis=-1, keepdims=True) + EPS) for oh in heads]
                lhs_ref[rows, sl] = _bf(jnp.concatenate(normed, axis=-1) * gain_ref[:, sl] * _silu(g[rows]))
        return _dot(lhs_ref[...], w_ref[...])

    br_ref[0] = branch(or_ref, RET_DV, rn_ref, 0, wr_ref)
    br_ref[1] = branch(og_ref, GDN_DV, gn_ref, 1, wg_ref)
    for sl in slabs:
        g_r = gate(2, sl)
        g_g = gate(3, sl)
        for rows in row_blocks:
            lhs_ref[rows, sl] = _bf(jax.nn.sigmoid(g_r[rows]) * br_ref[0, rows, sl]
                                    + jax.nn.sigmoid(g_g[rows]) * br_ref[1, rows, sl])
    o_ref[...] = x_ref[...] + _dot(lhs_ref[...], wo_ref[...])


def _attn_out(x2d, o_r, o_g, ln1, w_bf, w_zgg, ret_norm, gdn_norm, w_ret_br, w_gdn_br, w_out):
    n = x2d.shape[0]
    tm = min(512, n)
    rows = pl.BlockSpec((tm, D_MODEL), lambda i: (i, 0))
    vec = _const_spec((1, D_MODEL))
    wspec = _const_spec((D_MODEL, D_MODEL))
    return pl.pallas_call(
        _attn_out_kernel,
        grid=(n // tm,),
        in_specs=[rows, rows, rows, vec, _w_in_block(D_MODEL, 2), _const_spec((D_MODEL, GATE_W - D_MODEL)), vec, vec,
                  wspec, wspec, wspec],
        out_specs=rows,
        out_shape=jax.ShapeDtypeStruct((n, D_MODEL), F32),
        scratch_shapes=[pltpu.VMEM((tm, D_MODEL), BF16), pltpu.VMEM((tm, D_MODEL), BF16),
                        pltpu.VMEM((2, tm, D_MODEL), F32)],
        compiler_params=_params(("arbitrary",)),
        name="attn_out",
    )(x2d, o_r, o_g, ln1, w_bf, w_zgg, ret_norm, gdn_norm, w_ret_br, w_gdn_br, w_out)


FF_BLOCK = 1024


def _mlp_kernel(x_ref, ln2_ref, wu_ref, wd_ref, lnf_ref, o_ref):
    x = x_ref[...]
    hb = _bf(_rms(x, ln2_ref[...]))
    acc = x
    for f in range(D_FF // FF_BLOCK):
        sl = slice(f * FF_BLOCK, (f + 1) * FF_BLOCK)
        up = jnp.maximum(_dot(hb, wu_ref[:, sl]), 0.0)
        acc = acc + _dot(_bf(up * up), wd_ref[sl, :])
    o_ref[...] = _rms(acc, lnf_ref[...])


def _mlp(x2d, ln2, w_up, w_down, ln_f):
    n = x2d.shape[0]
    tm = min(1024, n)
    rows = pl.BlockSpec((tm, D_MODEL), lambda i: (i, 0))
    return pl.pallas_call(
        _mlp_kernel,
        grid=(n // tm,),
        in_specs=[rows, _const_spec((1, D_MODEL)), _const_spec((D_MODEL, D_FF)), _const_spec((D_FF, D_MODEL)),
                  _const_spec((1, D_MODEL))],
        out_specs=rows,
        out_shape=jax.ShapeDtypeStruct((n, D_MODEL), F32),
        compiler_params=_params(("arbitrary",)),
        name="mlp",
    )(x2d, ln2, w_up, w_down, ln_f)


def _group(x, pos, s_ret, s_gdn, s_conv, wts):
    (in_wts, w_zgg, ret_norm, gdn_norm, w_ret_br, w_gdn_br, w_out, ln2, w_up, w_down, ln_f) = wts
    nb, t, _ = x.shape
    c = CHUNK if t % CHUNK == 0 else t
    assert STEP_ROWS % c == 0 and nb % (STEP_ROWS // c) == 0, (nb, t)
    x2d = x.reshape(nb * t, D_MODEL)
    qk, v, conv, gb, conv_new = _in_proj(x2d, t, pos, s_conv, in_wts)
    o_r, o_g, ret_new, gdn_new = _recurrences(qk, v, conv, gb, s_ret, s_gdn, nb, t, c)
    x1 = _attn_out(x2d, o_r, o_g, in_wts[0], in_wts[1], w_zgg, ret_norm, gdn_norm, w_ret_br, w_gdn_br, w_out)
    y = _mlp(x1, ln2, w_up, w_down, ln_f)
    return y.reshape(x.shape), ret_new, gdn_new, conv_new


def kernel(x_prompt, x_sample, state_ret, state_gdn, state_conv, ln1, w_in, conv_w, a_log, dt_bias, ret_norm,
           gdn_norm, w_ret_br, w_gdn_br, w_out, ln2, w_up, w_down, ln_f):
    depth = w_in.shape[0]
    assert depth == 1, "single-layer trunk"
    bp, tp, _ = x_prompt.shape
    ts = x_sample.shape[1]
    w = w_in[0]
    o_gr = 2 * RET_QK + RET_V
    o_qkv = o_gr + RET_V
    o_a = o_qkv + CONV_CH
    o_z = o_a + 2 * GDN_HEADS
    assert (o_gr, o_qkv) == (QKV_W, CONV_CH)
    w_bf = w.astype(BF16)
    w_zgg = w_bf[:, o_z:]
    w_ab = jnp.zeros((D_MODEL, AB_W), F32)
    w_ab = w_ab.at[:, :GDN_HEADS].set(w[:, o_a:o_a + GDN_HEADS])
    w_ab = w_ab.at[:, LANES:LANES + GDN_HEADS].set(w[:, o_a + GDN_HEADS:o_z]).astype(BF16)
    vec = lambda v: v.reshape(1, -1)
    in_wts = (vec(ln1[0]), w_bf, w_ab, conv_w[0], a_log[0], dt_bias[0])
    wts = (in_wts, w_zgg, vec(ret_norm[0]), vec(gdn_norm[0]), w_ret_br[0].astype(BF16), w_gdn_br[0].astype(BF16),
           w_out[0].astype(BF16), vec(ln2[0]), w_up[0].astype(BF16), w_down[0].astype(BF16), vec(ln_f))
    pos_p = jnp.arange(tp, dtype=F32)
    pos_s = PAST_LEN + jnp.arange(ts, dtype=F32)
    yp, ret_p, gdn_p, conv_p = _group(x_prompt, pos_p, None, None, jnp.zeros((bp, CONV_W - 1, CONV_CH), F32), wts)
    ys, ret_s, gdn_s, conv_s = _group(x_sample, pos_s, state_ret[0], state_gdn[0], state_conv[0], wts)
    return (yp, ys, ret_p[None], gdn_p[None], conv_p[None], ret_s[None], gdn_s[None], conv_s[None])
```

```python
import functools

import jax
import jax.numpy as jnp
from jax import lax
from jax.experimental import pallas as pl
from jax.experimental.pallas import tpu as pltpu

F32 = jnp.float32
BF16 = jnp.bfloat16

D_MODEL = 1024
PAST_LEN = 16384
RET_HEADS, RET_DK, RET_DV = 4, 128, 256
RET_QK = RET_HEADS * RET_DK
RET_V = RET_HEADS * RET_DV
GDN_HEADS, GDN_DK, GDN_DV = 8, 128, 128
GDN_QK = GDN_HEADS * GDN_DK
GDN_V = GDN_HEADS * GDN_DV
CONV_W = 4
CONV_CH = 2 * GDN_QK + GDN_V
D_FF = 4 * D_MODEL
CHUNK = 64
ROPE_BASE = 10000.0
EPS = 1e-6

SUBLANES = 8
LANES = 128
MXU_COLS = 256
STEP_ROWS = 64
EPILOGUE_ROWS = 64
VMEM_LIMIT = 56 * 1024 * 1024

QKV_W = 2 * RET_QK + RET_V
GATE_W = 4 * D_MODEL
AB_W = 2 * LANES


def _params(sem):
    return pltpu.CompilerParams(dimension_semantics=sem, vmem_limit_bytes=VMEM_LIMIT)


def _const_spec(shape):
    zeros = (0,) * len(shape)
    return pl.BlockSpec(shape, lambda *_: zeros, pipeline_mode=pl.Buffered(1))


def _dot(a, b):
    return jnp.dot(a, b, preferred_element_type=F32)


def _dot_nt(a, b):
    return lax.dot_general(a, b, (((1,), (1,)), ((), ())), preferred_element_type=F32)


def _dot_tn(a, b):
    return lax.dot_general(a, b, (((0,), (0,)), ((), ())), preferred_element_type=F32)


def _bf(x):
    return x.astype(BF16)


def _rows(parts):
    return jnp.concatenate(parts, axis=0)


def _rms(x, g):
    return x * lax.rsqrt(jnp.mean(x * x, axis=-1, keepdims=True) + EPS) * g


def _silu(x):
    h = 0.5 * x
    return h + h * jnp.tanh(h)


def _in_proj_kernel(x_ref, ln_ref, wqkv_ref, wconv_ref, wab_ref, cos_ref, sin_ref, cw_ref, hist_ref,
                    alog_ref, dtb_ref, qk_ref, v_ref, conv_ref, gb_ref, tail_ref, h_ref, carry_ref,
                    *, seq_is_group, tiles_per_seq):
    i = pl.program_id(0)
    tm = x_ref.shape[0]
    row_blocks = [slice(b, b + EPILOGUE_ROWS) for b in range(0, tm, EPILOGUE_ROWS)]

    @pl.when(i == 0)
    def _():
        carry_ref[...] = jnp.zeros_like(carry_ref)

    h_ref[...] = _bf(_rms(x_ref[...], ln_ref[...]))
    ab = _dot(h_ref[...], wab_ref[...])
    gb_ref[:, :LANES] = -jnp.exp(alog_ref[...]) * jax.nn.softplus(ab[:, :LANES] + dtb_ref[...])
    gb_ref[:, LANES:] = jax.nn.sigmoid(ab[:, LANES:])

    def rope_slab(s):
        sl = slice(s * MXU_COLS, (s + 1) * MXU_COLS)
        r = _dot(h_ref[...], wqkv_ref[:, sl])
        for rows in row_blocks:
            for part in range(MXU_COLS // RET_DK):
                x = r[rows, part * RET_DK:(part + 1) * RET_DK]
                y = x * cos_ref[rows, :] + pltpu.roll(x, RET_DK // 2, 1) * sin_ref[rows, :]
                start = sl.start + part * RET_DK
                if start >= RET_QK:
                    y = y * (RET_DK ** -0.5)
                qk_ref[rows, start:start + RET_DK] = _bf(y)

    def v_slab(s):
        sl = slice(s * MXU_COLS, (s + 1) * MXU_COLS)
        v_ref[:, sl] = _bf(_dot(h_ref[...], wqkv_ref[:, 2 * RET_QK + sl.start:2 * RET_QK + sl.stop]))

    sub = lax.broadcasted_iota(jnp.int32, (1, SUBLANES, MXU_COLS), 1)

    def conv_slab(s):
        sl = slice(s * MXU_COLS, (s + 1) * MXU_COLS)
        w = [cw_ref[tap:tap + 1, sl][None] for tap in range(CONV_W)]
        r = _dot(h_ref[...], wconv_ref[:, sl])
        if not seq_is_group:
            tail_ref[:, :, sl] = r[tm - SUBLANES:][None]
            x_last = jnp.where(i % tiles_per_seq == 0, hist_ref[:, :, sl], carry_ref[:, sl][None])
            carry_ref[:, sl] = r[tm - SUBLANES:]
            x_last1 = pltpu.roll(x_last, 1, 1)
            pair_last2 = pltpu.roll(w[1] * x_last + w[0] * x_last1, 2, 1)
        for rows in row_blocks:
            g = (rows.stop - rows.start) // SUBLANES
            r3 = r[rows].reshape(g, SUBLANES, MXU_COLS)
            cur1 = pltpu.roll(r3, 1, 1)
            if seq_is_group:
                seqs = slice(rows.start // SUBLANES, rows.stop // SUBLANES)
                x3, x2, prev1 = (hist_ref[seqs, j:j + 1, sl] for j in range(CONV_W - 1))
                prev2 = jnp.where(sub < 1, w[1] * x2 + w[0] * x3, w[1] * prev1 + w[0] * x2)
                tail_ref[seqs, :, sl] = r3[:, SUBLANES - (CONV_W - 1):, :]
            else:
                prev1 = jnp.concatenate([x_last1, cur1[:g - 1]], axis=0)
            x1 = jnp.where(sub < 1, prev1, cur1)
            pair = w[1] * r3 + w[0] * x1
            cur2 = pltpu.roll(pair, 2, 1)
            if not seq_is_group:
                prev2 = jnp.concatenate([pair_last2, cur2[:g - 1]], axis=0)
                x_last1, pair_last2 = cur1[g - 1:], cur2[g - 1:]
            y = (w[3] * r3 + w[2] * x1) + jnp.where(sub < 2, prev2, cur2)
            conv_ref[rows, sl] = _silu(y).reshape(g * SUBLANES, MXU_COLS)

    conv = [functools.partial(conv_slab, s) for s in range(CONV_CH // MXU_COLS)]
    light = ([functools.partial(rope_slab, s) for s in range(2 * RET_QK // MXU_COLS)]
             + [functools.partial(v_slab, s) for s in range(RET_V // MXU_COLS)])
    order = []
    while conv or light:
        take = -(-len(conv) // max(len(light), 1))
        order += conv[:take] + light[:1]
        conv, light = conv[take:], light[1:]
    for slab in order:
        slab()


def _w_in_block(width, index):
    assert QKV_W + D_MODEL == CONV_CH and QKV_W == 2 * D_MODEL
    return pl.BlockSpec((D_MODEL, width), lambda i: (0, index), pipeline_mode=pl.Buffered(1))


def _in_proj(x2d, t, pos, conv0, wts):
    ln1, w_bf, w_ab, conv_w, a_log, dt_bias = wts
    n = x2d.shape[0]
    nb = n // t
    seq_is_group = t == SUBLANES
    tm = min(256, n) if seq_is_group else min(512, t)
    assert n % tm == 0 and (seq_is_group or t % tm == 0), (n, t, tm)
    tiles_per_seq = max(t // tm, 1)
    inv = ROPE_BASE ** (-jnp.arange(0, RET_DK, 2, dtype=F32) / RET_DK)
    ang = jnp.tile(pos, max(tm // t, 1))[:, None] * inv[None, :]
    cosf = jnp.concatenate([jnp.cos(ang), jnp.cos(ang)], axis=-1)
    sinf = jnp.concatenate([-jnp.sin(ang), jnp.sin(ang)], axis=-1)
    pos_blocks = cosf.shape[0] // tm
    rows = lambda cols: pl.BlockSpec((tm, cols), lambda i: (i, 0))
    if seq_is_group:
        hist = conv0
        hist_spec = pl.BlockSpec((tm // SUBLANES, CONV_W - 1, CONV_CH), lambda i: (i, 0, 0))
        tail_spec = hist_spec
        tail_shape = jax.ShapeDtypeStruct(conv0.shape, F32)
    else:
        hist = jnp.pad(conv0, ((0, 0), (SUBLANES - (CONV_W - 1), 0), (0, 0)))
        hist_spec = pl.BlockSpec((1, SUBLANES, CONV_CH), lambda i: (i // tiles_per_seq, 0, 0))
        tail_spec = pl.BlockSpec((1, SUBLANES, CONV_CH), lambda i: (i, 0, 0))
        tail_shape = jax.ShapeDtypeStruct((n // tm, SUBLANES, CONV_CH), F32)
    pad_lanes = lambda x: jnp.pad(x.reshape(1, GDN_HEADS), ((0, 0), (0, LANES - GDN_HEADS)))
    pos_spec = pl.BlockSpec((tm, RET_DK), lambda i: (i % pos_blocks, 0))
    qk, v, conv, gb, tail = pl.pallas_call(
        functools.partial(_in_proj_kernel, seq_is_group=seq_is_group, tiles_per_seq=tiles_per_seq),
        grid=(n // tm,),
        in_specs=[
            rows(D_MODEL),
            _const_spec((1, D_MODEL)),
            _w_in_block(QKV_W, 0),
            _w_in_block(CONV_CH, 1),
            _const_spec((D_MODEL, AB_W)),
            pos_spec,
            pos_spec,
            _const_spec((CONV_W, CONV_CH)),
            hist_spec,
            _const_spec((1, LANES)),
            _const_spec((1, LANES)),
        ],
        out_specs=[rows(2 * RET_QK), rows(RET_V), rows(CONV_CH), rows(AB_W), tail_spec],
        out_shape=[
            jax.ShapeDtypeStruct((n, 2 * RET_QK), BF16),
            jax.ShapeDtypeStruct((n, RET_V), BF16),
            jax.ShapeDtypeStruct((n, CONV_CH), F32),
            jax.ShapeDtypeStruct((n, AB_W), F32),
            tail_shape,
        ],
        scratch_shapes=[pltpu.VMEM((tm, D_MODEL), BF16), pltpu.VMEM((SUBLANES, CONV_CH), F32)],
        compiler_params=_params(("arbitrary",)),
        name="in_proj",
    )(x2d, ln1, w_bf, w_bf, w_ab, cosf, sinf, conv_w, hist, pad_lanes(a_log), pad_lanes(dt_bias))
    if not seq_is_group:
        tail = tail.reshape(nb, tiles_per_seq, SUBLANES, CONV_CH)[:, -1, SUBLANES - (CONV_W - 1):, :]
    return qk, v, conv, gb, tail


def _step_layout(nb, t, c):
    ns = STEP_ROWS // c
    if ns == 1:
        par = max(p for p in (8, 4, 2, 1) if nb % p == 0)
        return nb, t, ns, par
    assert t == c and nb % ns == 0, (nb, t, c)
    return nb // ns, STEP_ROWS, ns, 1


def _retention_stages(q_ref, k_ref, v_ref, dintra_ref, dq_ref, dk_ref, dc_ref, o_ref, s_ref, *, par, ns, c):
    units = [(p, h) for p in range(par) for h in range(RET_HEADS)]
    ids = range(len(units))
    seqs = range(ns)
    rows = lambda x, j: x[j * c:(j + 1) * c]
    dk_sl = lambda h: slice(h * RET_DK, (h + 1) * RET_DK)
    dv_sl = lambda h: slice(h * RET_DV, (h + 1) * RET_DV)
    st = {}

    def scores():
        st["qb"] = [q_ref[p, :, dk_sl(h)] for p, h in units]
        kb = [k_ref[p, :, dk_sl(h)] for p, h in units]
        st["vb"] = [v_ref[p, :, dv_sl(h)] for p, h in units]
        st["scores"] = [_dot_nt(st["qb"][u], kb[u]) * dintra_ref[units[u][1]] for u in ids]
        st["kd"] = [kb[u].astype(F32) * dk_ref[units[u][1]] for u in ids]

    def state_read():
        qb = st["qb"]
        if ns == 1:
            st["s"] = [s_ref[p, h] for p, h in units]
            st["qs"] = [_dot(qb[u], _bf(st["s"][u])) for u in ids]
        else:
            q32 = [qb[u].astype(F32) for u in ids]
            st["s"] = [[s_ref[j, h] for j in seqs] for _, h in units]
            st["qs"] = [_rows([_dot(_bf(rows(q32[u], j)), _bf(st["s"][u][j])) for j in seqs]) for u in ids]

    def state_write():
        s_, qs, kd, vb, sc = st["s"], st["qs"], st["kd"], st["vb"], st["scores"]
        if ns == 1:
            upd = [_dot(_bf(_rows([kd[u].T, sc[u]])), vb[u]) for u in ids]
            for u, (p, h) in enumerate(units):
                o_ref[p, :, dv_sl(h)] = upd[u][RET_DK:] + qs[u] * dq_ref[h]
                s_ref[p, h] = s_[u] * dc_ref[h] + upd[u][:RET_DK]
        else:
            v32 = [vb[u].astype(F32) for u in ids]
            intra = [_dot(_bf(sc[u]), vb[u]) for u in ids]
            for u, (p, h) in enumerate(units):
                o_ref[p, :, dv_sl(h)] = intra[u] + qs[u] * dq_ref[h]
            ktv = [[_dot_tn(_bf(rows(kd[u], j)), _bf(rows(v32[u], j))) for j in seqs] for u in ids]
            for u, (p, h) in enumerate(units):
                for j in seqs:
                    s_ref[j, h] = s_[u][j] * dc_ref[h] + ktv[u][j]

    return [scores, state_read, state_write]


def _retention_consts(c):
    log_g = jnp.log1p(-jnp.exp2(-5.0 - jnp.arange(RET_HEADS, dtype=F32)))
    idx = jnp.arange(STEP_ROWS)
    same = (idx[:, None] // c) == (idx[None, :] // c)
    off = (idx % c).astype(F32)
    diff = off[:, None] - off[None, :]
    causal = same & (diff >= 0)
    d_intra = jnp.where(causal[None], jnp.exp(log_g[:, None, None] * jnp.where(causal, diff, 0.0)[None]), 0.0)
    d_q = jnp.broadcast_to(jnp.exp(log_g[:, None] * (off + 1.0)[None, :])[..., None], (RET_HEADS, STEP_ROWS, RET_DV))
    d_k = jnp.broadcast_to(jnp.exp(log_g[:, None] * (c - 1.0 - off)[None, :])[..., None],
                           (RET_HEADS, STEP_ROWS, RET_DK))
    d_c = jnp.broadcast_to(jnp.exp(log_g * c)[:, None, None], (RET_HEADS, 1, RET_DV))
    return d_intra, d_q, d_k, d_c


def _block_diag(x2):
    m = x2.shape[0]
    r = lax.broadcasted_iota(jnp.int32, (2 * m, 2 * m), 0)
    cc = lax.broadcasted_iota(jnp.int32, (2 * m, 2 * m), 1)
    return jnp.where((r < m) == (cc < m), _rows([x2, x2]), jnp.zeros((), x2.dtype))


def _solve_correction(a, c):
    ids = range(len(a))
    m = a[0].shape[0]
    n = [-x for x in a]
    ab = [_bf(x) for x in a]
    p = [_dot(ab[i], _block_diag(ab[i])) for i in ids]
    span = 2
    while span < c:
        pb = [_block_diag(_bf(x)) for x in p]
        last = 2 * span >= c
        if last:
            prod = [_dot(_bf(n[i]), pb[i]) for i in ids]
            n = [n[i] + p[i] + prod[i] for i in ids]
        else:
            prod = [_dot(_bf(_rows([n[i], p[i]])), pb[i]) for i in ids]
            n = [n[i] + p[i] + prod[i][:m] for i in ids]
            p = [prod[i][m:] for i in ids]
        span *= 2
    return n


def _recurrence_kernel(*refs, par, ns, c, zero_init):
    (qkv_ref, gb_ref, q_ref, k_ref, v_ref, dintra_ref, dq_ref, dk_ref, dc_ref), refs = refs[:9], refs[9:]
    init_refs, (o_ref, s_ref, oret_ref, sret_ref) = refs[:-4], refs[-4:]

    @pl.when(pl.program_id(1) == 0)
    def _():
        if zero_init:
            s_ref[...] = jnp.zeros_like(s_ref)
            sret_ref[...] = jnp.zeros_like(sret_ref)
        else:
            s_ref[...] = init_refs[0][...]
            sret_ref[...] = init_refs[1][...]

    retention = _retention_stages(q_ref, k_ref, v_ref, dintra_ref, dq_ref, dk_ref, dc_ref, oret_ref, sret_ref,
                                  par=par, ns=ns, c=c)
    m = STEP_ROWS

    def l2norm(x):
        return x * lax.rsqrt(jnp.sum(x * x, axis=-1, keepdims=True) + EPS)

    lane = lax.broadcasted_iota(jnp.int32, (m, LANES), 1)
    off = lax.broadcasted_iota(jnp.int32, (m, LANES), 0) % c
    r2 = lax.broadcasted_iota(jnp.int32, (m, 2 * m), 0)
    lane2 = lax.broadcasted_iota(jnp.int32, (m, 2 * m), 1)
    c2 = lane2 % m
    same = (r2 // c) == (c2 // c)
    tril = same & (r2 >= c2)
    strict = same & (r2 > c2)
    even = lane2 < m
    pick = (lax.broadcasted_iota(jnp.int32, (2 * m, 2 * GDN_DK), 0) < m) == (
        lax.broadcasted_iota(jnp.int32, (2 * m, 2 * GDN_DK), 1) < GDN_DK)

    def gates(p):
        cum = jnp.where(lane < GDN_HEADS, gb_ref[p, :, :LANES], 0.0)
        shift = 1
        while shift < c:
            cum = cum + jnp.where(off >= shift, pltpu.roll(cum, shift, 0), 0.0)
            shift *= 2
        last = cum.reshape(ns, c, LANES)[:, c - 1:c, :]
        e_rest = jnp.exp((last - cum.reshape(ns, c, LANES)).reshape(m, LANES))
        return dict(cum=cum, cum_t=cum.T, e_cum=jnp.exp(cum), e_rest=e_rest, e_last=jnp.exp(last),
                    beta=gb_ref[p, :, LANES:])

    def padded(x, h):
        z = jnp.zeros_like(x)
        return _rows([x, z]) if h % 2 == 0 else _rows([z, x])

    gt = [gates(p) for p in range(par)]
    units = [(p, h) for p in range(par) for h in range(GDN_HEADS)]
    ids = range(len(units))
    pairs = range(len(units) // 2)
    pair_of = lambda u: u // 2
    both = lambda xs, i: jnp.concatenate([xs[2 * i], xs[2 * i + 1]], axis=-1)
    col = lambda x, h: x[:, h:h + 1]
    q = [l2norm(qkv_ref[p, :, h * GDN_DK:(h + 1) * GDN_DK]) * (GDN_DK ** -0.5) for p, h in units]
    k = [l2norm(qkv_ref[p, :, GDN_QK + h * GDN_DK:GDN_QK + (h + 1) * GDN_DK]) for p, h in units]
    v = [qkv_ref[p, :, 2 * GDN_QK + h * GDN_DV:2 * GDN_QK + (h + 1) * GDN_DV] for p, h in units]
    beta = [col(gt[p]["beta"], h) for p, h in units]
    eg = [col(gt[p]["e_cum"], h) for p, h in units]
    cum_col = [col(gt[p]["cum"], h) for p, h in units]
    cum_row = [gt[p]["cum_t"][h:h + 1, :] for p, h in units]
    decay = [jnp.exp(jnp.where(tril, jnp.where(even, cum_col[2 * i], cum_col[2 * i + 1]) - both(cum_row, i), -jnp.inf))
             for i in pairs]
    kb = [k[u] * beta[u] for u in ids]
    kpick = [jnp.where(pick, _rows([_bf(both(k, i))] * 2), jnp.zeros((), BF16)) for i in pairs]
    raw = [_dot_nt(_bf(_rows([both(kb, i), both(q, i)])), kpick[i]) for i in pairs]
    a = [jnp.where(strict, raw[i][:m] * decay[i], 0.0) for i in pairs]
    attn = [raw[i][m:] * decay[i] for i in pairs]
    retention[0]()
    n = _solve_correction(a, c)
    retention[1]()
    rhs = [jnp.concatenate([v[u] * beta[u], kb[u] * eg[u]], axis=-1) for u in ids]
    nb16 = [_bf(x) for x in n]
    sol = [rhs[u] + _dot(nb16[pair_of(u)], padded(_bf(rhs[u]), units[u][1])) for u in ids]
    uu = [sol[u][:, :GDN_DV] for u in ids]
    w = [sol[u][:, GDN_DV:] for u in ids]
    qe = [q[u] * eg[u] for u in ids]
    kd = [k[u] * col(gt[p]["e_rest"], h) for u, (p, h) in enumerate(units)]
    retention[2]()
    dv_sl = lambda h: slice(h * GDN_DV, (h + 1) * GDN_DV)
    if ns == 1:
        s = [s_ref[p, h] for p, h in units]
        ws_qs = [_dot(_bf(_rows([w[u], qe[u]])), _bf(s[u])) for u in ids]
        v_new = [uu[u] - ws_qs[u][:m] for u in ids]
        upd = [_dot(_bf(_rows([padded(kd[u], h).T, attn[pair_of(u)]])), padded(_bf(v_new[u]), h))
               for u, (p, h) in enumerate(units)]
        for u, (p, h) in enumerate(units):
            o_ref[p, :, dv_sl(h)] = ws_qs[u][m:] + upd[u][GDN_DK:]
            s_ref[p, h] = s[u] * gt[p]["e_last"][0, :, h:h + 1] + upd[u][:GDN_DK]
    else:
        seqs = range(ns)
        rows = lambda x, j: x[j * c:(j + 1) * c]
        s = [[s_ref[j, h] for j in seqs] for _, h in units]
        ws_qs = [[_dot(_bf(_rows([rows(w[u], j), rows(qe[u], j)])), _bf(s[u][j])) for j in seqs] for u in ids]
        v_new = [_rows([rows(uu[u], j) - ws_qs[u][j][:c] for j in seqs]) for u in ids]
        qs = [_rows([ws_qs[u][j][c:] for j in seqs]) for u in ids]
        intra = [_dot(_bf(attn[pair_of(u)]), padded(_bf(v_new[u]), units[u][1])) for u in ids]
        for u, (p, h) in enumerate(units):
            o_ref[p, :, dv_sl(h)] = qs[u] + intra[u]
        ktv = [[_dot_tn(_bf(rows(kd[u], j)), _bf(rows(v_new[u], j))) for j in seqs] for u in ids]
        for u, (p, h) in enumerate(units):
            for j in seqs:
                s_ref[j, h] = s[u][j] * gt[p]["e_last"][j, :, h:h + 1] + ktv[u][j]


def _recurrences(qk, v, conv, gb, s_ret, s_gdn, nb, t, c):
    groups, rows, ns, par = _step_layout(nb, t, c)
    zero_init = s_ret is None
    blk = lambda cols, col_blk=0: pl.BlockSpec((par, STEP_ROWS, cols), lambda g, i: (g, i, col_blk))
    ret_state = pl.BlockSpec((par * ns, RET_HEADS, RET_DK, RET_DV), lambda g, i: (g, 0, 0, 0))
    gdn_state = pl.BlockSpec((par * ns, GDN_HEADS, GDN_DK, GDN_DV), lambda g, i: (g, 0, 0, 0))
    view = lambda x: x.reshape(groups, rows, x.shape[-1])
    qk3 = view(qk)
    o_g, gdn_new, o_r, ret_new = pl.pallas_call(
        functools.partial(_recurrence_kernel, par=par, ns=ns, c=c, zero_init=zero_init),
        grid=(groups // par, rows // STEP_ROWS),
        in_specs=[
            blk(CONV_CH), blk(AB_W), blk(RET_QK, 0), blk(RET_QK, 1), blk(RET_V),
            _const_spec((RET_HEADS, STEP_ROWS, STEP_ROWS)),
            _const_spec((RET_HEADS, STEP_ROWS, RET_DV)),
            _const_spec((RET_HEADS, STEP_ROWS, RET_DK)),
            _const_spec((RET_HEADS, 1, RET_DV)),
        ] + ([] if zero_init else [gdn_state, ret_state]),
        out_specs=[blk(GDN_V), gdn_state, blk(RET_V), ret_state],
        out_shape=[
            jax.ShapeDtypeStruct((groups, rows, GDN_V), F32),
            jax.ShapeDtypeStruct((nb, GDN_HEADS, GDN_DK, GDN_DV), F32),
            jax.ShapeDtypeStruct((groups, rows, RET_V), F32),
            jax.ShapeDtypeStruct((nb, RET_HEADS, RET_DK, RET_DV), F32),
        ],
        compiler_params=_params(("arbitrary", "arbitrary")),
        name="recurrences",
    )(view(conv), view(gb), qk3, qk3, view(v), *_retention_consts(c), *(() if zero_init else (s_gdn, s_ret)))
    return o_r.reshape(nb * t, RET_V), o_g.reshape(nb * t, GDN_V), ret_new, gdn_new


def _attn_out_kernel(x_ref, or_ref, og_ref, ln1_ref, wgr_ref, wzgg_ref, rn_ref, gn_ref, wr_ref, wg_ref, wo_ref,
                     o_ref, h_ref, lhs_ref, br_ref):
    tm = x_ref.shape[0]
    h_ref[...] = _bf(_rms(x_ref[...], ln1_ref[...]))
    slabs = [slice(s * MXU_COLS, (s + 1) * MXU_COLS) for s in range(D_MODEL // MXU_COLS)]
    row_blocks = [slice(b, b + EPILOGUE_ROWS) for b in range(0, tm, EPILOGUE_ROWS)]

    def gate(block, sl):
        if block == 0:
            return _dot(h_ref[...], wgr_ref[:, sl])
        return _dot(h_ref[...], wzgg_ref[:, (block - 1) * D_MODEL + sl.start:(block - 1) * D_MODEL + sl.stop])

    def branch(src_ref, head_dim, gain_ref, block, w_ref):
        for sl in slabs:
            g = gate(block, sl)
            for rows in row_blocks:
                o = src_ref[rows, sl]
                heads = [o[:, k:k + head_dim] for k in range(0, MXU_COLS, head_dim)]
                normed = [oh * lax.rsqrt(jnp.mean(oh * oh, axis=-1, keepdims=True) + EPS) for oh in heads]
                lhs_ref[rows, sl] = _bf(jnp.concatenate(normed, axis=-1) * gain_ref[:, sl] * _silu(g[rows]))
        return _dot(lhs_ref[...], w_ref[...])

    br_ref[0] = branch(or_ref, RET_DV, rn_ref, 0, wr_ref)
    br_ref[1] = branch(og_ref, GDN_DV, gn_ref, 1, wg_ref)
    for sl in slabs:
        g_r = gate(2, sl)
        g_g = gate(3, sl)
        for rows in row_blocks:
            lhs_ref[rows, sl] = _bf(jax.nn.sigmoid(g_r[rows]) * br_ref[0, rows, sl]
                                    + jax.nn.sigmoid(g_g[rows]) * br_ref[1, rows, sl])
    o_ref[...] = x_ref[...] + _dot(lhs_ref[...], wo_ref[...])


def _attn_out(x2d, o_r, o_g, ln1, w_bf, w_zgg, ret_norm, gdn_norm, w_ret_br, w_gdn_br, w_out):
    n = x2d.shape[0]
    tm = min(512, n)
    rows = pl.BlockSpec((tm, D_MODEL), lambda i: (i, 0))
    vec = _const_spec((1, D_MODEL))
    wspec = _const_spec((D_MODEL, D_MODEL))
    return pl.pallas_call(
        _attn_out_kernel,
        grid=(n // tm,),
        in_specs=[rows, rows, rows, vec, _w_in_block(D_MODEL, 2), _const_spec((D_MODEL, GATE_W - D_MODEL)), vec, vec,
                  wspec, wspec, wspec],
        out_specs=rows,
        out_shape=jax.ShapeDtypeStruct((n, D_MODEL), F32),
        scratch_shapes=[pltpu.VMEM((tm, D_MODEL), BF16), pltpu.VMEM((tm, D_MODEL), BF16),
                        pltpu.VMEM((2, tm, D_MODEL), F32)],
        compiler_params=_params(("arbitrary",)),
        name="attn_out",
    )(x2d, o_r, o_g, ln1, w_bf, w_zgg, ret_norm, gdn_norm, w_ret_br, w_gdn_br, w_out)


FF_BLOCK = 1024


def _mlp_kernel(x_ref, ln2_ref, wu_ref, wd_ref, lnf_ref, o_ref):
    x = x_ref[...]
    hb = _bf(_rms(x, ln2_ref[...]))
    acc = x
    for f in range(D_FF // FF_BLOCK):
        sl = slice(f * FF_BLOCK, (f + 1) * FF_BLOCK)
        up = jnp.maximum(_dot(hb, wu_ref[:, sl]), 0.0)
        acc = acc + _dot(_bf(up * up), wd_ref[sl, :])
    o_ref[...] = _rms(acc, lnf_ref[...])


def _mlp(x2d, ln2, w_up, w_down, ln_f):
    n = x2d.shape[0]
    tm = min(1024, n)
    rows = pl.BlockSpec((tm, D_MODEL), lambda i: (i, 0))
    return pl.pallas_call(
        _mlp_kernel,
        grid=(n // tm,),
        in_specs=[rows, _const_spec((1, D_MODEL)), _const_spec((D_MODEL, D_FF)), _const_spec((D_FF, D_MODEL)),
                  _const_spec((1, D_MODEL))],
        out_specs=rows,
        out_shape=jax.ShapeDtypeStruct((n, D_MODEL), F32),
        compiler_params=_params(("arbitrary",)),
        name="mlp",
    )(x2d, ln2, w_up, w_down, ln_f)


def _group(x, pos, s_ret, s_gdn, s_conv, wts):
    (in_wts, w_zgg, ret_norm, gdn_norm, w_ret_br, w_gdn_br, w_out, ln2, w_up, w_down, ln_f) = wts
    nb, t, _ = x.shape
    c = CHUNK if t % CHUNK == 0 else t
    assert STEP_ROWS % c == 0 and nb % (STEP_ROWS // c) == 0, (nb, t)
    x2d = x.reshape(nb * t, D_MODEL)
    qk, v, conv, gb, conv_new = _in_proj(x2d, t, pos, s_conv, in_wts)
    o_r, o_g, ret_new, gdn_new = _recurrences(qk, v, conv, gb, s_ret, s_gdn, nb, t, c)
    x1 = _attn_out(x2d, o_r, o_g, in_wts[0], in_wts[1], w_zgg, ret_norm, gdn_norm, w_ret_br, w_gdn_br, w_out)
    y = _mlp(x1, ln2, w_up, w_down, ln_f)
    return y.reshape(x.shape), ret_new, gdn_new, conv_new


def kernel(x_prompt, x_sample, state_ret, state_gdn, state_conv, ln1, w_in, conv_w, a_log, dt_bias, ret_norm,
           gdn_norm, w_ret_br, w_gdn_br, w_out, ln2, w_up, w_down, ln_f):
    depth = w_in.shape[0]
    assert depth == 1, "single-layer trunk"
    bp, tp, _ = x_prompt.shape
    ts = x_sample.shape[1]
    w = w_in[0]
    o_gr = 2 * RET_QK + RET_V
    o_qkv = o_gr + RET_V
    o_a = o_qkv + CONV_CH
    o_z = o_a + 2 * GDN_HEADS
    assert (o_gr, o_qkv) == (QKV_W, CONV_CH)
    w_bf = w.astype(BF16)
    w_zgg = w_bf[:, o_z:]
    w_ab = jnp.zeros((D_MODEL, AB_W), F32)
    w_ab = w_ab.at[:, :GDN_HEADS].set(w[:, o_a:o_a + GDN_HEADS])
    w_ab = w_ab.at[:, LANES:LANES + GDN_HEADS].set(w[:, o_a + GDN_HEADS:o_z]).astype(BF16)
    vec = lambda v: v.reshape(1, -1)
    in_wts = (vec(ln1[0]), w_bf, w_ab, conv_w[0], a_log[0], dt_bias[0])
    wts = (in_wts, w_zgg, vec(ret_norm[0]), vec(gdn_norm[0]), w_ret_br[0].astype(BF16), w_gdn_br[0].astype(BF16),
           w_out[0].astype(BF16), vec(ln2[0]), w_up[0].astype(BF16), w_down[0].astype(BF16), vec(ln_f))
    pos_p = jnp.arange(tp, dtype=F32)
    pos_s = PAST_LEN + jnp.arange(ts, dtype=F32)
    yp, ret_p, gdn_p, conv_p = _group(x_prompt, pos_p, None, None, jnp.zeros((bp, CONV_W - 1, CONV_CH), F32), wts)
    ys, ret_s, gdn_s, conv_s = _group(x_sample, pos_s, state_ret[0], state_gdn[0], state_conv[0], wts)
    return (yp, ys, ret_p[None], gdn_p[None], conv_p[None], ret_s[None], gdn_s[None], conv_s[None])
```

```python
import functools

import jax
import jax.numpy as jnp
from jax import lax
from jax.experimental import pallas as pl
from jax.experimental.pallas import tpu as pltpu

F32 = jnp.float32
BF16 = jnp.bfloat16

D_MODEL = 1024
PAST_LEN = 16384
RET_HEADS, RET_DK, RET_DV = 4, 128, 256
RET_QK = RET_HEADS * RET_DK
RET_V = RET_HEADS * RET_DV
GDN_HEADS, GDN_DK, GDN_DV = 8, 128, 128
GDN_QK = GDN_HEADS * GDN_DK
GDN_V = GDN_HEADS * GDN_DV
CONV_W = 4
CONV_CH = 2 * GDN_QK + GDN_V
D_FF = 4 * D_MODEL
CHUNK = 64
ROPE_BASE = 10000.0
EPS = 1e-6

SUBLANES = 8
LANES = 128
MXU_COLS = 256
STEP_ROWS = 64
EPILOGUE_ROWS = 64
VMEM_LIMIT = 56 * 1024 * 1024

QKV_W = 2 * RET_QK + RET_V
GATE_W = 4 * D_MODEL
AB_W = 2 * LANES
BETA_LANE0 = GDN_HEADS


def _params(sem):
    return pltpu.CompilerParams(dimension_semantics=sem, vmem_limit_bytes=VMEM_LIMIT)


def _const_spec(shape):
    zeros = (0,) * len(shape)
    return pl.BlockSpec(shape, lambda *_: zeros, pipeline_mode=pl.Buffered(1))


def _dot(a, b):
    return jnp.dot(a, b, preferred_element_type=F32)


def _dot_nt(a, b):
    return lax.dot_general(a, b, (((1,), (1,)), ((), ())), preferred_element_type=F32)


def _dot_tn(a, b):
    return lax.dot_general(a, b, (((0,), (0,)), ((), ())), preferred_element_type=F32)


def _bf(x):
    return x.astype(BF16)


def _rows(parts):
    return jnp.concatenate(parts, axis=0)


def _rms(x, g):
    return x * lax.rsqrt(jnp.mean(x * x, axis=-1, keepdims=True) + EPS) * g


def _silu(x):
    h = 0.5 * x
    return h + h * jnp.tanh(h)


def _in_proj_kernel(x_ref, ln_ref, wqkv_ref, wconv_ref, wab_ref, cos_ref, sin_ref, cw_ref, hist_ref,
                    alog_ref, dtb_ref, qk_ref, v_ref, conv_ref, gb_ref, tail_ref, h_ref, carry_ref,
                    *, seq_is_group, tiles_per_seq):
    i = pl.program_id(0)
    tm = x_ref.shape[0]
    row_blocks = [slice(b, b + EPILOGUE_ROWS) for b in range(0, tm, EPILOGUE_ROWS)]

    @pl.when(i == 0)
    def _():
        carry_ref[...] = jnp.zeros_like(carry_ref)

    h_ref[...] = _bf(_rms(x_ref[...], ln_ref[...]))
    ab = _dot(h_ref[...], wab_ref[...])
    gb_ref[:, :LANES] = -jnp.exp(alog_ref[...]) * jax.nn.softplus(ab + dtb_ref[...])
    gb_ref[:, LANES:] = jax.nn.sigmoid(ab)

    def rope_slab(s):
        sl = slice(s * MXU_COLS, (s + 1) * MXU_COLS)
        r = _dot(h_ref[...], wqkv_ref[:, sl])
        for rows in row_blocks:
            for part in range(MXU_COLS // RET_DK):
                x = r[rows, part * RET_DK:(part + 1) * RET_DK]
                y = x * cos_ref[rows, :] + pltpu.roll(x, RET_DK // 2, 1) * sin_ref[rows, :]
                start = sl.start + part * RET_DK
                if start >= RET_QK:
                    y = y * (RET_DK ** -0.5)
                qk_ref[rows, start:start + RET_DK] = _bf(y)

    def v_slab(s):
        sl = slice(s * MXU_COLS, (s + 1) * MXU_COLS)
        v_ref[:, sl] = _bf(_dot(h_ref[...], wqkv_ref[:, 2 * RET_QK + sl.start:2 * RET_QK + sl.stop]))

    sub = lax.broadcasted_iota(jnp.int32, (1, SUBLANES, MXU_COLS), 1)

    def conv_slab(s):
        sl = slice(s * MXU_COLS, (s + 1) * MXU_COLS)
        w = [cw_ref[tap:tap + 1, sl][None] for tap in range(CONV_W)]
        r = _dot(h_ref[...], wconv_ref[:, sl])
        if not seq_is_group:
            tail_ref[:, :, sl] = r[tm - SUBLANES:][None]
            x_last = jnp.where(i % tiles_per_seq == 0, hist_ref[:, :, sl], carry_ref[:, sl][None])
            carry_ref[:, sl] = r[tm - SUBLANES:]
            x_last1 = pltpu.roll(x_last, 1, 1)
            pair_last2 = pltpu.roll(w[1] * x_last + w[0] * x_last1, 2, 1)
        for rows in row_blocks:
            g = (rows.stop - rows.start) // SUBLANES
            r3 = r[rows].reshape(g, SUBLANES, MXU_COLS)
            cur1 = pltpu.roll(r3, 1, 1)
            if seq_is_group:
                seqs = slice(rows.start // SUBLANES, rows.stop // SUBLANES)
                x3, x2, prev1 = (hist_ref[j, seqs, sl][:, None, :] for j in range(CONV_W - 1))
                prev2 = jnp.where(sub < 1, w[1] * x2 + w[0] * x3, w[1] * prev1 + w[0] * x2)
                for j in range(CONV_W - 1):
                    tail_ref[j, seqs, sl] = r3[:, SUBLANES - (CONV_W - 1) + j, :]
            else:
                prev1 = jnp.concatenate([x_last1, cur1[:g - 1]], axis=0)
            x1 = jnp.where(sub < 1, prev1, cur1)
            pair = w[1] * r3 + w[0] * x1
            cur2 = pltpu.roll(pair, 2, 1)
            if not seq_is_group:
                prev2 = jnp.concatenate([pair_last2, cur2[:g - 1]], axis=0)
                x_last1, pair_last2 = cur1[g - 1:], cur2[g - 1:]
            y = (w[3] * r3 + w[2] * x1) + jnp.where(sub < 2, prev2, cur2)
            conv_ref[rows, sl] = _silu(y).reshape(g * SUBLANES, MXU_COLS)

    conv = [functools.partial(conv_slab, s) for s in range(CONV_CH // MXU_COLS)]
    light = ([functools.partial(rope_slab, s) for s in range(2 * RET_QK // MXU_COLS)]
             + [functools.partial(v_slab, s) for s in range(RET_V // MXU_COLS)])
    order = []
    while conv or light:
        take = -(-len(conv) // max(len(light), 1))
        order += conv[:take] + light[:1]
        conv, light = conv[take:], light[1:]
    for slab in order:
        slab()


def _w_in_block(width, index):
    assert QKV_W + D_MODEL == CONV_CH and QKV_W == 2 * D_MODEL
    return pl.BlockSpec((D_MODEL, width), lambda i: (0, index), pipeline_mode=pl.Buffered(1))


def _in_proj(x2d, t, pos, conv0, wts):
    ln1, w_bf, conv_w, a_log, dt_bias = wts
    n = x2d.shape[0]
    nb = n // t
    seq_is_group = t == SUBLANES
    tm = min(256, n) if seq_is_group else min(512, t)
    assert n % tm == 0 and (seq_is_group or t % tm == 0), (n, t, tm)
    tiles_per_seq = max(t // tm, 1)
    inv = ROPE_BASE ** (-jnp.arange(0, RET_DK, 2, dtype=F32) / RET_DK)
    ang = jnp.tile(pos, max(tm // t, 1))[:, None] * inv[None, :]
    cosf = jnp.concatenate([jnp.cos(ang), jnp.cos(ang)], axis=-1)
    sinf = jnp.concatenate([-jnp.sin(ang), jnp.sin(ang)], axis=-1)
    pos_blocks = cosf.shape[0] // tm
    rows = lambda cols: pl.BlockSpec((tm, cols), lambda i: (i, 0))
    if seq_is_group:
        hist = conv0.transpose(1, 0, 2)
        hist_spec = pl.BlockSpec((CONV_W - 1, tm // SUBLANES, CONV_CH), lambda i: (0, i, 0))
        tail_spec = hist_spec
        tail_shape = jax.ShapeDtypeStruct(hist.shape, F32)
    else:
        hist = jnp.pad(conv0, ((0, 0), (SUBLANES - (CONV_W - 1), 0), (0, 0)))
        hist_spec = pl.BlockSpec((1, SUBLANES, CONV_CH), lambda i: (i // tiles_per_seq, 0, 0))
        tail_spec = pl.BlockSpec((1, SUBLANES, CONV_CH), lambda i: (i, 0, 0))
        tail_shape = jax.ShapeDtypeStruct((n // tm, SUBLANES, CONV_CH), F32)
    pad_lanes = lambda x: jnp.pad(x.reshape(1, GDN_HEADS), ((0, 0), (0, LANES - GDN_HEADS)))
    pos_spec = pl.BlockSpec((tm, RET_DK), lambda i: (i % pos_blocks, 0))
    qk, v, conv, gb, tail = pl.pallas_call(
        functools.partial(_in_proj_kernel, seq_is_group=seq_is_group, tiles_per_seq=tiles_per_seq),
        grid=(n // tm,),
        in_specs=[
            rows(D_MODEL),
            _const_spec((1, D_MODEL)),
            _w_in_block(QKV_W, 0),
            _w_in_block(CONV_CH, 1),
            _w_in_block(LANES, (QKV_W + D_MODEL + CONV_CH) // LANES),
            pos_spec,
            pos_spec,
            _const_spec((CONV_W, CONV_CH)),
            hist_spec,
            _const_spec((1, LANES)),
            _const_spec((1, LANES)),
        ],
        out_specs=[rows(2 * RET_QK), rows(RET_V), rows(CONV_CH), rows(AB_W), tail_spec],
        out_shape=[
            jax.ShapeDtypeStruct((n, 2 * RET_QK), BF16),
            jax.ShapeDtypeStruct((n, RET_V), BF16),
            jax.ShapeDtypeStruct((n, CONV_CH), F32),
            jax.ShapeDtypeStruct((n, AB_W), F32),
            tail_shape,
        ],
        scratch_shapes=[pltpu.VMEM((tm, D_MODEL), BF16), pltpu.VMEM((SUBLANES, CONV_CH), F32)],
        compiler_params=_params(("arbitrary",)),
        name="in_proj",
    )(x2d, ln1, w_bf, w_bf, w_bf, cosf, sinf, conv_w, hist, pad_lanes(a_log), pad_lanes(dt_bias))
    if seq_is_group:
        tail = tail.transpose(1, 0, 2)
    else:
        tail = tail.reshape(nb, tiles_per_seq, SUBLANES, CONV_CH)[:, -1, SUBLANES - (CONV_W - 1):, :]
    return qk, v, conv, gb, tail


def _step_layout(nb, t, c):
    ns = STEP_ROWS // c
    if ns == 1:
        par = max(p for p in (8, 4, 2, 1) if nb % p == 0)
        return nb, t, ns, par
    assert t == c and nb % ns == 0, (nb, t, c)
    return nb // ns, STEP_ROWS, ns, 1


def _retention_stages(q_ref, k_ref, v_ref, dintra_ref, dq_ref, dk_ref, dc_ref, o_ref, s_ref, *, par, ns, c):
    units = [(p, h) for p in range(par) for h in range(RET_HEADS)]
    ids = range(len(units))
    seqs = range(ns)
    rows = lambda x, j: x[j * c:(j + 1) * c]
    dk_sl = lambda h: slice(h * RET_DK, (h + 1) * RET_DK)
    dv_sl = lambda h: slice(h * RET_DV, (h + 1) * RET_DV)
    st = {}

    def scores():
        st["qb"] = [q_ref[p, :, dk_sl(h)] for p, h in units]
        kb = [k_ref[p, :, dk_sl(h)] for p, h in units]
        st["vb"] = [v_ref[p, :, dv_sl(h)] for p, h in units]
        st["scores"] = [_dot_nt(st["qb"][u], kb[u]) * dintra_ref[units[u][1]] for u in ids]
        st["kd"] = [kb[u].astype(F32) * dk_ref[units[u][1]] for u in ids]

    def state_read():
        qb = st["qb"]
        if ns == 1:
            st["s"] = [s_ref[p, h] for p, h in units]
            st["qs"] = [_dot(qb[u], _bf(st["s"][u])) for u in ids]
        else:
            q32 = [qb[u].astype(F32) for u in ids]
            st["s"] = [[s_ref[j, h] for j in seqs] for _, h in units]
            st["qs"] = [_rows([_dot(_bf(rows(q32[u], j)), _bf(st["s"][u][j])) for j in seqs]) for u in ids]

    def state_write():
        s_, qs, kd, vb, sc = st["s"], st["qs"], st["kd"], st["vb"], st["scores"]
        if ns == 1:
            upd = [_dot(_bf(_rows([kd[u].T, sc[u]])), vb[u]) for u in ids]
            for u, (p, h) in enumerate(units):
                o_ref[p, :, dv_sl(h)] = upd[u][RET_DK:] + qs[u] * dq_ref[h]
                s_ref[p, h] = s_[u] * dc_ref[h] + upd[u][:RET_DK]
        else:
            v32 = [vb[u].astype(F32) for u in ids]
            intra = [_dot(_bf(sc[u]), vb[u]) for u in ids]
            for u, (p, h) in enumerate(units):
                o_ref[p, :, dv_sl(h)] = intra[u] + qs[u] * dq_ref[h]
            ktv = [[_dot_tn(_bf(rows(kd[u], j)), _bf(rows(v32[u], j))) for j in seqs] for u in ids]
            for u, (p, h) in enumerate(units):
                for j in seqs:
                    s_ref[j, h] = s_[u][j] * dc_ref[h] + ktv[u][j]

    return [scores, state_read, state_write]


def _retention_consts(c):
    log_g = jnp.log1p(-jnp.exp2(-5.0 - jnp.arange(RET_HEADS, dtype=F32)))
    idx = jnp.arange(STEP_ROWS)
    same = (idx[:, None] // c) == (idx[None, :] // c)
    off = (idx % c).astype(F32)
    diff = off[:, None] - off[None, :]
    causal = same & (diff >= 0)
    d_intra = jnp.where(causal[None], jnp.exp(log_g[:, None, None] * jnp.where(causal, diff, 0.0)[None]), 0.0)
    d_q = jnp.broadcast_to(jnp.exp(log_g[:, None] * (off + 1.0)[None, :])[..., None], (RET_HEADS, STEP_ROWS, RET_DV))
    d_k = jnp.broadcast_to(jnp.exp(log_g[:, None] * (c - 1.0 - off)[None, :])[..., None],
                           (RET_HEADS, STEP_ROWS, RET_DK))
    d_c = jnp.broadcast_to(jnp.exp(log_g * c)[:, None, None], (RET_HEADS, 1, RET_DV))
    return d_intra, d_q, d_k, d_c


def _block_diag(x2):
    m = x2.shape[0]
    r = lax.broadcasted_iota(jnp.int32, (2 * m, 2 * m), 0)
    cc = lax.broadcasted_iota(jnp.int32, (2 * m, 2 * m), 1)
    return jnp.where((r < m) == (cc < m), _rows([x2, x2]), jnp.zeros((), x2.dtype))


def _solve_correction(a, c):
    ids = range(len(a))
    m = a[0].shape[0]
    n = [-x for x in a]
    ab = [_bf(x) for x in a]
    p = [_dot(ab[i], _block_diag(ab[i])) for i in ids]
    span = 2
    while span < c:
        pb = [_block_diag(_bf(x)) for x in p]
        last = 2 * span >= c
        if last:
            prod = [_dot(_bf(n[i]), pb[i]) for i in ids]
            n = [n[i] + p[i] + prod[i] for i in ids]
        else:
            prod = [_dot(_bf(_rows([n[i], p[i]])), pb[i]) for i in ids]
            n = [n[i] + p[i] + prod[i][:m] for i in ids]
            p = [prod[i][m:] for i in ids]
        span *= 2
    return n


def _recurrence_kernel(*refs, par, ns, c, zero_init):
    (qkv_ref, gb_ref, q_ref, k_ref, v_ref, dintra_ref, dq_ref, dk_ref, dc_ref), refs = refs[:9], refs[9:]
    init_refs, (o_ref, s_ref, oret_ref, sret_ref) = refs[:-4], refs[-4:]

    @pl.when(pl.program_id(1) == 0)
    def _():
        if zero_init:
            s_ref[...] = jnp.zeros_like(s_ref)
            sret_ref[...] = jnp.zeros_like(sret_ref)
        else:
            s_ref[...] = init_refs[0][...]
            sret_ref[...] = init_refs[1][...]

    retention = _retention_stages(q_ref, k_ref, v_ref, dintra_ref, dq_ref, dk_ref, dc_ref, oret_ref, sret_ref,
                                  par=par, ns=ns, c=c)
    m = STEP_ROWS

    def l2norm(x):
        return x * lax.rsqrt(jnp.sum(x * x, axis=-1, keepdims=True) + EPS)

    lane = lax.broadcasted_iota(jnp.int32, (m, LANES), 1)
    off = lax.broadcasted_iota(jnp.int32, (m, LANES), 0) % c
    r2 = lax.broadcasted_iota(jnp.int32, (m, 2 * m), 0)
    lane2 = lax.broadcasted_iota(jnp.int32, (m, 2 * m), 1)
    c2 = lane2 % m
    same = (r2 // c) == (c2 // c)
    tril = same & (r2 >= c2)
    strict = same & (r2 > c2)
    even = lane2 < m
    pick = (lax.broadcasted_iota(jnp.int32, (2 * m, 2 * GDN_DK), 0) < m) == (
        lax.broadcasted_iota(jnp.int32, (2 * m, 2 * GDN_DK), 1) < GDN_DK)

    def gates(p):
        cum = jnp.where(lane < GDN_HEADS, gb_ref[p, :, :LANES], 0.0)
        shift = 1
        while shift < c:
            cum = cum + jnp.where(off >= shift, pltpu.roll(cum, shift, 0), 0.0)
            shift *= 2
        last = cum.reshape(ns, c, LANES)[:, c - 1:c, :]
        e_rest = jnp.exp((last - cum.reshape(ns, c, LANES)).reshape(m, LANES))
        return dict(cum=cum, cum_t=cum.T, e_cum=jnp.exp(cum), e_rest=e_rest, e_last=jnp.exp(last),
                    beta=gb_ref[p, :, LANES:])

    def padded(x, h):
        z = jnp.zeros_like(x)
        return _rows([x, z]) if h % 2 == 0 else _rows([z, x])

    gt = [gates(p) for p in range(par)]
    units = [(p, h) for p in range(par) for h in range(GDN_HEADS)]
    ids = range(len(units))
    pairs = range(len(units) // 2)
    pair_of = lambda u: u // 2
    both = lambda xs, i: jnp.concatenate([xs[2 * i], xs[2 * i + 1]], axis=-1)
    col = lambda x, h: x[:, h:h + 1]
    q = [l2norm(qkv_ref[p, :, h * GDN_DK:(h + 1) * GDN_DK]) * (GDN_DK ** -0.5) for p, h in units]
    k = [l2norm(qkv_ref[p, :, GDN_QK + h * GDN_DK:GDN_QK + (h + 1) * GDN_DK]) for p, h in units]
    v = [qkv_ref[p, :, 2 * GDN_QK + h * GDN_DV:2 * GDN_QK + (h + 1) * GDN_DV] for p, h in units]
    beta = [col(gt[p]["beta"], BETA_LANE0 + h) for p, h in units]
    eg = [col(gt[p]["e_cum"], h) for p, h in units]
    cum_col = [col(gt[p]["cum"], h) for p, h in units]
    cum_row = [gt[p]["cum_t"][h:h + 1, :] for p, h in units]
    decay = [jnp.exp(jnp.where(tril, jnp.where(even, cum_col[2 * i], cum_col[2 * i + 1]) - both(cum_row, i), -jnp.inf))
             for i in pairs]
    kb = [k[u] * beta[u] for u in ids]
    kpick = [jnp.where(pick, _rows([_bf(both(k, i))] * 2), jnp.zeros((), BF16)) for i in pairs]
    raw = [_dot_nt(_bf(_rows([both(kb, i), both(q, i)])), kpick[i]) for i in pairs]
    a = [jnp.where(strict, raw[i][:m] * decay[i], 0.0) for i in pairs]
    attn = [raw[i][m:] * decay[i] for i in pairs]
    retention[0]()
    n = _solve_correction(a, c)
    retention[1]()
    rhs = [jnp.concatenate([v[u] * beta[u], kb[u] * eg[u]], axis=-1) for u in ids]
    nb16 = [_bf(x) for x in n]
    sol = [rhs[u] + _dot(nb16[pair_of(u)], padded(_bf(rhs[u]), units[u][1])) for u in ids]
    uu = [sol[u][:, :GDN_DV] for u in ids]
    w = [sol[u][:, GDN_DV:] for u in ids]
    qe = [q[u] * eg[u] for u in ids]
    kd = [k[u] * col(gt[p]["e_rest"], h) for u, (p, h) in enumerate(units)]
    retention[2]()
    dv_sl = lambda h: slice(h * GDN_DV, (h + 1) * GDN_DV)
    if ns == 1:
        s = [s_ref[p, h] for p, h in units]
        ws_qs = [_dot(_bf(_rows([w[u], qe[u]])), _bf(s[u])) for u in ids]
        v_new = [uu[u] - ws_qs[u][:m] for u in ids]
        upd = [_dot(_bf(_rows([padded(kd[u], h).T, attn[pair_of(u)]])), padded(_bf(v_new[u]), h))
               for u, (p, h) in enumerate(units)]
        for u, (p, h) in enumerate(units):
            o_ref[p, :, dv_sl(h)] = ws_qs[u][m:] + upd[u][GDN_DK:]
            s_ref[p, h] = s[u] * gt[p]["e_last"][0, :, h:h + 1] + upd[u][:GDN_DK]
    else:
        seqs = range(ns)
        rows = lambda x, j: x[j * c:(j + 1) * c]
        s = [[s_ref[j, h] for j in seqs] for _, h in units]
        ws_qs = [[_dot(_bf(_rows([rows(w[u], j), rows(qe[u], j)])), _bf(s[u][j])) for j in seqs] for u in ids]
        v_new = [_rows([rows(uu[u], j) - ws_qs[u][j][:c] for j in seqs]) for u in ids]
        qs = [_rows([ws_qs[u][j][c:] for j in seqs]) for u in ids]
        intra = [_dot(_bf(attn[pair_of(u)]), padded(_bf(v_new[u]), units[u][1])) for u in ids]
        for u, (p, h) in enumerate(units):
            o_ref[p, :, dv_sl(h)] = qs[u] + intra[u]
        ktv = [[_dot_tn(_bf(rows(kd[u], j)), _bf(rows(v_new[u], j))) for j in seqs] for u in ids]
        for u, (p, h) in enumerate(units):
            for j in seqs:
                s_ref[j, h] = s[u][j] * gt[p]["e_last"][j, :, h:h + 1] + ktv[u][j]


def _recurrences(qk, v, conv, gb, s_ret, s_gdn, nb, t, c):
    groups, rows, ns, par = _step_layout(nb, t, c)
    zero_init = s_ret is None
    blk = lambda cols, col_blk=0: pl.BlockSpec((par, STEP_ROWS, cols), lambda g, i: (g, i, col_blk))
    ret_state = pl.BlockSpec((par * ns, RET_HEADS, RET_DK, RET_DV), lambda g, i: (g, 0, 0, 0))
    gdn_state = pl.BlockSpec((par * ns, GDN_HEADS, GDN_DK, GDN_DV), lambda g, i: (g, 0, 0, 0))
    view = lambda x: x.reshape(groups, rows, x.shape[-1])
    qk3 = view(qk)
    o_g, gdn_new, o_r, ret_new = pl.pallas_call(
        functools.partial(_recurrence_kernel, par=par, ns=ns, c=c, zero_init=zero_init),
        grid=(groups // par, rows // STEP_ROWS),
        in_specs=[
            blk(CONV_CH), blk(AB_W), blk(RET_QK, 0), blk(RET_QK, 1), blk(RET_V),
            _const_spec((RET_HEADS, STEP_ROWS, STEP_ROWS)),
            _const_spec((RET_HEADS, STEP_ROWS, RET_DV)),
            _const_spec((RET_HEADS, STEP_ROWS, RET_DK)),
            _const_spec((RET_HEADS, 1, RET_DV)),
        ] + ([] if zero_init else [gdn_state, ret_state]),
        out_specs=[blk(GDN_V), gdn_state, blk(RET_V), ret_state],
        out_shape=[
            jax.ShapeDtypeStruct((groups, rows, GDN_V), F32),
            jax.ShapeDtypeStruct((nb, GDN_HEADS, GDN_DK, GDN_DV), F32),
            jax.ShapeDtypeStruct((groups, rows, RET_V), F32),
            jax.ShapeDtypeStruct((nb, RET_HEADS, RET_DK, RET_DV), F32),
        ],
        compiler_params=_params(("arbitrary", "arbitrary")),
        name="recurrences",
    )(view(conv), view(gb), qk3, qk3, view(v), *_retention_consts(c), *(() if zero_init else (s_gdn, s_ret)))
    return o_r.reshape(nb * t, RET_V), o_g.reshape(nb * t, GDN_V), ret_new, gdn_new


def _attn_out_kernel(x_ref, or_ref, og_ref, ln1_ref, wgr_ref, wzgg_ref, rn_ref, gn_ref, wr_ref, wg_ref, wo_ref,
                     o_ref, h_ref, lhs_ref, br_ref):
    tm = x_ref.shape[0]
    h_ref[...] = _bf(_rms(x_ref[...], ln1_ref[...]))
    slabs = [slice(s * MXU_COLS, (s + 1) * MXU_COLS) for s in range(D_MODEL // MXU_COLS)]
    row_blocks = [slice(b, b + EPILOGUE_ROWS) for b in range(0, tm, EPILOGUE_ROWS)]

    def gate(block, sl):
        if block == 0:
            return _dot(h_ref[...], wgr_ref[:, sl])
        return _dot(h_ref[...], wzgg_ref[:, (block - 1) * D_MODEL + sl.start:(block - 1) * D_MODEL + sl.stop])

    def branch(src_ref, head_dim, gain_ref, block, w_ref):
        for sl in slabs:
            g = gate(block, sl)
            for rows in row_blocks:
                o = src_ref[rows, sl]
                heads = [o[:, k:k + head_dim] for k in range(0, MXU_COLS, head_dim)]
                normed = [oh * lax.rsqrt(jnp.mean(oh * oh, axis=-1, keepdims=True) + EPS) for oh in heads]
                lhs_ref[rows, sl] = _bf(jnp.concatenate(normed, axis=-1) * gain_ref[:, sl] * _silu(g[rows]))
        return _dot(lhs_ref[...], w_ref[...])

    br_ref[0] = branch(or_ref, RET_DV, rn_ref, 0, wr_ref)
    br_ref[1] = branch(og_ref, GDN_DV, gn_ref, 1, wg_ref)
    for sl in slabs:
        g_r = gate(2, sl)
        g_g = gate(3, sl)
        for rows in row_blocks:
            lhs_ref[rows, sl] = _bf(jax.nn.sigmoid(g_r[rows]) * br_ref[0, rows, sl]
                                    + jax.nn.sigmoid(g_g[rows]) * br_ref[1, rows, sl])
    o_ref[...] = x_ref[...] + _dot(lhs_ref[...], wo_ref[...])


def _attn_out(x2d, o_r, o_g, ln1, w_bf, w_zgg, ret_norm, gdn_norm, w_ret_br, w_gdn_br, w_out):
    n = x2d.shape[0]
    tm = min(512, n)
    rows = pl.BlockSpec((tm, D_MODEL), lambda i: (i, 0))
    vec = _const_spec((1, D_MODEL))
    wspec = _const_spec((D_MODEL, D_MODEL))
    return pl.pallas_call(
        _attn_out_kernel,
        grid=(n // tm,),
        in_specs=[rows, rows, rows, vec, _w_in_block(D_MODEL, 2), _const_spec((D_MODEL, GATE_W - D_MODEL)), vec, vec,
                  wspec, wspec, wspec],
        out_specs=rows,
        out_shape=jax.ShapeDtypeStruct((n, D_MODEL), F32),
        scratch_shapes=[pltpu.VMEM((tm, D_MODEL), BF16), pltpu.VMEM((tm, D_MODEL), BF16),
                        pltpu.VMEM((2, tm, D_MODEL), F32)],
        compiler_params=_params(("arbitrary",)),
        name="attn_out",
    )(x2d, o_r, o_g, ln1, w_bf, w_zgg, ret_norm, gdn_norm, w_ret_br, w_gdn_br, w_out)


FF_BLOCK = 1024


def _mlp_kernel(x_ref, ln2_ref, wu_ref, wd_ref, lnf_ref, o_ref):
    x = x_ref[...]
    hb = _bf(_rms(x, ln2_ref[...]))
    acc = x
    for f in range(D_FF // FF_BLOCK):
        sl = slice(f * FF_BLOCK, (f + 1) * FF_BLOCK)
        up = jnp.maximum(_dot(hb, wu_ref[:, sl]), 0.0)
        acc = acc + _dot(_bf(up * up), wd_ref[sl, :])
    o_ref[...] = _rms(acc, lnf_ref[...])


def _mlp(x2d, ln2, w_up, w_down, ln_f):
    n = x2d.shape[0]
    tm = min(1024, n)
    rows = pl.BlockSpec((tm, D_MODEL), lambda i: (i, 0))
    return pl.pallas_call(
        _mlp_kernel,
        grid=(n // tm,),
        in_specs=[rows, _const_spec((1, D_MODEL)), _const_spec((D_MODEL, D_FF)), _const_spec((D_FF, D_MODEL)),
                  _const_spec((1, D_MODEL))],
        out_specs=rows,
        out_shape=jax.ShapeDtypeStruct((n, D_MODEL), F32),
        compiler_params=_params(("arbitrary",)),
        name="mlp",
    )(x2d, ln2, w_up, w_down, ln_f)


def _group(x, pos, s_ret, s_gdn, s_conv, wts):
    (in_wts, w_zgg, ret_norm, gdn_norm, w_ret_br, w_gdn_br, w_out, ln2, w_up, w_down, ln_f) = wts
    nb, t, _ = x.shape
    c = CHUNK if t % CHUNK == 0 else t
    assert STEP_ROWS % c == 0 and nb % (STEP_ROWS // c) == 0, (nb, t)
    x2d = x.reshape(nb * t, D_MODEL)
    qk, v, conv, gb, conv_new = _in_proj(x2d, t, pos, s_conv, in_wts)
    o_r, o_g, ret_new, gdn_new = _recurrences(qk, v, conv, gb, s_ret, s_gdn, nb, t, c)
    x1 = _attn_out(x2d, o_r, o_g, in_wts[0], in_wts[1], w_zgg, ret_norm, gdn_norm, w_ret_br, w_gdn_br, w_out)
    y = _mlp(x1, ln2, w_up, w_down, ln_f)
    return y.reshape(x.shape), ret_new, gdn_new, conv_new


def kernel(x_prompt, x_sample, state_ret, state_gdn, state_conv, ln1, w_in, conv_w, a_log, dt_bias, ret_norm,
           gdn_norm, w_ret_br, w_gdn_br, w_out, ln2, w_up, w_down, ln_f):
    depth = w_in.shape[0]
    assert depth == 1, "single-layer trunk"
    bp, tp, _ = x_prompt.shape
    ts = x_sample.shape[1]
    w = w_in[0]
    o_gr = 2 * RET_QK + RET_V
    o_qkv = o_gr + RET_V
    o_a = o_qkv + CONV_CH
    o_z = o_a + 2 * GDN_HEADS
    assert (o_gr, o_qkv) == (QKV_W, CONV_CH)
    w_bf = w.astype(BF16)
    w_zgg = w_bf[:, o_z:]
    vec = lambda v: v.reshape(1, -1)
    in_wts = (vec(ln1[0]), w_bf, conv_w[0], a_log[0], dt_bias[0])
    wts = (in_wts, w_zgg, vec(ret_norm[0]), vec(gdn_norm[0]), w_ret_br[0].astype(BF16), w_gdn_br[0].astype(BF16),
           w_out[0].astype(BF16), vec(ln2[0]), w_up[0].astype(BF16), w_down[0].astype(BF16), vec(ln_f))
    pos_p = jnp.arange(tp, dtype=F32)
    pos_s = PAST_LEN + jnp.arange(ts, dtype=F32)
    yp, ret_p, gdn_p, conv_p = _group(x_prompt, pos_p, None, None, jnp.zeros((bp, CONV_W - 1, CONV_CH), F32), wts)
    ys, ret_s, gdn_s, conv_s = _group(x_sample, pos_s, state_ret[0], state_gdn[0], state_conv[0], wts)
    return (yp, ys, ret_p[None], gdn_p[None], conv_p[None], ret_s[None], gdn_s[None], conv_s[None])
```

```python
import functools

import jax
import jax.numpy as jnp
from jax import lax
from jax.experimental import pallas as pl
from jax.experimental.pallas import tpu as pltpu

F32 = jnp.float32
BF16 = jnp.bfloat16

D_MODEL = 1024
PAST_LEN = 16384
RET_HEADS, RET_DK, RET_DV = 4, 128, 256
RET_QK = RET_HEADS * RET_DK
RET_V = RET_HEADS * RET_DV
GDN_HEADS, GDN_DK, GDN_DV = 8, 128, 128
GDN_QK = GDN_HEADS * GDN_DK
GDN_V = GDN_HEADS * GDN_DV
CONV_W = 4
CONV_CH = 2 * GDN_QK + GDN_V
D_FF = 4 * D_MODEL
CHUNK = 64
ROPE_BASE = 10000.0
EPS = 1e-6

SUBLANES = 8
LANES = 128
MXU_COLS = 256
STEP_ROWS = 64
EPILOGUE_ROWS = 64
VMEM_LIMIT = 56 * 1024 * 1024

QKV_W = 2 * RET_QK + RET_V
GATE_W = 4 * D_MODEL
AB_W = 2 * LANES
BETA_LANE0 = GDN_HEADS


def _params(sem):
    return pltpu.CompilerParams(dimension_semantics=sem, vmem_limit_bytes=VMEM_LIMIT)


def _const_spec(shape):
    zeros = (0,) * len(shape)
    return pl.BlockSpec(shape, lambda *_: zeros, pipeline_mode=pl.Buffered(1))


def _dot(a, b):
    return jnp.dot(a, b, preferred_element_type=F32)


def _dot_nt(a, b):
    return lax.dot_general(a, b, (((1,), (1,)), ((), ())), preferred_element_type=F32)


def _dot_tn(a, b):
    return lax.dot_general(a, b, (((0,), (0,)), ((), ())), preferred_element_type=F32)


def _bf(x):
    return x.astype(BF16)


def _rows(parts):
    return jnp.concatenate(parts, axis=0)


def _rms(x, g):
    return x * lax.rsqrt(jnp.mean(x * x, axis=-1, keepdims=True) + EPS) * g


def _silu(x):
    h = 0.5 * x
    return h + h * jnp.tanh(h)


def _in_proj_kernel(x0_ref, xn_ref, ln_ref, wqkv_ref, wconv_ref, wab_ref, cos_ref, sin_ref, cw_ref, hist_ref,
                    alog_ref, dtb_ref, qk_ref, v_ref, conv_ref, gb_ref, tail_ref, hout_ref, h2_ref, carry_ref,
                    *, seq_is_group, tiles_per_seq):
    i = pl.program_id(0)
    tm = x0_ref.shape[0]
    row_blocks = [slice(b, b + EPILOGUE_ROWS) for b in range(0, tm, EPILOGUE_ROWS)]

    @pl.when(i == 0)
    def _():
        carry_ref[...] = jnp.zeros_like(carry_ref)
        h2_ref[0] = _bf(_rms(x0_ref[...], ln_ref[...]))

    h_ref = h2_ref.at[i % 2]

    def norm_next():
        h2_ref[(i + 1) % 2] = _bf(_rms(xn_ref[...], ln_ref[...]))

    hout_ref[...] = h_ref[...]
    ab = _dot(h_ref[...], wab_ref[...])
    gb_ref[:, :LANES] = -jnp.exp(alog_ref[...]) * jax.nn.softplus(ab + dtb_ref[...])
    gb_ref[:, LANES:] = jax.nn.sigmoid(ab)

    def rope_slab(s):
        sl = slice(s * MXU_COLS, (s + 1) * MXU_COLS)
        r = _dot(h_ref[...], wqkv_ref[:, sl])
        for rows in row_blocks:
            for part in range(MXU_COLS // RET_DK):
                x = r[rows, part * RET_DK:(part + 1) * RET_DK]
                y = x * cos_ref[rows, :] + pltpu.roll(x, RET_DK // 2, 1) * sin_ref[rows, :]
                start = sl.start + part * RET_DK
                if start >= RET_QK:
                    y = y * (RET_DK ** -0.5)
                qk_ref[rows, start:start + RET_DK] = _bf(y)

    def v_slab(s):
        sl = slice(s * MXU_COLS, (s + 1) * MXU_COLS)
        v_ref[:, sl] = _bf(_dot(h_ref[...], wqkv_ref[:, 2 * RET_QK + sl.start:2 * RET_QK + sl.stop]))

    sub = lax.broadcasted_iota(jnp.int32, (1, SUBLANES, MXU_COLS), 1)

    def conv_slab(s):
        sl = slice(s * MXU_COLS, (s + 1) * MXU_COLS)
        w = [cw_ref[tap:tap + 1, sl][None] for tap in range(CONV_W)]
        r = _dot(h_ref[...], wconv_ref[:, sl])
        if not seq_is_group:
            tail_ref[:, :, sl] = r[tm - SUBLANES:][None]
            x_last = jnp.where(i % tiles_per_seq == 0, hist_ref[:, :, sl], carry_ref[:, sl][None])
            carry_ref[:, sl] = r[tm - SUBLANES:]
            x_last1 = pltpu.roll(x_last, 1, 1)
            pair_last2 = pltpu.roll(w[1] * x_last + w[0] * x_last1, 2, 1)
        for rows in row_blocks:
            g = (rows.stop - rows.start) // SUBLANES
            r3 = r[rows].reshape(g, SUBLANES, MXU_COLS)
            cur1 = pltpu.roll(r3, 1, 1)
            if seq_is_group:
                seqs = slice(rows.start // SUBLANES, rows.stop // SUBLANES)
                x3, x2, prev1 = (hist_ref[j, seqs, sl][:, None, :] for j in range(CONV_W - 1))
                prev2 = jnp.where(sub < 1, w[1] * x2 + w[0] * x3, w[1] * prev1 + w[0] * x2)
                for j in range(CONV_W - 1):
                    tail_ref[j, seqs, sl] = r3[:, SUBLANES - (CONV_W - 1) + j, :]
            else:
                prev1 = jnp.concatenate([x_last1, cur1[:g - 1]], axis=0)
            x1 = jnp.where(sub < 1, prev1, cur1)
            pair = w[1] * r3 + w[0] * x1
            cur2 = pltpu.roll(pair, 2, 1)
            if not seq_is_group:
                prev2 = jnp.concatenate([pair_last2, cur2[:g - 1]], axis=0)
                x_last1, pair_last2 = cur1[g - 1:], cur2[g - 1:]
            y = (w[3] * r3 + w[2] * x1) + jnp.where(sub < 2, prev2, cur2)
            conv_ref[rows, sl] = _silu(y).reshape(g * SUBLANES, MXU_COLS)

    conv = [functools.partial(conv_slab, s) for s in range(CONV_CH // MXU_COLS)]
    light = ([functools.partial(rope_slab, s) for s in range(2 * RET_QK // MXU_COLS)]
             + [functools.partial(v_slab, s) for s in range(RET_V // MXU_COLS)])
    order = []
    while conv or light:
        take = -(-len(conv) // max(len(light), 1))
        order += conv[:take] + light[:1]
        conv, light = conv[take:], light[1:]
    order.insert(len(order) // 2, norm_next)
    for slab in order:
        slab()


def _w_in_block(width, index):
    assert QKV_W + D_MODEL == CONV_CH and QKV_W == 2 * D_MODEL
    return pl.BlockSpec((D_MODEL, width), lambda i: (0, index), pipeline_mode=pl.Buffered(1))


def _in_proj(x2d, t, pos, conv0, wts):
    ln1, w_bf, conv_w, a_log, dt_bias = wts
    n = x2d.shape[0]
    nb = n // t
    seq_is_group = t == SUBLANES
    tm = min(256, n) if seq_is_group else min(512, t)
    assert n % tm == 0 and (seq_is_group or t % tm == 0), (n, t, tm)
    tiles_per_seq = max(t // tm, 1)
    inv = ROPE_BASE ** (-jnp.arange(0, RET_DK, 2, dtype=F32) / RET_DK)
    ang = jnp.tile(pos, max(tm // t, 1))[:, None] * inv[None, :]
    cosf = jnp.concatenate([jnp.cos(ang), jnp.cos(ang)], axis=-1)
    sinf = jnp.concatenate([-jnp.sin(ang), jnp.sin(ang)], axis=-1)
    pos_blocks = cosf.shape[0] // tm
    rows = lambda cols: pl.BlockSpec((tm, cols), lambda i: (i, 0))
    if seq_is_group:
        hist = conv0.transpose(1, 0, 2)
        hist_spec = pl.BlockSpec((CONV_W - 1, tm // SUBLANES, CONV_CH), lambda i: (0, i, 0))
        tail_spec = hist_spec
        tail_shape = jax.ShapeDtypeStruct(hist.shape, F32)
    else:
        hist = jnp.pad(conv0, ((0, 0), (SUBLANES - (CONV_W - 1), 0), (0, 0)))
        hist_spec = pl.BlockSpec((1, SUBLANES, CONV_CH), lambda i: (i // tiles_per_seq, 0, 0))
        tail_spec = pl.BlockSpec((1, SUBLANES, CONV_CH), lambda i: (i, 0, 0))
        tail_shape = jax.ShapeDtypeStruct((n // tm, SUBLANES, CONV_CH), F32)
    pad_lanes = lambda x: jnp.pad(x.reshape(1, GDN_HEADS), ((0, 0), (0, LANES - GDN_HEADS)))
    pos_spec = pl.BlockSpec((tm, RET_DK), lambda i: (i % pos_blocks, 0))
    last_tile = n // tm - 1
    qk, v, conv, gb, tail, h = pl.pallas_call(
        functools.partial(_in_proj_kernel, seq_is_group=seq_is_group, tiles_per_seq=tiles_per_seq),
        grid=(n // tm,),
        in_specs=[
            _const_spec((tm, D_MODEL)),
            pl.BlockSpec((tm, D_MODEL), lambda i: (jnp.minimum(i + 1, last_tile), 0)),
            _const_spec((1, D_MODEL)),
            _w_in_block(QKV_W, 0),
            _w_in_block(CONV_CH, 1),
            _w_in_block(LANES, (QKV_W + D_MODEL + CONV_CH) // LANES),
            pos_spec,
            pos_spec,
            _const_spec((CONV_W, CONV_CH)),
            hist_spec,
            _const_spec((1, LANES)),
            _const_spec((1, LANES)),
        ],
        out_specs=[rows(2 * RET_QK), rows(RET_V), rows(CONV_CH), rows(AB_W), tail_spec, rows(D_MODEL)],
        out_shape=[
            jax.ShapeDtypeStruct((n, 2 * RET_QK), BF16),
            jax.ShapeDtypeStruct((n, RET_V), BF16),
            jax.ShapeDtypeStruct((n, CONV_CH), F32),
            jax.ShapeDtypeStruct((n, AB_W), F32),
            tail_shape,
            jax.ShapeDtypeStruct((n, D_MODEL), BF16),
        ],
        scratch_shapes=[pltpu.VMEM((2, tm, D_MODEL), BF16), pltpu.VMEM((SUBLANES, CONV_CH), F32)],
        compiler_params=_params(("arbitrary",)),
        name="in_proj",
    )(x2d, x2d, ln1, w_bf, w_bf, w_bf, cosf, sinf, conv_w, hist, pad_lanes(a_log), pad_lanes(dt_bias))
    if seq_is_group:
        tail = tail.transpose(1, 0, 2)
    else:
        tail = tail.reshape(nb, tiles_per_seq, SUBLANES, CONV_CH)[:, -1, SUBLANES - (CONV_W - 1):, :]
    return qk, v, conv, gb, tail, h


def _step_layout(nb, t, c):
    ns = STEP_ROWS // c
    if ns == 1:
        par = max(p for p in (8, 4, 2, 1) if nb % p == 0)
        return nb, t, ns, par
    assert t == c and nb % ns == 0, (nb, t, c)
    return nb // ns, STEP_ROWS, ns, 1


def _retention_stages(q_ref, k_ref, v_ref, dintra_ref, dq_ref, dk_ref, dc_ref, o_ref, s_ref, *, par, ns, c):
    units = [(p, h) for p in range(par) for h in range(RET_HEADS)]
    ids = range(len(units))
    seqs = range(ns)
    rows = lambda x, j: x[j * c:(j + 1) * c]
    dk_sl = lambda h: slice(h * RET_DK, (h + 1) * RET_DK)
    dv_sl = lambda h: slice(h * RET_DV, (h + 1) * RET_DV)
    st = {}

    def scores():
        st["qb"] = [q_ref[p, :, dk_sl(h)] for p, h in units]
        kb = [k_ref[p, :, dk_sl(h)] for p, h in units]
        st["vb"] = [v_ref[p, :, dv_sl(h)] for p, h in units]
        st["scores"] = [_dot_nt(st["qb"][u], kb[u]) * dintra_ref[units[u][1]] for u in ids]
        st["kd"] = [kb[u].astype(F32) * dk_ref[units[u][1]] for u in ids]

    def state_read():
        qb = st["qb"]
        if ns == 1:
            st["s"] = [s_ref[p, h] for p, h in units]
            st["qs"] = [_dot(qb[u], _bf(st["s"][u])) for u in ids]
        else:
            q32 = [qb[u].astype(F32) for u in ids]
            st["s"] = [[s_ref[j, h] for j in seqs] for _, h in units]
            st["qs"] = [_rows([_dot(_bf(rows(q32[u], j)), _bf(st["s"][u][j])) for j in seqs]) for u in ids]

    def state_write():
        s_, qs, kd, vb, sc = st["s"], st["qs"], st["kd"], st["vb"], st["scores"]
        if ns == 1:
            upd = [_dot(_bf(_rows([kd[u].T, sc[u]])), vb[u]) for u in ids]
            for u, (p, h) in enumerate(units):
                o_ref[p, :, dv_sl(h)] = upd[u][RET_DK:] + qs[u] * dq_ref[h]
                s_ref[p, h] = s_[u] * dc_ref[h] + upd[u][:RET_DK]
        else:
            v32 = [vb[u].astype(F32) for u in ids]
            intra = [_dot(_bf(sc[u]), vb[u]) for u in ids]
            for u, (p, h) in enumerate(units):
                o_ref[p, :, dv_sl(h)] = intra[u] + qs[u] * dq_ref[h]
            ktv = [[_dot_tn(_bf(rows(kd[u], j)), _bf(rows(v32[u], j))) for j in seqs] for u in ids]
            for u, (p, h) in enumerate(units):
                for j in seqs:
                    s_ref[j, h] = s_[u][j] * dc_ref[h] + ktv[u][j]

    return [scores, state_read, state_write]


def _retention_consts(c):
    log_g = jnp.log1p(-jnp.exp2(-5.0 - jnp.arange(RET_HEADS, dtype=F32)))
    idx = jnp.arange(STEP_ROWS)
    same = (idx[:, None] // c) == (idx[None, :] // c)
    off = (idx % c).astype(F32)
    diff = off[:, None] - off[None, :]
    causal = same & (diff >= 0)
    d_intra = jnp.where(causal[None], jnp.exp(log_g[:, None, None] * jnp.where(causal, diff, 0.0)[None]), 0.0)
    d_q = jnp.broadcast_to(jnp.exp(log_g[:, None] * (off + 1.0)[None, :])[..., None], (RET_HEADS, STEP_ROWS, RET_DV))
    d_k = jnp.broadcast_to(jnp.exp(log_g[:, None] * (c - 1.0 - off)[None, :])[..., None],
                           (RET_HEADS, STEP_ROWS, RET_DK))
    d_c = jnp.broadcast_to(jnp.exp(log_g * c)[:, None, None], (RET_HEADS, 1, RET_DV))
    return d_intra, d_q, d_k, d_c


def _block_diag(x2):
    m = x2.shape[0]
    r = lax.broadcasted_iota(jnp.int32, (2 * m, 2 * m), 0)
    cc = lax.broadcasted_iota(jnp.int32, (2 * m, 2 * m), 1)
    return jnp.where((r < m) == (cc < m), _rows([x2, x2]), jnp.zeros((), x2.dtype))


def _solve_correction(a, c):
    ids = range(len(a))
    m = a[0].shape[0]
    n = [-x for x in a]
    ab = [_bf(x) for x in a]
    p = [_dot(ab[i], _block_diag(ab[i])) for i in ids]
    span = 2
    while span < c:
        pb = [_block_diag(_bf(x)) for x in p]
        last = 2 * span >= c
        if last:
            prod = [_dot(_bf(n[i]), pb[i]) for i in ids]
            n = [n[i] + p[i] + prod[i] for i in ids]
        else:
            prod = [_dot(_bf(_rows([n[i], p[i]])), pb[i]) for i in ids]
            n = [n[i] + p[i] + prod[i][:m] for i in ids]
            p = [prod[i][m:] for i in ids]
        span *= 2
    return n


def _recurrence_kernel(*refs, par, ns, c, zero_init):
    (qkv_ref, gb_ref, q_ref, k_ref, v_ref, dintra_ref, dq_ref, dk_ref, dc_ref), refs = refs[:9], refs[9:]
    init_refs, (o_ref, s_ref, oret_ref, sret_ref) = refs[:-4], refs[-4:]

    @pl.when(pl.program_id(1) == 0)
    def _():
        if zero_init:
            s_ref[...] = jnp.zeros_like(s_ref)
            sret_ref[...] = jnp.zeros_like(sret_ref)
        else:
            s_ref[...] = init_refs[0][...]
            sret_ref[...] = init_refs[1][...]

    retention = _retention_stages(q_ref, k_ref, v_ref, dintra_ref, dq_ref, dk_ref, dc_ref, oret_ref, sret_ref,
                                  par=par, ns=ns, c=c)
    m = STEP_ROWS

    def l2norm(x):
        return x * lax.rsqrt(jnp.sum(x * x, axis=-1, keepdims=True) + EPS)

    lane = lax.broadcasted_iota(jnp.int32, (m, LANES), 1)
    off = lax.broadcasted_iota(jnp.int32, (m, LANES), 0) % c
    r2 = lax.broadcasted_iota(jnp.int32, (m, 2 * m), 0)
    lane2 = lax.broadcasted_iota(jnp.int32, (m, 2 * m), 1)
    c2 = lane2 % m
    same = (r2 // c) == (c2 // c)
    tril = same & (r2 >= c2)
    strict = same & (r2 > c2)
    even = lane2 < m
    pick = (lax.broadcasted_iota(jnp.int32, (2 * m, 2 * GDN_DK), 0) < m) == (
        lax.broadcasted_iota(jnp.int32, (2 * m, 2 * GDN_DK), 1) < GDN_DK)

    def gates(p):
        cum = jnp.where(lane < GDN_HEADS, gb_ref[p, :, :LANES], 0.0)
        shift = 1
        while shift < c:
            cum = cum + jnp.where(off >= shift, pltpu.roll(cum, shift, 0), 0.0)
            shift *= 2
        last = cum.reshape(ns, c, LANES)[:, c - 1:c, :]
        e_rest = jnp.exp((last - cum.reshape(ns, c, LANES)).reshape(m, LANES))
        return dict(cum=cum, cum_t=cum.T, e_cum=jnp.exp(cum), e_rest=e_rest, e_last=jnp.exp(last),
                    beta=gb_ref[p, :, LANES:])

    def padded(x, h):
        z = jnp.zeros_like(x)
        return _rows([x, z]) if h % 2 == 0 else _rows([z, x])

    gt = [gates(p) for p in range(par)]
    units = [(p, h) for p in range(par) for h in range(GDN_HEADS)]
    ids = range(len(units))
    pairs = range(len(units) // 2)
    pair_of = lambda u: u // 2
    both = lambda xs, i: jnp.concatenate([xs[2 * i], xs[2 * i + 1]], axis=-1)
    col = lambda x, h: x[:, h:h + 1]
    q = [l2norm(qkv_ref[p, :, h * GDN_DK:(h + 1) * GDN_DK]) * (GDN_DK ** -0.5) for p, h in units]
    k = [l2norm(qkv_ref[p, :, GDN_QK + h * GDN_DK:GDN_QK + (h + 1) * GDN_DK]) for p, h in units]
    v = [qkv_ref[p, :, 2 * GDN_QK + h * GDN_DV:2 * GDN_QK + (h + 1) * GDN_DV] for p, h in units]
    beta = [col(gt[p]["beta"], BETA_LANE0 + h) for p, h in units]
    eg = [col(gt[p]["e_cum"], h) for p, h in units]
    cum_col = [col(gt[p]["cum"], h) for p, h in units]
    cum_row = [gt[p]["cum_t"][h:h + 1, :] for p, h in units]
    decay = [jnp.exp(jnp.where(tril, jnp.where(even, cum_col[2 * i], cum_col[2 * i + 1]) - both(cum_row, i), -jnp.inf))
             for i in pairs]
    kb = [k[u] * beta[u] for u in ids]
    kpick = [jnp.where(pick, _rows([_bf(both(k, i))] * 2), jnp.zeros((), BF16)) for i in pairs]
    raw = [_dot_nt(_bf(_rows([both(kb, i), both(q, i)])), kpick[i]) for i in pairs]
    a = [jnp.where(strict, raw[i][:m] * decay[i], 0.0) for i in pairs]
    attn = [raw[i][m:] * decay[i] for i in pairs]
    retention[0]()
    n = _solve_correction(a, c)
    retention[1]()
    rhs = [jnp.concatenate([v[u] * beta[u], kb[u] * eg[u]], axis=-1) for u in ids]
    nb16 = [_bf(x) for x in n]
    sol = [rhs[u] + _dot(nb16[pair_of(u)], padded(_bf(rhs[u]), units[u][1])) for u in ids]
    uu = [sol[u][:, :GDN_DV] for u in ids]
    w = [sol[u][:, GDN_DV:] for u in ids]
    qe = [q[u] * eg[u] for u in ids]
    kd = [k[u] * col(gt[p]["e_rest"], h) for u, (p, h) in enumerate(units)]
    retention[2]()
    dv_sl = lambda h: slice(h * GDN_DV, (h + 1) * GDN_DV)
    if ns == 1:
        s = [s_ref[p, h] for p, h in units]
        ws_qs = [_dot(_bf(_rows([w[u], qe[u]])), _bf(s[u])) for u in ids]
        v_new = [uu[u] - ws_qs[u][:m] for u in ids]
        upd = [_dot(_bf(_rows([padded(kd[u], h).T, attn[pair_of(u)]])), padded(_bf(v_new[u]), h))
               for u, (p, h) in enumerate(units)]
        for u, (p, h) in enumerate(units):
            o_ref[p, :, dv_sl(h)] = ws_qs[u][m:] + upd[u][GDN_DK:]
            s_ref[p, h] = s[u] * gt[p]["e_last"][0, :, h:h + 1] + upd[u][:GDN_DK]
    else:
        seqs = range(ns)
        rows = lambda x, j: x[j * c:(j + 1) * c]
        s = [[s_ref[j, h] for j in seqs] for _, h in units]
        ws_qs = [[_dot(_bf(_rows([rows(w[u], j), rows(qe[u], j)])), _bf(s[u][j])) for j in seqs] for u in ids]
        v_new = [_rows([rows(uu[u], j) - ws_qs[u][j][:c] for j in seqs]) for u in ids]
        qs = [_rows([ws_qs[u][j][c:] for j in seqs]) for u in ids]
        intra = [_dot(_bf(attn[pair_of(u)]), padded(_bf(v_new[u]), units[u][1])) for u in ids]
        for u, (p, h) in enumerate(units):
            o_ref[p, :, dv_sl(h)] = qs[u] + intra[u]
        ktv = [[_dot_tn(_bf(rows(kd[u], j)), _bf(rows(v_new[u], j))) for j in seqs] for u in ids]
        for u, (p, h) in enumerate(units):
            for j in seqs:
                s_ref[j, h] = s[u][j] * gt[p]["e_last"][j, :, h:h + 1] + ktv[u][j]


def _recurrences(qk, v, conv, gb, s_ret, s_gdn, nb, t, c):
    groups, rows, ns, par = _step_layout(nb, t, c)
    zero_init = s_ret is None
    blk = lambda cols, col_blk=0: pl.BlockSpec((par, STEP_ROWS, cols), lambda g, i: (g, i, col_blk))
    ret_state = pl.BlockSpec((par * ns, RET_HEADS, RET_DK, RET_DV), lambda g, i: (g, 0, 0, 0))
    gdn_state = pl.BlockSpec((par * ns, GDN_HEADS, GDN_DK, GDN_DV), lambda g, i: (g, 0, 0, 0))
    view = lambda x: x.reshape(groups, rows, x.shape[-1])
    qk3 = view(qk)
    o_g, gdn_new, o_r, ret_new = pl.pallas_call(
        functools.partial(_recurrence_kernel, par=par, ns=ns, c=c, zero_init=zero_init),
        grid=(groups // par, rows // STEP_ROWS),
        in_specs=[
            blk(CONV_CH), blk(AB_W), blk(RET_QK, 0), blk(RET_QK, 1), blk(RET_V),
            _const_spec((RET_HEADS, STEP_ROWS, STEP_ROWS)),
            _const_spec((RET_HEADS, STEP_ROWS, RET_DV)),
            _const_spec((RET_HEADS, STEP_ROWS, RET_DK)),
            _const_spec((RET_HEADS, 1, RET_DV)),
        ] + ([] if zero_init else [gdn_state, ret_state]),
        out_specs=[blk(GDN_V), gdn_state, blk(RET_V), ret_state],
        out_shape=[
            jax.ShapeDtypeStruct((groups, rows, GDN_V), F32),
            jax.ShapeDtypeStruct((nb, GDN_HEADS, GDN_DK, GDN_DV), F32),
            jax.ShapeDtypeStruct((groups, rows, RET_V), F32),
            jax.ShapeDtypeStruct((nb, RET_HEADS, RET_DK, RET_DV), F32),
        ],
        compiler_params=_params(("arbitrary", "arbitrary")),
        name="recurrences",
    )(view(conv), view(gb), qk3, qk3, view(v), *_retention_consts(c), *(() if zero_init else (s_gdn, s_ret)))
    return o_r.reshape(nb * t, RET_V), o_g.reshape(nb * t, GDN_V), ret_new, gdn_new


def _attn_out_kernel(x_ref, h_ref, or_ref, og_ref, wgr_ref, wzgg_ref, rn_ref, gn_ref, wr_ref, wg_ref, wo_ref,
                     o_ref, lhs_ref, br_ref):
    tm = x_ref.shape[0]
    slabs = [slice(s * MXU_COLS, (s + 1) * MXU_COLS) for s in range(D_MODEL // MXU_COLS)]
    row_blocks = [slice(b, b + EPILOGUE_ROWS) for b in range(0, tm, EPILOGUE_ROWS)]

    def gate(block, sl):
        if block == 0:
            return _dot(h_ref[...], wgr_ref[:, sl])
        return _dot(h_ref[...], wzgg_ref[:, (block - 1) * D_MODEL + sl.start:(block - 1) * D_MODEL + sl.stop])

    def branch(src_ref, head_dim, gain_ref, block, w_ref):
        for sl in slabs:
            g = gate(block, sl)
            for rows in row_blocks:
                o = src_ref[rows, sl]
                heads = [o[:, k:k + head_dim] for k in range(0, MXU_COLS, head_dim)]
                normed = [oh * lax.rsqrt(jnp.mean(oh * oh, axis=-1, keepdims=True) + EPS) for oh in heads]
                lhs_ref[rows, sl] = _bf(jnp.concatenate(normed, axis=-1) * gain_ref[:, sl] * _silu(g[rows]))
        return _dot(lhs_ref[...], w_ref[...])

    br_ref[0] = branch(or_ref, RET_DV, rn_ref, 0, wr_ref)
    br_ref[1] = branch(og_ref, GDN_DV, gn_ref, 1, wg_ref)
    for sl in slabs:
        g_r = gate(2, sl)
        g_g = gate(3, sl)
        for rows in row_blocks:
            lhs_ref[rows, sl] = _bf(jax.nn.sigmoid(g_r[rows]) * br_ref[0, rows, sl]
                                    + jax.nn.sigmoid(g_g[rows]) * br_ref[1, rows, sl])
    o_ref[...] = x_ref[...] + _dot(lhs_ref[...], wo_ref[...])


def _attn_out(x2d, h, o_r, o_g, w_bf, w_zgg, ret_norm, gdn_norm, w_ret_br, w_gdn_br, w_out):
    n = x2d.shape[0]
    tm = min(512, n)
    rows = pl.BlockSpec((tm, D_MODEL), lambda i: (i, 0))
    vec = _const_spec((1, D_MODEL))
    wspec = _const_spec((D_MODEL, D_MODEL))
    return pl.pallas_call(
        _attn_out_kernel,
        grid=(n // tm,),
        in_specs=[rows, rows, rows, rows, _w_in_block(D_MODEL, 2), _const_spec((D_MODEL, GATE_W - D_MODEL)), vec, vec,
                  wspec, wspec, wspec],
        out_specs=rows,
        out_shape=jax.ShapeDtypeStruct((n, D_MODEL), F32),
        scratch_shapes=[pltpu.VMEM((tm, D_MODEL), BF16), pltpu.VMEM((2, tm, D_MODEL), F32)],
        compiler_params=_params(("arbitrary",)),
        name="attn_out",
    )(x2d, h, o_r, o_g, w_bf, w_zgg, ret_norm, gdn_norm, w_ret_br, w_gdn_br, w_out)


FF_BLOCK = 1024


def _mlp_kernel(x_ref, ln2_ref, wu_ref, wd_ref, lnf_ref, o_ref):
    x = x_ref[...]
    hb = _bf(_rms(x, ln2_ref[...]))
    acc = x
    for f in range(D_FF // FF_BLOCK):
        sl = slice(f * FF_BLOCK, (f + 1) * FF_BLOCK)
        up = jnp.maximum(_dot(hb, wu_ref[:, sl]), 0.0)
        acc = acc + _dot(_bf(up * up), wd_ref[sl, :])
    o_ref[...] = _rms(acc, lnf_ref[...])


def _mlp(x2d, ln2, w_up, w_down, ln_f):
    n = x2d.shape[0]
    tm = min(1024, n)
    rows = pl.BlockSpec((tm, D_MODEL), lambda i: (i, 0))
    return pl.pallas_call(
        _mlp_kernel,
        grid=(n // tm,),
        in_specs=[rows, _const_spec((1, D_MODEL)), _const_spec((D_MODEL, D_FF)), _const_spec((D_FF, D_MODEL)),
                  _const_spec((1, D_MODEL))],
        out_specs=rows,
        out_shape=jax.ShapeDtypeStruct((n, D_MODEL), F32),
        compiler_params=_params(("arbitrary",)),
        name="mlp",
    )(x2d, ln2, w_up, w_down, ln_f)


def _group(x, pos, s_ret, s_gdn, s_conv, wts):
    (in_wts, w_zgg, ret_norm, gdn_norm, w_ret_br, w_gdn_br, w_out, ln2, w_up, w_down, ln_f) = wts
    nb, t, _ = x.shape
    c = CHUNK if t % CHUNK == 0 else t
    assert STEP_ROWS % c == 0 and nb % (STEP_ROWS // c) == 0, (nb, t)
    x2d = x.reshape(nb * t, D_MODEL)
    qk, v, conv, gb, conv_new, h = _in_proj(x2d, t, pos, s_conv, in_wts)
    o_r, o_g, ret_new, gdn_new = _recurrences(qk, v, conv, gb, s_ret, s_gdn, nb, t, c)
    x1 = _attn_out(x2d, h, o_r, o_g, in_wts[1], w_zgg, ret_norm, gdn_norm, w_ret_br, w_gdn_br, w_out)
    y = _mlp(x1, ln2, w_up, w_down, ln_f)
    return y.reshape(x.shape), ret_new, gdn_new, conv_new


def kernel(x_prompt, x_sample, state_ret, state_gdn, state_conv, ln1, w_in, conv_w, a_log, dt_bias, ret_norm,
           gdn_norm, w_ret_br, w_gdn_br, w_out, ln2, w_up, w_down, ln_f):
    depth = w_in.shape[0]
    assert depth == 1, "single-layer trunk"
    bp, tp, _ = x_prompt.shape
    ts = x_sample.shape[1]
    w = w_in[0]
    o_gr = 2 * RET_QK + RET_V
    o_qkv = o_gr + RET_V
    o_a = o_qkv + CONV_CH
    o_z = o_a + 2 * GDN_HEADS
    assert (o_gr, o_qkv) == (QKV_W, CONV_CH)
    w_bf = w.astype(BF16)
    w_zgg = w_bf[:, o_z:]
    vec = lambda v: v.reshape(1, -1)
    in_wts = (vec(ln1[0]), w_bf, conv_w[0], a_log[0], dt_bias[0])
    wts = (in_wts, w_zgg, vec(ret_norm[0]), vec(gdn_norm[0]), w_ret_br[0].astype(BF16), w_gdn_br[0].astype(BF16),
           w_out[0].astype(BF16), vec(ln2[0]), w_up[0].astype(BF16), w_down[0].astype(BF16), vec(ln_f))
    pos_p = jnp.arange(tp, dtype=F32)
    pos_s = PAST_LEN + jnp.arange(ts, dtype=F32)
    yp, ret_p, gdn_p, conv_p = _group(x_prompt, pos_p, None, None, jnp.zeros((bp, CONV_W - 1, CONV_CH), F32), wts)
    ys, ret_s, gdn_s, conv_s = _group(x_sample, pos_s, state_ret[0], state_gdn[0], state_conv[0], wts)
    return (yp, ys, ret_p[None], gdn_p[None], conv_p[None], ret_s[None], gdn_s[None], conv_s[None])
```

```python
import functools

import jax
import jax.numpy as jnp
from jax import lax
from jax.experimental import pallas as pl
from jax.experimental.pallas import tpu as pltpu

F32 = jnp.float32
BF16 = jnp.bfloat16

D_MODEL = 1024
PAST_LEN = 16384
RET_HEADS, RET_DK, RET_DV = 4, 128, 256
RET_QK = RET_HEADS * RET_DK
RET_V = RET_HEADS * RET_DV
GDN_HEADS, GDN_DK, GDN_DV = 8, 128, 128
GDN_QK = GDN_HEADS * GDN_DK
GDN_V = GDN_HEADS * GDN_DV
CONV_W = 4
CONV_CH = 2 * GDN_QK + GDN_V
D_FF = 4 * D_MODEL
CHUNK = 64
ROPE_BASE = 10000.0
EPS = 1e-6

SUBLANES = 8
LANES = 128
MXU_COLS = 256
STEP_ROWS = 64
EPILOGUE_ROWS = 64
VMEM_LIMIT = 56 * 1024 * 1024

QKV_W = 2 * RET_QK + RET_V
GATE_W = 4 * D_MODEL
AB_W = 2 * LANES
BETA_LANE0 = GDN_HEADS


def _params(sem):
    return pltpu.CompilerParams(dimension_semantics=sem, vmem_limit_bytes=VMEM_LIMIT)


def _const_spec(shape):
    zeros = (0,) * len(shape)
    return pl.BlockSpec(shape, lambda *_: zeros, pipeline_mode=pl.Buffered(1))


def _dot(a, b):
    return jnp.dot(a, b, preferred_element_type=F32)


def _dot_nt(a, b):
    return lax.dot_general(a, b, (((1,), (1,)), ((), ())), preferred_element_type=F32)


def _dot_tn(a, b):
    return lax.dot_general(a, b, (((0,), (0,)), ((), ())), preferred_element_type=F32)


def _bf(x):
    return x.astype(BF16)


def _rows(parts):
    return jnp.concatenate(parts, axis=0)


def _rms(x, g):
    return x * lax.rsqrt(jnp.mean(x * x, axis=-1, keepdims=True) + EPS) * g


def _silu(x):
    h = 0.5 * x
    return h + h * jnp.tanh(h)


def _in_proj_kernel(x_ref, ln_ref, wqkv_ref, wconv_ref, wab_ref, cos_ref, sin_ref, cw_ref, hist_ref,
                    alog_ref, dtb_ref, qk_ref, v_ref, conv_ref, gb_ref, tail_ref, hout_ref, h_ref, carry_ref,
                    *, seq_is_group, tiles_per_seq):
    i = pl.program_id(0)
    tm = x_ref.shape[0]
    row_blocks = [slice(b, b + EPILOGUE_ROWS) for b in range(0, tm, EPILOGUE_ROWS)]

    @pl.when(i == 0)
    def _():
        carry_ref[...] = jnp.zeros_like(carry_ref)

    h_ref[...] = _bf(_rms(x_ref[...], ln_ref[...]))
    hout_ref[...] = h_ref[...]
    ab = _dot(h_ref[...], wab_ref[...])
    gb_ref[:, :LANES] = -jnp.exp(alog_ref[...]) * jax.nn.softplus(ab + dtb_ref[...])
    gb_ref[:, LANES:] = jax.nn.sigmoid(ab)

    def rope_slab(s):
        sl = slice(s * MXU_COLS, (s + 1) * MXU_COLS)
        r = _dot(h_ref[...], wqkv_ref[:, sl])
        for rows in row_blocks:
            for part in range(MXU_COLS // RET_DK):
                x = r[rows, part * RET_DK:(part + 1) * RET_DK]
                y = x * cos_ref[rows, :] + pltpu.roll(x, RET_DK // 2, 1) * sin_ref[rows, :]
                start = sl.start + part * RET_DK
                if start >= RET_QK:
                    y = y * (RET_DK ** -0.5)
                qk_ref[rows, start:start + RET_DK] = _bf(y)

    def v_slab(s):
        sl = slice(s * MXU_COLS, (s + 1) * MXU_COLS)
        v_ref[:, sl] = _bf(_dot(h_ref[...], wqkv_ref[:, 2 * RET_QK + sl.start:2 * RET_QK + sl.stop]))

    sub = lax.broadcasted_iota(jnp.int32, (1, SUBLANES, MXU_COLS), 1)

    def conv_slab(s):
        sl = slice(s * MXU_COLS, (s + 1) * MXU_COLS)
        w = [cw_ref[tap:tap + 1, sl][None] for tap in range(CONV_W)]
        r = _dot(h_ref[...], wconv_ref[:, sl])
        if not seq_is_group:
            tail_ref[:, :, sl] = r[tm - SUBLANES:][None]
            x_last = jnp.where(i % tiles_per_seq == 0, hist_ref[:, :, sl], carry_ref[:, sl][None])
            carry_ref[:, sl] = r[tm - SUBLANES:]
            x_last1 = pltpu.roll(x_last, 1, 1)
            pair_last2 = pltpu.roll(w[1] * x_last + w[0] * x_last1, 2, 1)
        for rows in row_blocks:
            g = (rows.stop - rows.start) // SUBLANES
            r3 = r[rows].reshape(g, SUBLANES, MXU_COLS)
            cur1 = pltpu.roll(r3, 1, 1)
            if seq_is_group:
                seqs = slice(rows.start // SUBLANES, rows.stop // SUBLANES)
                x3, x2, prev1 = (hist_ref[j, seqs, sl][:, None, :] for j in range(CONV_W - 1))
                prev2 = jnp.where(sub < 1, w[1] * x2 + w[0] * x3, w[1] * prev1 + w[0] * x2)
                for j in range(CONV_W - 1):
                    tail_ref[j, seqs, sl] = r3[:, SUBLANES - (CONV_W - 1) + j, :]
            else:
                prev1 = jnp.concatenate([x_last1, cur1[:g - 1]], axis=0)
            x1 = jnp.where(sub < 1, prev1, cur1)
            pair = w[1] * r3 + w[0] * x1
            cur2 = pltpu.roll(pair, 2, 1)
            if not seq_is_group:
                prev2 = jnp.concatenate([pair_last2, cur2[:g - 1]], axis=0)
                x_last1, pair_last2 = cur1[g - 1:], cur2[g - 1:]
            y = (w[3] * r3 + w[2] * x1) + jnp.where(sub < 2, prev2, cur2)
            conv_ref[rows, sl] = _silu(y).reshape(g * SUBLANES, MXU_COLS)

    conv = [functools.partial(conv_slab, s) for s in range(CONV_CH // MXU_COLS)]
    light = ([functools.partial(rope_slab, s) for s in range(2 * RET_QK // MXU_COLS)]
             + [functools.partial(v_slab, s) for s in range(RET_V // MXU_COLS)])
    order = []
    while conv or light:
        take = -(-len(conv) // max(len(light), 1))
        order += conv[:take] + light[:1]
        conv, light = conv[take:], light[1:]
    for slab in order:
        slab()


def _w_in_block(width, index):
    assert QKV_W + D_MODEL == CONV_CH and QKV_W == 2 * D_MODEL
    return pl.BlockSpec((D_MODEL, width), lambda i: (0, index), pipeline_mode=pl.Buffered(1))


def _in_proj(x2d, t, pos, conv0, wts):
    ln1, w_bf, conv_w, a_log, dt_bias = wts
    n = x2d.shape[0]
    nb = n // t
    seq_is_group = t == SUBLANES
    tm = min(256, n) if seq_is_group else min(512, t)
    assert n % tm == 0 and (seq_is_group or t % tm == 0), (n, t, tm)
    tiles_per_seq = max(t // tm, 1)
    inv = ROPE_BASE ** (-jnp.arange(0, RET_DK, 2, dtype=F32) / RET_DK)
    ang = jnp.tile(pos, max(tm // t, 1))[:, None] * inv[None, :]
    cosf = jnp.concatenate([jnp.cos(ang), jnp.cos(ang)], axis=-1)
    sinf = jnp.concatenate([-jnp.sin(ang), jnp.sin(ang)], axis=-1)
    pos_blocks = cosf.shape[0] // tm
    rows = lambda cols: pl.BlockSpec((tm, cols), lambda i: (i, 0))
    if seq_is_group:
        hist = conv0.transpose(1, 0, 2)
        hist_spec = pl.BlockSpec((CONV_W - 1, tm // SUBLANES, CONV_CH), lambda i: (0, i, 0))
        tail_spec = hist_spec
        tail_shape = jax.ShapeDtypeStruct(hist.shape, F32)
    else:
        hist = jnp.pad(conv0, ((0, 0), (SUBLANES - (CONV_W - 1), 0), (0, 0)))
        hist_spec = pl.BlockSpec((1, SUBLANES, CONV_CH), lambda i: (i // tiles_per_seq, 0, 0))
        tail_spec = pl.BlockSpec((1, SUBLANES, CONV_CH), lambda i: (i, 0, 0))
        tail_shape = jax.ShapeDtypeStruct((n // tm, SUBLANES, CONV_CH), F32)
    pad_lanes = lambda x: jnp.pad(x.reshape(1, GDN_HEADS), ((0, 0), (0, LANES - GDN_HEADS)))
    pos_spec = pl.BlockSpec((tm, RET_DK), lambda i: (i % pos_blocks, 0))
    qk, v, conv, gb, tail, h = pl.pallas_call(
        functools.partial(_in_proj_kernel, seq_is_group=seq_is_group, tiles_per_seq=tiles_per_seq),
        grid=(n // tm,),
        in_specs=[
            rows(D_MODEL),
            _const_spec((1, D_MODEL)),
            _w_in_block(QKV_W, 0),
            _w_in_block(CONV_CH, 1),
            _w_in_block(LANES, (QKV_W + D_MODEL + CONV_CH) // LANES),
            pos_spec,
            pos_spec,
            _const_spec((CONV_W, CONV_CH)),
            hist_spec,
            _const_spec((1, LANES)),
            _const_spec((1, LANES)),
        ],
        out_specs=[rows(2 * RET_QK), rows(RET_V), rows(CONV_CH), rows(AB_W), tail_spec, rows(D_MODEL)],
        out_shape=[
            jax.ShapeDtypeStruct((n, 2 * RET_QK), BF16),
            jax.ShapeDtypeStruct((n, RET_V), BF16),
            jax.ShapeDtypeStruct((n, CONV_CH), F32),
            jax.ShapeDtypeStruct((n, AB_W), F32),
            tail_shape,
            jax.ShapeDtypeStruct((n, D_MODEL), BF16),
        ],
        scratch_shapes=[pltpu.VMEM((tm, D_MODEL), BF16), pltpu.VMEM((SUBLANES, CONV_CH), F32)],
        compiler_params=_params(("arbitrary",)),
        name="in_proj",
    )(x2d, ln1, w_bf, w_bf, w_bf, cosf, sinf, conv_w, hist, pad_lanes(a_log), pad_lanes(dt_bias))
    if seq_is_group:
        tail = tail.transpose(1, 0, 2)
    else:
        tail = tail.reshape(nb, tiles_per_seq, SUBLANES, CONV_CH)[:, -1, SUBLANES - (CONV_W - 1):, :]
    return qk, v, conv, gb, tail, h


def _step_layout(nb, t, c):
    ns = STEP_ROWS // c
    if ns == 1:
        par = max(p for p in (8, 4, 2, 1) if nb % p == 0)
        return nb, t, ns, par
    assert t == c and nb % ns == 0, (nb, t, c)
    return nb // ns, STEP_ROWS, ns, 1


def _retention_stages(q_ref, k_ref, v_ref, dintra_ref, dq_ref, dk_ref, dc_ref, o_ref, s_ref, *, par, ns, c):
    units = [(p, h) for p in range(par) for h in range(RET_HEADS)]
    ids = range(len(units))
    seqs = range(ns)
    rows = lambda x, j: x[j * c:(j + 1) * c]
    dk_sl = lambda h: slice(h * RET_DK, (h + 1) * RET_DK)
    dv_sl = lambda h: slice(h * RET_DV, (h + 1) * RET_DV)
    st = {}

    def scores():
        st["qb"] = [q_ref[p, :, dk_sl(h)] for p, h in units]
        kb = [k_ref[p, :, dk_sl(h)] for p, h in units]
        st["vb"] = [v_ref[p, :, dv_sl(h)] for p, h in units]
        st["scores"] = [_dot_nt(st["qb"][u], kb[u]) * dintra_ref[units[u][1]] for u in ids]
        st["kd"] = [kb[u].astype(F32) * dk_ref[units[u][1]] for u in ids]

    def state_read():
        qb = st["qb"]
        if ns == 1:
            st["s"] = [s_ref[p, h] for p, h in units]
            st["qs"] = [_dot(qb[u], _bf(st["s"][u])) for u in ids]
        else:
            q32 = [qb[u].astype(F32) for u in ids]
            st["s"] = [[s_ref[j, h] for j in seqs] for _, h in units]
            st["qs"] = [_rows([_dot(_bf(rows(q32[u], j)), _bf(st["s"][u][j])) for j in seqs]) for u in ids]

    def state_write():
        s_, qs, kd, vb, sc = st["s"], st["qs"], st["kd"], st["vb"], st["scores"]
        if ns == 1:
            upd = [_dot(_bf(_rows([kd[u].T, sc[u]])), vb[u]) for u in ids]
            for u, (p, h) in enumerate(units):
                o_ref[p, :, dv_sl(h)] = upd[u][RET_DK:] + qs[u] * dq_ref[h]
                s_ref[p, h] = s_[u] * dc_ref[h] + upd[u][:RET_DK]
        else:
            v32 = [vb[u].astype(F32) for u in ids]
            intra = [_dot(_bf(sc[u]), vb[u]) for u in ids]
            for u, (p, h) in enumerate(units):
                o_ref[p, :, dv_sl(h)] = intra[u] + qs[u] * dq_ref[h]
            ktv = [[_dot_tn(_bf(rows(kd[u], j)), _bf(rows(v32[u], j))) for j in seqs] for u in ids]
            for u, (p, h) in enumerate(units):
                for j in seqs:
                    s_ref[j, h] = s_[u][j] * dc_ref[h] + ktv[u][j]

    return [scores, state_read, state_write]


def _retention_consts(c):
    log_g = jnp.log1p(-jnp.exp2(-5.0 - jnp.arange(RET_HEADS, dtype=F32)))
    idx = jnp.arange(STEP_ROWS)
    same = (idx[:, None] // c) == (idx[None, :] // c)
    off = (idx % c).astype(F32)
    diff = off[:, None] - off[None, :]
    causal = same & (diff >= 0)
    d_intra = jnp.where(causal[None], jnp.exp(log_g[:, None, None] * jnp.where(causal, diff, 0.0)[None]), 0.0)
    d_q = jnp.broadcast_to(jnp.exp(log_g[:, None] * (off + 1.0)[None, :])[..., None], (RET_HEADS, STEP_ROWS, RET_DV))
    d_k = jnp.broadcast_to(jnp.exp(log_g[:, None] * (c - 1.0 - off)[None, :])[..., None],
                           (RET_HEADS, STEP_ROWS, RET_DK))
    d_c = jnp.broadcast_to(jnp.exp(log_g * c)[:, None, None], (RET_HEADS, 1, RET_DV))
    return d_intra, d_q, d_k, d_c


def _block_diag(x2):
    m = x2.shape[0]
    r = lax.broadcasted_iota(jnp.int32, (2 * m, 2 * m), 0)
    cc = lax.broadcasted_iota(jnp.int32, (2 * m, 2 * m), 1)
    return jnp.where((r < m) == (cc < m), _rows([x2, x2]), jnp.zeros((), x2.dtype))


def _solve_correction(a, c):
    ids = range(len(a))
    m = a[0].shape[0]
    n = [-x for x in a]
    ab = [_bf(x) for x in a]
    p = [_dot(ab[i], _block_diag(ab[i])) for i in ids]
    span = 2
    while span < c:
        pb = [_block_diag(_bf(x)) for x in p]
        last = 2 * span >= c
        if last:
            prod = [_dot(_bf(n[i]), pb[i]) for i in ids]
            n = [n[i] + p[i] + prod[i] for i in ids]
        else:
            prod = [_dot(_bf(_rows([n[i], p[i]])), pb[i]) for i in ids]
            n = [n[i] + p[i] + prod[i][:m] for i in ids]
            p = [prod[i][m:] for i in ids]
        span *= 2
    return n


def _recurrence_kernel(*refs, par, ns, c, zero_init):
    (qkv_ref, gb_ref, q_ref, k_ref, v_ref, dintra_ref, dq_ref, dk_ref, dc_ref), refs = refs[:9], refs[9:]
    init_refs, (o_ref, s_ref, oret_ref, sret_ref) = refs[:-4], refs[-4:]

    @pl.when(pl.program_id(1) == 0)
    def _():
        if zero_init:
            s_ref[...] = jnp.zeros_like(s_ref)
            sret_ref[...] = jnp.zeros_like(sret_ref)
        else:
            s_ref[...] = init_refs[0][...]
            sret_ref[...] = init_refs[1][...]

    retention = _retention_stages(q_ref, k_ref, v_ref, dintra_ref, dq_ref, dk_ref, dc_ref, oret_ref, sret_ref,
                                  par=par, ns=ns, c=c)
    m = STEP_ROWS

    def l2norm(x):
        return x * lax.rsqrt(jnp.sum(x * x, axis=-1, keepdims=True) + EPS)

    lane = lax.broadcasted_iota(jnp.int32, (m, LANES), 1)
    off = lax.broadcasted_iota(jnp.int32, (m, LANES), 0) % c
    r2 = lax.broadcasted_iota(jnp.int32, (m, 2 * m), 0)
    lane2 = lax.broadcasted_iota(jnp.int32, (m, 2 * m), 1)
    c2 = lane2 % m
    same = (r2 // c) == (c2 // c)
    tril = same & (r2 >= c2)
    strict = same & (r2 > c2)
    even = lane2 < m
    pick = (lax.broadcasted_iota(jnp.int32, (2 * m, 2 * GDN_DK), 0) < m) == (
        lax.broadcasted_iota(jnp.int32, (2 * m, 2 * GDN_DK), 1) < GDN_DK)

    def gates(p):
        cum = jnp.where(lane < GDN_HEADS, gb_ref[p, :, :LANES], 0.0)
        shift = 1
        while shift < c:
            cum = cum + jnp.where(off >= shift, pltpu.roll(cum, shift, 0), 0.0)
            shift *= 2
        last = cum.reshape(ns, c, LANES)[:, c - 1:c, :]
        e_rest = jnp.exp((last - cum.reshape(ns, c, LANES)).reshape(m, LANES))
        return dict(cum=cum, cum_t=cum.T, e_cum=jnp.exp(cum), e_rest=e_rest, e_last=jnp.exp(last),
                    beta=gb_ref[p, :, LANES:])

    def padded(x, h):
        z = jnp.zeros_like(x)
        return _rows([x, z]) if h % 2 == 0 else _rows([z, x])

    gt = [gates(p) for p in range(par)]
    units = [(p, h) for p in range(par) for h in range(GDN_HEADS)]
    ids = range(len(units))
    pairs = range(len(units) // 2)
    pair_of = lambda u: u // 2
    both = lambda xs, i: jnp.concatenate([xs[2 * i], xs[2 * i + 1]], axis=-1)
    col = lambda x, h: x[:, h:h + 1]
    q = [l2norm(qkv_ref[p, :, h * GDN_DK:(h + 1) * GDN_DK]) * (GDN_DK ** -0.5) for p, h in units]
    k = [l2norm(qkv_ref[p, :, GDN_QK + h * GDN_DK:GDN_QK + (h + 1) * GDN_DK]) for p, h in units]
    v = [qkv_ref[p, :, 2 * GDN_QK + h * GDN_DV:2 * GDN_QK + (h + 1) * GDN_DV] for p, h in units]
    beta = [col(gt[p]["beta"], BETA_LANE0 + h) for p, h in units]
    eg = [col(gt[p]["e_cum"], h) for p, h in units]
    cum_col = [col(gt[p]["cum"], h) for p, h in units]
    cum_row = [gt[p]["cum_t"][h:h + 1, :] for p, h in units]
    decay = [jnp.exp(jnp.where(tril, jnp.where(even, cum_col[2 * i], cum_col[2 * i + 1]) - both(cum_row, i), -jnp.inf))
             for i in pairs]
    kb = [k[u] * beta[u] for u in ids]
    kpick = [jnp.where(pick, _rows([_bf(both(k, i))] * 2), jnp.zeros((), BF16)) for i in pairs]
    raw = [_dot_nt(_bf(_rows([both(kb, i), both(q, i)])), kpick[i]) for i in pairs]
    a = [jnp.where(strict, raw[i][:m] * decay[i], 0.0) for i in pairs]
    attn = [raw[i][m:] * decay[i] for i in pairs]
    retention[0]()
    n = _solve_correction(a, c)
    retention[1]()
    rhs = [jnp.concatenate([v[u] * beta[u], kb[u] * eg[u]], axis=-1) for u in ids]
    nb16 = [_bf(x) for x in n]
    sol = [rhs[u] + _dot(nb16[pair_of(u)], padded(_bf(rhs[u]), units[u][1])) for u in ids]
    uu = [sol[u][:, :GDN_DV] for u in ids]
    w = [sol[u][:, GDN_DV:] for u in ids]
    qe = [q[u] * eg[u] for u in ids]
    kd = [k[u] * col(gt[p]["e_rest"], h) for u, (p, h) in enumerate(units)]
    retention[2]()
    dv_sl = lambda h: slice(h * GDN_DV, (h + 1) * GDN_DV)
    if ns == 1:
        s = [s_ref[p, h] for p, h in units]
        ws_qs = [_dot(_bf(_rows([w[u], qe[u]])), _bf(s[u])) for u in ids]
        v_new = [uu[u] - ws_qs[u][:m] for u in ids]
        upd = [_dot(_bf(_rows([padded(kd[u], h).T, attn[pair_of(u)]])), padded(_bf(v_new[u]), h))
               for u, (p, h) in enumerate(units)]
        for u, (p, h) in enumerate(units):
            o_ref[p, :, dv_sl(h)] = ws_qs[u][m:] + upd[u][GDN_DK:]
            s_ref[p, h] = s[u] * gt[p]["e_last"][0, :, h:h + 1] + upd[u][:GDN_DK]
    else:
        seqs = range(ns)
        rows = lambda x, j: x[j * c:(j + 1) * c]
        s = [[s_ref[j, h] for j in seqs] for _, h in units]
        ws_qs = [[_dot(_bf(_rows([rows(w[u], j), rows(qe[u], j)])), _bf(s[u][j])) for j in seqs] for u in ids]
        v_new = [_rows([rows(uu[u], j) - ws_qs[u][j][:c] for j in seqs]) for u in ids]
        qs = [_rows([ws_qs[u][j][c:] for j in seqs]) for u in ids]
        intra = [_dot(_bf(attn[pair_of(u)]), padded(_bf(v_new[u]), units[u][1])) for u in ids]
        for u, (p, h) in enumerate(units):
            o_ref[p, :, dv_sl(h)] = qs[u] + intra[u]
        ktv = [[_dot_tn(_bf(rows(kd[u], j)), _bf(rows(v_new[u], j))) for j in seqs] for u in ids]
        for u, (p, h) in enumerate(units):
            for j in seqs:
                s_ref[j, h] = s[u][j] * gt[p]["e_last"][j, :, h:h + 1] + ktv[u][j]


def _recurrences(qk, v, conv, gb, s_ret, s_gdn, nb, t, c):
    groups, rows, ns, par = _step_layout(nb, t, c)
    zero_init = s_ret is None
    blk = lambda cols, col_blk=0: pl.BlockSpec((par, STEP_ROWS, cols), lambda g, i: (g, i, col_blk))
    ret_state = pl.BlockSpec((par * ns, RET_HEADS, RET_DK, RET_DV), lambda g, i: (g, 0, 0, 0))
    gdn_state = pl.BlockSpec((par * ns, GDN_HEADS, GDN_DK, GDN_DV), lambda g, i: (g, 0, 0, 0))
    view = lambda x: x.reshape(groups, rows, x.shape[-1])
    qk3 = view(qk)
    o_g, gdn_new, o_r, ret_new = pl.pallas_call(
        functools.partial(_recurrence_kernel, par=par, ns=ns, c=c, zero_init=zero_init),
        grid=(groups // par, rows // STEP_ROWS),
        in_specs=[
            blk(CONV_CH), blk(AB_W), blk(RET_QK, 0), blk(RET_QK, 1), blk(RET_V),
            _const_spec((RET_HEADS, STEP_ROWS, STEP_ROWS)),
            _const_spec((RET_HEADS, STEP_ROWS, RET_DV)),
            _const_spec((RET_HEADS, STEP_ROWS, RET_DK)),
            _const_spec((RET_HEADS, 1, RET_DV)),
        ] + ([] if zero_init else [gdn_state, ret_state]),
        out_specs=[blk(GDN_V), gdn_state, blk(RET_V), ret_state],
        out_shape=[
            jax.ShapeDtypeStruct((groups, rows, GDN_V), F32),
            jax.ShapeDtypeStruct((nb, GDN_HEADS, GDN_DK, GDN_DV), F32),
            jax.ShapeDtypeStruct((groups, rows, RET_V), F32),
            jax.ShapeDtypeStruct((nb, RET_HEADS, RET_DK, RET_DV), F32),
        ],
        compiler_params=_params(("arbitrary", "arbitrary")),
        name="recurrences",
    )(view(conv), view(gb), qk3, qk3, view(v), *_retention_consts(c), *(() if zero_init else (s_gdn, s_ret)))
    return o_r.reshape(nb * t, RET_V), o_g.reshape(nb * t, GDN_V), ret_new, gdn_new


def _attn_out_kernel(x_ref, h_ref, or_ref, og_ref, wgr_ref, wzgg_ref, rn_ref, gn_ref, wr_ref, wg_ref, wo_ref,
                     o_ref, lhs_ref, br_ref):
    tm = x_ref.shape[0]
    slabs = [slice(s * MXU_COLS, (s + 1) * MXU_COLS) for s in range(D_MODEL // MXU_COLS)]
    row_blocks = [slice(b, b + EPILOGUE_ROWS) for b in range(0, tm, EPILOGUE_ROWS)]

    def gate(block, sl):
        if block == 0:
            return _dot(h_ref[...], wgr_ref[:, sl])
        return _dot(h_ref[...], wzgg_ref[:, (block - 1) * D_MODEL + sl.start:(block - 1) * D_MODEL + sl.stop])

    def branch(src_ref, head_dim, gain_ref, block, w_ref):
        for sl in slabs:
            g = gate(block, sl)
            for rows in row_blocks:
                o = src_ref[rows, sl]
                heads = [o[:, k:k + head_dim] for k in range(0, MXU_COLS, head_dim)]
                normed = [oh * lax.rsqrt(jnp.mean(oh * oh, axis=-1, keepdims=True) + EPS) for oh in heads]
                lhs_ref[rows, sl] = _bf(jnp.concatenate(normed, axis=-1) * gain_ref[:, sl] * _silu(g[rows]))
        return _dot(lhs_ref[...], w_ref[...])

    br_ref[0] = branch(or_ref, RET_DV, rn_ref, 0, wr_ref)
    br_ref[1] = branch(og_ref, GDN_DV, gn_ref, 1, wg_ref)
    for sl in slabs:
        g_r = gate(2, sl)
        g_g = gate(3, sl)
        for rows in row_blocks:
            lhs_ref[rows, sl] = _bf(jax.nn.sigmoid(g_r[rows]) * br_ref[0, rows, sl]
                                    + jax.nn.sigmoid(g_g[rows]) * br_ref[1, rows, sl])
    o_ref[...] = x_ref[...] + _dot(lhs_ref[...], wo_ref[...])


def _attn_out(x2d, h, o_r, o_g, w_bf, w_zgg, ret_norm, gdn_norm, w_ret_br, w_gdn_br, w_out):
    n = x2d.shape[0]
    tm = min(512, n)
    rows = pl.BlockSpec((tm, D_MODEL), lambda i: (i, 0))
    vec = _const_spec((1, D_MODEL))
    wspec = _const_spec((D_MODEL, D_MODEL))
    return pl.pallas_call(
        _attn_out_kernel,
        grid=(n // tm,),
        in_specs=[rows, rows, rows, rows, _w_in_block(D_MODEL, 2), _const_spec((D_MODEL, GATE_W - D_MODEL)), vec, vec,
                  wspec, wspec, wspec],
        out_specs=rows,
        out_shape=jax.ShapeDtypeStruct((n, D_MODEL), F32),
        scratch_shapes=[pltpu.VMEM((tm, D_MODEL), BF16), pltpu.VMEM((2, tm, D_MODEL), F32)],
        compiler_params=_params(("arbitrary",)),
        name="attn_out",
    )(x2d, h, o_r, o_g, w_bf, w_zgg, ret_norm, gdn_norm, w_ret_br, w_gdn_br, w_out)


FF_BLOCK = 1024


def _mlp_kernel(x_ref, ln2_ref, wu_ref, wd_ref, lnf_ref, o_ref):
    x = x_ref[...]
    hb = _bf(_rms(x, ln2_ref[...]))
    acc = x
    for f in range(D_FF // FF_BLOCK):
        sl = slice(f * FF_BLOCK, (f + 1) * FF_BLOCK)
        up = jnp.maximum(_dot(hb, wu_ref[:, sl]), 0.0)
        acc = acc + _dot(_bf(up * up), wd_ref[sl, :])
    o_ref[...] = _rms(acc, lnf_ref[...])


def _mlp(x2d, ln2, w_up, w_down, ln_f):
    n = x2d.shape[0]
    tm = min(1024, n)
    rows = pl.BlockSpec((tm, D_MODEL), lambda i: (i, 0))
    return pl.pallas_call(
        _mlp_kernel,
        grid=(n // tm,),
        in_specs=[rows, _const_spec((1, D_MODEL)), _const_spec((D_MODEL, D_FF)), _const_spec((D_FF, D_MODEL)),
                  _const_spec((1, D_MODEL))],
        out_specs=rows,
        out_shape=jax.ShapeDtypeStruct((n, D_MODEL), F32),
        compiler_params=_params(("arbitrary",)),
        name="mlp",
    )(x2d, ln2, w_up, w_down, ln_f)


def _group(x, pos, s_ret, s_gdn, s_conv, wts):
    (in_wts, w_zgg, ret_norm, gdn_norm, w_ret_br, w_gdn_br, w_out, ln2, w_up, w_down, ln_f) = wts
    nb, t, _ = x.shape
    c = CHUNK if t % CHUNK == 0 else t
    assert STEP_ROWS % c == 0 and nb % (STEP_ROWS // c) == 0, (nb, t)
    x2d = x.reshape(nb * t, D_MODEL)
    qk, v, conv, gb, conv_new, h = _in_proj(x2d, t, pos, s_conv, in_wts)
    o_r, o_g, ret_new, gdn_new = _recurrences(qk, v, conv, gb, s_ret, s_gdn, nb, t, c)
    x1 = _attn_out(x2d, h, o_r, o_g, in_wts[1], w_zgg, ret_norm, gdn_norm, w_ret_br, w_gdn_br, w_out)
    y = _mlp(x1, ln2, w_up, w_down, ln_f)
    return y.reshape(x.shape), ret_new, gdn_new, conv_new


def kernel(x_prompt, x_sample, state_ret, state_gdn, state_conv, ln1, w_in, conv_w, a_log, dt_bias, ret_norm,
           gdn_norm, w_ret_br, w_gdn_br, w_out, ln2, w_up, w_down, ln_f):
    depth = w_in.shape[0]
    assert depth == 1, "single-layer trunk"
    bp, tp, _ = x_prompt.shape
    ts = x_sample.shape[1]
    w = w_in[0]
    o_gr = 2 * RET_QK + RET_V
    o_qkv = o_gr + RET_V
    o_a = o_qkv + CONV_CH
    o_z = o_a + 2 * GDN_HEADS
    assert (o_gr, o_qkv) == (QKV_W, CONV_CH)
    w_bf = w.astype(BF16)
    w_zgg = w_bf[:, o_z:]
    vec = lambda v: v.reshape(1, -1)
    in_wts = (vec(ln1[0]), w_bf, conv_w[0], a_log[0], dt_bias[0])
    wts = (in_wts, w_zgg, vec(ret_norm[0]), vec(gdn_norm[0]), w_ret_br[0].astype(BF16), w_gdn_br[0].astype(BF16),
           w_out[0].astype(BF16), vec(ln2[0]), w_up[0].astype(BF16), w_down[0].astype(BF16), vec(ln_f))
    pos_p = jnp.arange(tp, dtype=F32)
    pos_s = PAST_LEN + jnp.arange(ts, dtype=F32)
    yp, ret_p, gdn_p, conv_p = _group(x_prompt, pos_p, None, None, jnp.zeros((bp, CONV_W - 1, CONV_CH), F32), wts)
    ys, ret_s, gdn_s, conv_s = _group(x_sample, pos_s, state_ret[0], state_gdn[0], state_conv[0], wts)
    return (yp, ys, ret_p[None], gdn_p[None], conv_p[None], ret_s[None], gdn_s[None], conv_s[None])
```
